```python
import jax
import jax.numpy as jnp
from jax import lax
import numpy as np

D_MODEL = 1024
BATCH = 4
SEQ = 4096
DEPTH = 4

N_MIXERS = 3
N_CONV_LAYERS = (DEPTH + 2) // 3
N_LRU_LAYERS = (DEPTH + 1) // 3
N_FOX_LAYERS = DEPTH // 3
CONV_WIDTH = 3
LRU_WIDTH = D_MODEL
LRU_HEADS = 4
LRU_BLOCK = LRU_WIDTH // LRU_HEADS
LRU_CONV_WIDTH = 4
LRU_C = 8.0
FOX_HEADS = 16
FOX_HEAD_DIM = D_MODEL // FOX_HEADS
Q_BLOCK = 128
N_EXPERTS = 32
TOP_K = 4
D_EXPERT = D_MODEL
SWIGLU_LIMIT = 7.0
SWIGLU_ALPHA = 1.702
EXPERT_ROWS = 128
RMS_EPS = 1e-6
NEG_INF = -1e30

kernel_name = "hybrid_conv_rglru_fox_moe_adaln"


def rms_norm(x, g):
    xf = x.astype(jnp.float32)
    y = xf * lax.rsqrt(jnp.mean(xf * xf, axis=-1, keepdims=True) + RMS_EPS)
    return (y * g.astype(jnp.float32)).astype(x.dtype)


def modulate(h, shift, scale):
    return h * (1.0 + scale) + shift


def causal_depthwise_conv(u, w):
    k_w = w.shape[0]
    s_len = u.shape[1]
    up = jnp.pad(u, ((0, 0), (k_w - 1, 0), (0, 0)))
    out = up[:, 0:s_len, :] * w[0]
    for k in range(1, k_w):
        out = out + up[:, k:k + s_len, :] * w[k]
    return out


def short_conv_mixer(h, w_in, conv_w, w_out):
    b_gate, c_gate, v = jnp.split(h @ w_in, 3, axis=-1)
    return (b_gate * causal_depthwise_conv(c_gate * v, conv_w)) @ w_out


def _linear_recurrence_combine(left, right):
    a1, b1 = left
    a2, b2 = right
    return a1 * a2, a2 * b1 + b2


def rglru_mixer(h, w_in, conv_w, conv_b, gate_w, gate_b, a_param, w_out):
    f32 = jnp.float32
    bsz, s_len, _ = h.shape
    g_branch, x_branch = jnp.split(h @ w_in, 2, axis=-1)
    xc = (causal_depthwise_conv(x_branch, conv_w) + conv_b).astype(f32)
    xh = xc.reshape(bsz, s_len, LRU_HEADS, LRU_BLOCK)
    g = jnp.einsum('bshi,hij->bshj', xh, gate_w.astype(f32)) + gate_b.astype(f32)
    r_gate, i_gate = jnp.split(g, 2, axis=-1)
    r_gate = jax.nn.sigmoid(r_gate).reshape(bsz, s_len, LRU_WIDTH)
    i_gate = jax.nn.sigmoid(i_gate).reshape(bsz, s_len, LRU_WIDTH)
    log_a = -LRU_C * r_gate * jax.nn.softplus(a_param.astype(f32))
    a = jnp.exp(log_a)
    mult = jnp.sqrt(-jnp.expm1(2.0 * log_a))
    u = mult * (i_gate * xc)
    _, hs = lax.associative_scan(_linear_recurrence_combine, (a, u), axis=1)
    y = (jax.nn.gelu(g_branch.astype(f32)) * hs).astype(h.dtype)
    return y @ w_out


def forgetting_attention_mixer(h, w_in, b_f, w_out):
    f32 = jnp.float32
    bsz, s_len, d = h.shape
    q, k, v, f = jnp.split(h @ w_in, [d, 2 * d, 3 * d], axis=-1)

    def heads(t):
        return t.reshape(bsz, s_len, FOX_HEADS, FOX_HEAD_DIM).transpose(0, 2, 1, 3)

    q, k, v = heads(q), heads(k), heads(v)
    log_f = jax.nn.log_sigmoid((f + b_f).astype(f32))
    cum = jnp.cumsum(log_f, axis=1).transpose(0, 2, 1)
    nb = s_len // Q_BLOCK
    q_blocks = jnp.moveaxis(q.reshape(bsz, FOX_HEADS, nb, Q_BLOCK, FOX_HEAD_DIM), 2, 0)
    c_blocks = jnp.moveaxis(cum.reshape(bsz, FOX_HEADS, nb, Q_BLOCK), 2, 0)
    k_pos = jnp.arange(s_len)
    scale = FOX_HEAD_DIM ** -0.5

    def attend_block(args):
        qb, cb, bi = args
        s = jnp.einsum('bhqd,bhkd->bhqk', qb, k).astype(f32) * scale
        s = s + (cb[..., :, None] - cum[..., None, :])
        q_pos = bi * Q_BLOCK + jnp.arange(Q_BLOCK)
        s = jnp.where(k_pos[None, :] <= q_pos[:, None], s, NEG_INF)
        p = jax.nn.softmax(s, axis=-1)
        return jnp.einsum('bhqk,bhkd->bhqd', p.astype(v.dtype), v)

    o = lax.map(attend_block, (q_blocks, c_blocks, jnp.arange(nb)))
    o = jnp.moveaxis(o, 0, 2).reshape(bsz, FOX_HEADS, s_len, FOX_HEAD_DIM)
    o = o.transpose(0, 2, 1, 3).reshape(bsz, s_len, d)
    return o @ w_out


def moe_ffn(h, router_w, router_b, w_gate_up, b_gate_up, w_down, b_down):
    f32 = jnp.float32
    n_tok, d = h.shape
    n_assign = n_tok * TOP_K
    logits = h.astype(f32) @ router_w.astype(f32) + router_b.astype(f32)
    top_v, top_e = lax.top_k(logits, TOP_K)
    gates = jax.nn.softmax(top_v, axis=-1)
    flat_e = top_e.reshape(-1)
    flat_tok = jnp.repeat(jnp.arange(n_tok, dtype=jnp.int32), TOP_K)
    order = jnp.argsort(flat_e)
    se = flat_e[order]
    st = flat_tok[order]
    sg = gates.reshape(-1)[order]
    counts = jnp.bincount(flat_e, length=N_EXPERTS)
    padded = (counts + EXPERT_ROWS - 1) // EXPERT_ROWS * EXPERT_ROWS
    start = jnp.cumsum(counts) - counts
    pad_end = jnp.cumsum(padded)
    pad_start = pad_end - padded
    dest = pad_start[se] + jnp.arange(n_assign) - start[se]
    n_blocks = (n_assign + EXPERT_ROWS - 1) // EXPERT_ROWS + N_EXPERTS
    n_rows = n_blocks * EXPERT_ROWS
    row_tok = jnp.full((n_rows,), n_tok, dtype=jnp.int32).at[dest].set(st)
    block_e = jnp.minimum(
        jnp.searchsorted(pad_end, jnp.arange(n_blocks) * EXPERT_ROWS, side='right'),
        N_EXPERTS - 1)
    h_pad = jnp.concatenate([h, jnp.zeros((1, d), h.dtype)], axis=0)
    xb = h_pad[row_tok].reshape(n_blocks, EXPERT_ROWS, d)

    def expert_block(args):
        xr, e = args
        gu = xr @ w_gate_up[e] + b_gate_up[e]
        gate, up = jnp.split(gu, 2, axis=-1)
        gate = jnp.minimum(gate, SWIGLU_LIMIT)
        up = jnp.clip(up, -SWIGLU_LIMIT, SWIGLU_LIMIT)
        act = gate * jax.nn.sigmoid(SWIGLU_ALPHA * gate) * (up + 1.0)
        return act @ w_down[e] + b_down[e]

    yb = lax.map(expert_block, (xb, block_e)).reshape(n_rows, d)
    y = jnp.zeros((n_tok, d), f32).at[st].add(yb[dest].astype(f32) * sg[:, None])
    return y.astype(h.dtype)


def setup_inputs(seed: int = 0) -> dict:
    key = jax.random.key(seed)
    ks = jax.random.split(key, 32)
    f32 = jnp.float32
    D = D_MODEL
    W = LRU_WIDTH
    H = FOX_HEADS
    E = N_EXPERTS
    F = D_EXPERT

    def nrm(k, shape, std):
        return std * jax.random.normal(k, shape, f32)

    u = jax.random.uniform(ks[14], (N_LRU_LAYERS, W), f32, 0.9 ** 2, 0.999 ** 2)
    lru_a_param = jnp.log(jnp.expm1(-0.5 * jnp.log(u)))
    return {
        "x": nrm(ks[0], (BATCH, SEQ, D), 1.0),
        "c": nrm(ks[1], (BATCH, D), 1.0),
        "norm_mix_g": 1.0 + nrm(ks[2], (DEPTH, D), 0.1),
        "norm_ffn_g": 1.0 + nrm(ks[3], (DEPTH, D), 0.1),
        "w_mod": nrm(ks[4], (DEPTH, D, 6 * D), 0.5 * D ** -0.5),
        "b_mod": nrm(ks[5], (DEPTH, 6 * D), 0.02),
        "conv_w_in": nrm(ks[6], (N_CONV_LAYERS, D, 3 * D), D ** -0.5),
        "conv_w": nrm(ks[7], (N_CONV_LAYERS, CONV_WIDTH, D), CONV_WIDTH ** -0.5),
        "conv_w_out": nrm(ks[8], (N_CONV_LAYERS, D, D), D ** -0.5),
        "lru_w_in": nrm(ks[9], (N_LRU_LAYERS, D, 2 * W), D ** -0.5),
        "lru_conv_w": nrm(ks[10], (N_LRU_LAYERS, LRU_CONV_WIDTH, W), LRU_CONV_WIDTH ** -0.5),
        "lru_conv_b": nrm(ks[11], (N_LRU_LAYERS, W), 0.02),
        "lru_gate_w": nrm(ks[12], (N_LRU_LAYERS, LRU_HEADS, LRU_BLOCK, 2 * LRU_BLOCK), LRU_BLOCK ** -0.5),
        "lru_gate_b": nrm(ks[13], (N_LRU_LAYERS, LRU_HEADS, 2 * LRU_BLOCK), 0.02),
        "lru_a_param": lru_a_param,
        "lru_w_out": nrm(ks[15], (N_LRU_LAYERS, W, D), W ** -0.5),
        "fox_w_in": nrm(ks[16], (N_FOX_LAYERS, D, 3 * D + H), D ** -0.5),
        "fox_b_f": jax.random.uniform(ks[17], (N_FOX_LAYERS, H), f32, 2.0, 6.0),
        "fox_w_out": nrm(ks[18], (N_FOX_LAYERS, D, D), D ** -0.5),
        "router_w": nrm(ks[19], (DEPTH, D, E), D ** -0.5),
        "router_b": nrm(ks[20], (DEPTH, E), 0.01),
        "moe_w_gate_up": nrm(ks[21], (DEPTH, E, D, 2 * F), D ** -0.5),
        "moe_b_gate_up": nrm(ks[22], (DEPTH, E, 2 * F), 0.02),
        "moe_w_down": nrm(ks[23], (DEPTH, E, F, D), F ** -0.5),
        "moe_b_down": nrm(ks[24], (DEPTH, E, D), 0.02),
        "final_g": 1.0 + nrm(ks[25], (D,), 0.1),
    }


def reference(x, c, norm_mix_g, norm_ffn_g, w_mod, b_mod,
              conv_w_in, conv_w, conv_w_out,
              lru_w_in, lru_conv_w, lru_conv_b, lru_gate_w, lru_gate_b, lru_a_param, lru_w_out,
              fox_w_in, fox_b_f, fox_w_out,
              router_w, router_b, moe_w_gate_up, moe_b_gate_up, moe_w_down, moe_b_down,
              final_g):
    bsz, s_len, d = x.shape
    c_act = jax.nn.silu(c)
    for i in range(DEPTH):
        mod = (c_act @ w_mod[i] + b_mod[i]).reshape(bsz, 6, d)[:, :, None, :]
        shift1, scale1, gate1 = mod[:, 0], mod[:, 1], mod[:, 2]
        shift2, scale2, gate2 = mod[:, 3], mod[:, 4], mod[:, 5]

        h = modulate(rms_norm(x, norm_mix_g[i]), shift1, scale1)
        kind = i % N_MIXERS
        j = i // N_MIXERS
        if kind == 0:
            y = short_conv_mixer(h, conv_w_in[j], conv_w[j], conv_w_out[j])
        elif kind == 1:
            y = rglru_mixer(h, lru_w_in[j], lru_conv_w[j], lru_conv_b[j],
                            lru_gate_w[j], lru_gate_b[j], lru_a_param[j], lru_w_out[j])
        else:
            y = forgetting_attention_mixer(h, fox_w_in[j], fox_b_f[j], fox_w_out[j])
        x = x + gate1 * y

        h = modulate(rms_norm(x, norm_ffn_g[i]), shift2, scale2)
        y = moe_ffn(h.reshape(bsz * s_len, d), router_w[i], router_b[i],
                    moe_w_gate_up[i], moe_b_gate_up[i], moe_w_down[i], moe_b_down[i])
        x = x + gate2 * y.reshape(bsz, s_len, d)
    return rms_norm(x, final_g)
```

```python
import functools

import jax
import jax.numpy as jnp
from jax import lax
from jax.experimental import pallas as pl
from jax.experimental.pallas import tpu as pltpu

F32 = jnp.float32
BF16 = jnp.bfloat16
I32 = jnp.int32
HIGHEST = lax.Precision.HIGHEST

RMS_EPS = 1e-6
LRU_HEADS = 4
LRU_C = 8.0
FOX_HEADS = 16
N_EXPERTS = 32
TOP_K = 4
SWIGLU_LIMIT = 7.0
SWIGLU_ALPHA = 1.702
NEG_BIG = -1e30
GELU_C = 0.7978845608028654

LANES = 128
SUBLANES = 8
VMEM_LIMIT = 56 * 1024 * 1024

ROW_TILE = 512
EXPERT_TILE = 256
DISPATCH_TILE = 256
COMBINE_TILE = 128
ATTN_TILE = 512


def _params(*sem):
    return pltpu.CompilerParams(dimension_semantics=sem, vmem_limit_bytes=VMEM_LIMIT)


def _sigmoid(z):
    return 1.0 / (1.0 + jnp.exp(-z))


def _norm_mod(x, g, shift, scale):
    ms = jnp.mean(x * x, axis=-1, keepdims=True)
    y = x * lax.rsqrt(ms + RMS_EPS) * g
    return y * (1.0 + scale) + shift


def _shifted_rows(u, carry_ref, j, row):
    sh = pltpu.roll(u, j, 0)
    for r in range(j):
        src = SUBLANES - j + r
        sh = jnp.where(row == r, carry_ref[src:src + 1, :], sh)
    return sh


def _mod_kernel(c_ref, w_ref, b_ref, o_ref):
    c = c_ref[...]
    ca = c * _sigmoid(c)
    o_ref[0] = jnp.dot(ca, w_ref[0], precision=HIGHEST, preferred_element_type=F32) + b_ref[0]


def _modulation(c, w_mod, b_mod):
    depth, d, n = w_mod.shape
    bsz = c.shape[0]
    rows = -(-bsz // SUBLANES) * SUBLANES
    c_pad = jnp.zeros((rows, d), F32).at[:bsz].set(c)
    tn = n // 4
    out = pl.pallas_call(
        _mod_kernel,
        out_shape=jax.ShapeDtypeStruct((depth, rows, n), F32),
        grid=(depth, n // tn),
        in_specs=[
            pl.BlockSpec((rows, d), lambda l, j: (0, 0)),
            pl.BlockSpec((1, d, tn), lambda l, j: (l, 0, j)),
            pl.BlockSpec((1, 1, tn), lambda l, j: (l, 0, j)),
        ],
        out_specs=pl.BlockSpec((1, rows, tn), lambda l, j: (l, 0, j)),
        compiler_params=_params("parallel", "parallel"),
        name="adaln_modulation",
    )(c_pad, w_mod, b_mod.reshape(depth, 1, n))
    return out[:, :bsz].reshape(depth, bsz, 6, 1, d)


def _norm_matmul_kernel(x_ref, g_ref, sh_ref, sc_ref, w_ref, *out_refs, n_main):
    h = _norm_mod(x_ref[...], g_ref[...], sh_ref[...], sc_ref[...]).astype(BF16)
    y = jnp.dot(h, w_ref[...], preferred_element_type=F32)
    out_refs[0][...] = y[:, :n_main].astype(out_refs[0].dtype)
    if len(out_refs) > 1:
        out_refs[1][...] = y[:, n_main:]


def _norm_matmul(x, g, shift, scale, w, n_main, out_dtype, seq):
    t, d = x.shape
    n = w.shape[1]
    tm = ROW_TILE
    per_seq = seq // tm
    out_shape = [jax.ShapeDtypeStruct((t, n_main), out_dtype)]
    out_specs = [pl.BlockSpec((tm, n_main), lambda i: (i, 0))]
    if n > n_main:
        out_shape.append(jax.ShapeDtypeStruct((t, n - n_main), F32))
        out_specs.append(pl.BlockSpec((tm, n - n_main), lambda i: (i, 0)))
    return pl.pallas_call(
        functools.partial(_norm_matmul_kernel, n_main=n_main),
        out_shape=out_shape,
        grid=(t // tm,),
        in_specs=[
            pl.BlockSpec((tm, d), lambda i: (i, 0)),
            pl.BlockSpec((1, d), lambda i: (0, 0)),
            pl.BlockSpec((None, 1, d), lambda i: (i // per_seq, 0, 0)),
            pl.BlockSpec((None, 1, d), lambda i: (i // per_seq, 0, 0)),
            pl.BlockSpec((d, n), lambda i: (0, 0)),
        ],
        out_specs=out_specs,
        compiler_params=_params("parallel"),
        name="norm_mod_in_proj",
    )(x, g, shift, scale, w)


def _proj_residual_kernel(z_ref, w_ref, x_ref, gate_ref, o_ref):
    y = jnp.dot(z_ref[...].astype(BF16), w_ref[...], preferred_element_type=F32)
    o_ref[...] = x_ref[...] + gate_ref[...] * y


def _proj_residual(z, w, x, gate, seq):
    t, d = x.shape
    tm = ROW_TILE
    per_seq = seq // tm
    return pl.pallas_call(
        _proj_residual_kernel,
        out_shape=jax.ShapeDtypeStruct((t, d), F32),
        grid=(t // tm,),
        in_specs=[
            pl.BlockSpec((tm, z.shape[1]), lambda i: (i, 0)),
            pl.BlockSpec(w.shape, lambda i: (0, 0)),
            pl.BlockSpec((tm, d), lambda i: (i, 0)),
            pl.BlockSpec((None, 1, d), lambda i: (i // per_seq, 0, 0)),
        ],
        out_specs=pl.BlockSpec((tm, d), lambda i: (i, 0)),
        compiler_params=_params("parallel"),
        name="out_proj_residual",
    )(z, w, x, gate)


def _conv_mixer_kernel(b_ref, c_ref, v_ref, cw_ref, w_ref, x_ref, gate_ref, o_ref, carry_ref, *, per_seq):
    @pl.when(pl.program_id(0) % per_seq == 0)
    def _():
        carry_ref[...] = jnp.zeros_like(carry_ref)

    cv = c_ref[...] * v_ref[...]
    tm = cv.shape[0]
    k_w = cw_ref.shape[0]
    row = lax.broadcasted_iota(I32, cv.shape, 0)
    conv = cw_ref[k_w - 1:k_w, :] * cv
    for j in range(1, k_w):
        conv = conv + cw_ref[k_w - 1 - j:k_w - j, :] * _shifted_rows(cv, carry_ref, j, row)
    carry_ref[...] = cv[tm - SUBLANES:, :]
    z = (b_ref[...] * conv).astype(BF16)
    y = jnp.dot(z, w_ref[...], preferred_element_type=F32)
    o_ref[...] = x_ref[...] + gate_ref[...] * y


def _conv_mixer(bcv, conv_w, w_out, x, gate, seq):
    t, d = x.shape
    tm = ROW_TILE
    per_seq = seq // tm
    return pl.pallas_call(
        functools.partial(_conv_mixer_kernel, per_seq=per_seq),
        out_shape=jax.ShapeDtypeStruct((t, d), F32),
        grid=(t // tm,),
        in_specs=[
            pl.BlockSpec((tm, d), lambda i: (i, 0)),
            pl.BlockSpec((tm, d), lambda i: (i, 1)),
            pl.BlockSpec((tm, d), lambda i: (i, 2)),
            pl.BlockSpec(conv_w.shape, lambda i: (0, 0)),
            pl.BlockSpec(w_out.shape, lambda i: (0, 0)),
            pl.BlockSpec((tm, d), lambda i: (i, 0)),
            pl.BlockSpec((None, 1, d), lambda i: (i // per_seq, 0, 0)),
        ],
        out_specs=pl.BlockSpec((tm, d), lambda i: (i, 0)),
        scratch_shapes=[pltpu.VMEM((SUBLANES, d), F32)],
        compiler_params=_params("arbitrary"),
        name="short_conv_mixer",
    )(bcv, bcv, bcv, conv_w, w_out, x, gate)


def _lru_kernel(xb_ref, gb_ref, cw_ref, cb_ref, gw_ref, gbias_ref, ap_ref, y_ref, xcarry_ref, hcarry_ref):
    @pl.when(pl.program_id(2) == 0)
    def _():
        xcarry_ref[...] = jnp.zeros_like(xcarry_ref)
        hcarry_ref[...] = jnp.zeros_like(hcarry_ref)

    x = xb_ref[...]
    ts, blk = x.shape
    k_w = cw_ref.shape[0]
    row = lax.broadcasted_iota(I32, x.shape, 0)
    conv = cw_ref[k_w - 1:k_w, :] * x
    for j in range(1, k_w):
        conv = conv + cw_ref[k_w - 1 - j:k_w - j, :] * _shifted_rows(x, xcarry_ref, j, row)
    xcarry_ref[...] = x[ts - SUBLANES:, :]
    xc = conv + cb_ref[...]

    g = jnp.dot(xc.astype(BF16), gw_ref[...], preferred_element_type=F32) + gbias_ref[...]
    r_gate = _sigmoid(g[:, :blk])
    i_gate = _sigmoid(g[:, blk:])
    ap = ap_ref[...]
    softplus = jnp.maximum(ap, 0.0) + jnp.log(1.0 + jnp.exp(-jnp.abs(ap)))
    a = jnp.exp((-LRU_C) * r_gate * softplus)
    u = jnp.sqrt(1.0 - a * a) * (i_gate * xc)

    a_cum, h_loc = a, u
    step = 1
    while step < ts:
        a_prev = jnp.where(row < step, 1.0, pltpu.roll(a_cum, step, 0))
        h_prev = jnp.where(row < step, 0.0, pltpu.roll(h_loc, step, 0))
        h_loc = h_loc + a_cum * h_prev
        a_cum = a_cum * a_prev
        step *= 2
    hs = h_loc + a_cum * hcarry_ref[...]
    hcarry_ref[...] = hs[ts - 1:ts, :]

    gb = gb_ref[...]
    gelu = 0.5 * gb * (1.0 + jnp.tanh(GELU_C * (gb + 0.044715 * (gb * gb * gb))))
    y_ref[...] = (gelu * hs).astype(y_ref.dtype)


def _lru_core(gx, conv_w, conv_b, gate_w, gate_b, a_param, bsz, seq):
    t = gx.shape[0]
    width = gx.shape[1] // 2
    blk = width // LRU_HEADS
    ts = ROW_TILE
    per_seq = seq // ts
    return pl.pallas_call(
        _lru_kernel,
        out_shape=jax.ShapeDtypeStruct((t, width), BF16),
        grid=(bsz, LRU_HEADS, per_seq),
        in_specs=[
            pl.BlockSpec((ts, blk), lambda b, h, s: (b * per_seq + s, LRU_HEADS + h)),
            pl.BlockSpec((ts, blk), lambda b, h, s: (b * per_seq + s, h)),
            pl.BlockSpec((conv_w.shape[0], blk), lambda b, h, s: (0, h)),
            pl.BlockSpec((1, blk), lambda b, h, s: (0, h)),
            pl.BlockSpec((None, blk, 2 * blk), lambda b, h, s: (h, 0, 0)),
            pl.BlockSpec((None, 1, 2 * blk), lambda b, h, s: (h, 0, 0)),
            pl.BlockSpec((1, blk), lambda b, h, s: (0, h)),
        ],
        out_specs=pl.BlockSpec((ts, blk), lambda b, h, s: (b * per_seq + s, h)),
        scratch_shapes=[pltpu.VMEM((SUBLANES, blk), F32), pltpu.VMEM((1, blk), F32)],
        compiler_params=_params("parallel", "parallel", "arbitrary"),
        name="rglru_scan",
    )(gx, gx, conv_w, conv_b, gate_w, gate_b, a_param)


def _forget_cumsum_kernel(f_ref, bf_ref, o_ref):
    x = f_ref[...] + bf_ref[...]
    c = jnp.minimum(x, 0.0) - jnp.log(1.0 + jnp.exp(-jnp.abs(x)))
    seq = c.shape[0]
    row = lax.broadcasted_iota(I32, c.shape, 0)
    step = 1
    while step < seq:
        c = c + jnp.where(row < step, 0.0, pltpu.roll(c, step, 0))
        step *= 2
    o_ref[...] = c.T[:FOX_HEADS, :]


def _forget_cumsum(f, b_f, bsz, seq):
    return pl.pallas_call(
        _forget_cumsum_kernel,
        out_shape=jax.ShapeDtypeStruct((bsz, FOX_HEADS, seq), F32),
        grid=(bsz,),
        in_specs=[
            pl.BlockSpec((seq, LANES), lambda b: (b, 0)),
            pl.BlockSpec((1, LANES), lambda b: (0, 0)),
        ],
        out_specs=pl.BlockSpec((None, FOX_HEADS, seq), lambda b: (b, 0, 0)),
        compiler_params=_params("parallel"),
        name="forget_gate_cumsum",
    )(f, b_f)


def _fox_attn_kernel(qi_ref, kj_ref, q_ref, k_ref, v_ref, ck_ref, o_ref, m_ref, l_ref, acc_ref, *, scale):
    p = pl.program_id(2)
    qi = qi_ref[p]
    kj = kj_ref[p]

    @pl.when(kj == 0)
    def _():
        m_ref[...] = jnp.full_like(m_ref, NEG_BIG)
        l_ref[...] = jnp.zeros_like(l_ref)
        acc_ref[...] = jnp.zeros_like(acc_ref)

    q = q_ref[...]
    k = k_ref[...]
    v = v_ref[...]
    tq, tk = q.shape[0], k.shape[0]
    half = q.shape[1] // 2
    lane = lax.broadcasted_iota(I32, (1, q.shape[1]), 1)
    row = qi * tq + lax.broadcasted_iota(I32, (tq, tk), 0)
    col = kj * tk + lax.broadcasted_iota(I32, (tq, tk), 1)
    causal = col <= row
    ck = ck_ref[...]
    for hh in range(2):
        sel = (lane < half) if hh == 0 else (lane >= half)
        qh = (jnp.where(sel, q, jnp.zeros_like(q)).astype(F32) * scale).astype(BF16)
        s = lax.dot_general(qh, k, (((1,), (1,)), ((), ())), preferred_element_type=F32)
        s = jnp.where(causal, s - ck[hh:hh + 1, :], NEG_BIG)
        m_prev = m_ref[hh]
        m_new = jnp.maximum(m_prev, jnp.max(s, axis=-1, keepdims=True))
        alpha = jnp.exp(m_prev - m_new)
        pexp = jnp.exp(s - m_new)
        l_ref[hh] = alpha * l_ref[hh] + jnp.sum(pexp, axis=-1, keepdims=True)
        acc_ref[hh] = alpha * acc_ref[hh] + jnp.dot(pexp.astype(BF16), v, preferred_element_type=F32)
        m_ref[hh] = m_new

    @pl.when(kj == qi)
    def _():
        o0 = acc_ref[0] / l_ref[0]
        o1 = acc_ref[1] / l_ref[1]
        o_ref[...] = jnp.where(lane < half, o0, o1).astype(o_ref.dtype)


def _fox_attention(qkv, cum, bsz, seq):
    t = qkv.shape[0]
    d = qkv.shape[1] // 3
    pairs = d // LANES
    tq = ATTN_TILE
    per_seq = seq // tq
    qi_tab = jnp.asarray([i for i in range(per_seq) for _ in range(i + 1)], I32)
    kj_tab = jnp.asarray([j for i in range(per_seq) for j in range(i + 1)], I32)
    scale = (d // FOX_HEADS) ** -0.5
    ck = cum.reshape(bsz, pairs, 2, seq)
    grid_spec = pltpu.PrefetchScalarGridSpec(
        num_scalar_prefetch=2,
        grid=(bsz, pairs, qi_tab.shape[0]),
        in_specs=[
            pl.BlockSpec((tq, LANES), lambda b, hp, p, qi, kj: (b * per_seq + qi[p], hp)),
            pl.BlockSpec((tq, LANES), lambda b, hp, p, qi, kj: (b * per_seq + kj[p], pairs + hp)),
            pl.BlockSpec((tq, LANES), lambda b, hp, p, qi, kj: (b * per_seq + kj[p], 2 * pairs + hp)),
            pl.BlockSpec((None, None, 2, tq), lambda b, hp, p, qi, kj: (b, hp, 0, kj[p])),
        ],
        out_specs=pl.BlockSpec((tq, LANES), lambda b, hp, p, qi, kj: (b * per_seq + qi[p], hp)),
        scratch_shapes=[
            pltpu.VMEM((2, tq, 1), F32),
            pltpu.VMEM((2, tq, 1), F32),
            pltpu.VMEM((2, tq, LANES), F32),
        ],
    )
    return pl.pallas_call(
        functools.partial(_fox_attn_kernel, scale=scale),
        out_shape=jax.ShapeDtypeStruct((t, d), BF16),
        grid_spec=grid_spec,
        compiler_params=_params("parallel", "parallel", "arbitrary"),
        name="forgetting_attention",
    )(qi_tab, kj_tab, qkv, qkv, qkv, ck)


def _router_kernel(x_ref, g_ref, sh_ref, sc_ref, rw_ref, rb_ref,
                   h_ref, e_ref, gt_ref, rk_ref, cnt_ref, carry_ref):
    @pl.when(pl.program_id(0) == 0)
    def _():
        carry_ref[...] = jnp.zeros_like(carry_ref)

    h = _norm_mod(x_ref[...], g_ref[...], sh_ref[...], sc_ref[...])
    h_ref[...] = h
    logits = jnp.dot(h, rw_ref[...], precision=HIGHEST, preferred_element_type=F32) + rb_ref[...]
    tm = logits.shape[0]
    lane = lax.broadcasted_iota(I32, logits.shape, 1).astype(F32)

    work = logits
    picked = jnp.zeros(logits.shape, F32)
    vals, idxs = [], []
    for _ in range(TOP_K):
        mx = jnp.max(work, axis=-1, keepdims=True)
        idx = jnp.min(jnp.where(work == mx, lane, float(LANES)), axis=-1, keepdims=True)
        sel = lane == idx
        vals.append(mx)
        idxs.append(idx)
        picked = jnp.where(sel, 1.0, picked)
        work = jnp.where(sel, -3.0e38, work)

    ex = [jnp.exp(v - vals[0]) for v in vals]
    den = ex[0] + ex[1] + ex[2] + ex[3]

    r_i = lax.broadcasted_iota(I32, (tm, tm), 0)
    c_i = lax.broadcasted_iota(I32, (tm, tm), 1)
    earlier = jnp.where(c_i < r_i, 1.0, 0.0).astype(BF16)
    before = jnp.dot(earlier, picked.astype(BF16), preferred_element_type=F32) + carry_ref[...]
    carry_ref[...] = carry_ref[...] + jnp.sum(picked, axis=0, keepdims=True)
    cnt_ref[...] = carry_ref[...]

    e_out = jnp.zeros(logits.shape, F32)
    g_out = jnp.zeros(logits.shape, F32)
    r_out = jnp.zeros(logits.shape, F32)
    for k in range(TOP_K):
        rank = jnp.sum(jnp.where(lane == idxs[k], before, 0.0), axis=-1, keepdims=True)
        at_k = lane == float(k)
        e_out = jnp.where(at_k, idxs[k], e_out)
        g_out = jnp.where(at_k, ex[k] / den, g_out)
        r_out = jnp.where(at_k, rank, r_out)
    e_ref[...] = e_out.astype(I32)
    gt_ref[...] = g_out
    rk_ref[...] = r_out.astype(I32)


def _router(x, g, shift, scale, rw, rb, seq):
    t, d = x.shape
    tm = ROW_TILE
    per_seq = seq // tm
    slab = lambda dt: jax.ShapeDtypeStruct((t, LANES), dt)
    slab_spec = pl.BlockSpec((tm, LANES), lambda i: (i, 0))
    return pl.pallas_call(
        _router_kernel,
        out_shape=[jax.ShapeDtypeStruct((t, d), F32), slab(I32), slab(F32), slab(I32),
                   jax.ShapeDtypeStruct((1, LANES), F32)],
        grid=(t // tm,),
        in_specs=[
            pl.BlockSpec((tm, d), lambda i: (i, 0)),
            pl.BlockSpec((1, d), lambda i: (0, 0)),
            pl.BlockSpec((None, 1, d), lambda i: (i // per_seq, 0, 0)),
            pl.BlockSpec((None, 1, d), lambda i: (i // per_seq, 0, 0)),
            pl.BlockSpec((d, LANES), lambda i: (0, 0)),
            pl.BlockSpec((1, LANES), lambda i: (0, 0)),
        ],
        out_specs=[pl.BlockSpec((tm, d), lambda i: (i, 0)), slab_spec, slab_spec, slab_spec,
                   pl.BlockSpec((1, LANES), lambda i: (0, 0))],
        scratch_shapes=[pltpu.VMEM((1, LANES), F32)],
        compiler_params=_params("arbitrary"),
        name="moe_router",
    )(x, g, shift, scale, rw, rb)


def _dispatch_kernel(dest_ref, pad_end_ref, padded_ref, h_hbm, xs_hbm, zero_ref, zsem, sem, *, td):
    i = pl.program_id(0)
    tile = zero_ref.shape[0]
    n_tiles = xs_hbm.shape[0] // tile
    n_exp = pad_end_ref.shape[0]
    n_used = pad_end_ref[n_exp - 1] // tile

    def zero_tile(tile_idx):
        start = pl.multiple_of(tile_idx * tile, tile)
        return pltpu.make_async_copy(zero_ref, xs_hbm.at[pl.ds(start, tile)], zsem)

    def for_zeroed_tiles(fn):
        def per_expert(e, carry):
            @pl.when(padded_ref[e] > 0)
            def _():
                fn(zero_tile(pad_end_ref[e] // tile - 1))
            return carry
        lax.fori_loop(0, n_exp, per_expert, 0)

        def per_tail(tile_idx, carry):
            fn(zero_tile(tile_idx))
            return carry
        lax.fori_loop(n_used, n_tiles, per_tail, 0)

    @pl.when(i == 0)
    def _():
        zero_ref[...] = jnp.zeros_like(zero_ref)
        for_zeroed_tiles(lambda cp: cp.start())
        for_zeroed_tiles(lambda cp: cp.wait())

    def row_copy(tok, slot):
        return pltpu.make_async_copy(h_hbm.at[pl.ds(tok, 1)], xs_hbm.at[pl.ds(dest_ref[slot], 1)], sem)

    def issue(j, carry):
        tok = i * td + j
        for k in range(TOP_K):
            row_copy(tok, tok * TOP_K + k).start()
        return carry

    def drain(j, carry):
        tok = i * td + j
        for k in range(TOP_K):
            row_copy(tok, tok * TOP_K + k).wait()
        return carry

    lax.fori_loop(0, td, issue, 0)
    lax.fori_loop(0, td, drain, 0)


def _dispatch(dest, pad_end, padded, h, n_rows):
    t, d = h.shape
    td = DISPATCH_TILE
    grid_spec = pltpu.PrefetchScalarGridSpec(
        num_scalar_prefetch=3,
        grid=(t // td,),
        in_specs=[pl.BlockSpec(memory_space=pl.ANY)],
        out_specs=pl.BlockSpec(memory_space=pl.ANY),
        scratch_shapes=[pltpu.VMEM((EXPERT_TILE, d), F32), pltpu.SemaphoreType.DMA, pltpu.SemaphoreType.DMA],
    )
    return pl.pallas_call(
        functools.partial(_dispatch_kernel, td=td),
        out_shape=jax.ShapeDtypeStruct((n_rows, d), F32),
        grid_spec=grid_spec,
        compiler_params=_params("arbitrary"),
        name="moe_dispatch",
    )(dest, pad_end, padded, h)


def _expert_kernel(te_ref, nu_ref, xs_ref, wgu_ref, bgu_ref, wd_ref, bd_ref, y_ref):
    @pl.when(pl.program_id(0) >= nu_ref[0])
    def _():
        y_ref[...] = jnp.zeros_like(y_ref)

    @pl.when(pl.program_id(0) < nu_ref[0])
    def _():
        f = wd_ref.shape[0]
        gu = jnp.dot(xs_ref[...].astype(BF16), wgu_ref[...], preferred_element_type=F32) + bgu_ref[...]
        gate = jnp.minimum(gu[:, :f], SWIGLU_LIMIT)
        up = jnp.clip(gu[:, f:], -SWIGLU_LIMIT, SWIGLU_LIMIT)
        act = gate * _sigmoid(SWIGLU_ALPHA * gate) * (up + 1.0)
        y_ref[...] = jnp.dot(act.astype(BF16), wd_ref[...], preferred_element_type=F32) + bd_ref[...]


def _expert_mlp(tile_e, n_used, xs, wgu, bgu, wd, bd):
    n_rows, d = xs.shape
    tile = EXPERT_TILE
    f = wd.shape[1]
    row_map = lambda i, te, nu: (jnp.minimum(i, nu[0] - 1), 0)
    exp_map = lambda i, te, nu: (te[i], 0, 0)
    grid_spec = pltpu.PrefetchScalarGridSpec(
        num_scalar_prefetch=2,
        grid=(n_rows // tile,),
        in_specs=[
            pl.BlockSpec((tile, d), row_map),
            pl.BlockSpec((None, d, 2 * f), exp_map),
            pl.BlockSpec((None, 1, 2 * f), exp_map),
            pl.BlockSpec((None, f, d), exp_map),
            pl.BlockSpec((None, 1, d), exp_map),
        ],
        out_specs=pl.BlockSpec((tile, d), lambda i, te, nu: (i, 0)),
    )
    return pl.pallas_call(
        _expert_kernel,
        out_shape=jax.ShapeDtypeStruct((n_rows, d), F32),
        grid_spec=grid_spec,
        compiler_params=_params("arbitrary"),
        name="moe_expert_mlp",
    )(tile_e, n_used, xs, wgu, bgu, wd, bd)


def _combine_kernel(dest_ref, x_ref, gt_ref, gate_ref, yb_hbm, o_ref, buf_ref, sem, *, tc):
    i = pl.program_id(0)
    n = pl.num_programs(0)

    def row_copy(step, slot, j, k):
        tok = step * tc + j
        return pltpu.make_async_copy(yb_hbm.at[pl.ds(dest_ref[tok * TOP_K + k], 1)],
                                     buf_ref.at[slot, k, pl.ds(j, 1)], sem.at[slot])

    def issue(step, slot):
        def body(j, carry):
            for k in range(TOP_K):
                row_copy(step, slot, j, k).start()
            return carry
        lax.fori_loop(0, tc, body, 0)

    @pl.when(i == 0)
    def _():
        issue(0, 0)

    @pl.when(i + 1 < n)
    def _():
        issue(i + 1, (i + 1) % 2)

    slot = i % 2

    def drain(j, carry):
        for k in range(TOP_K):
            row_copy(i, slot, j, k).wait()
        return carry
    lax.fori_loop(0, tc, drain, 0)

    gates = gt_ref[...]
    y = gates[:, 0:1] * buf_ref[slot, 0]
    for k in range(1, TOP_K):
        y = y + gates[:, k:k + 1] * buf_ref[slot, k]
    o_ref[...] = x_ref[...] + gate_ref[...] * y


def _combine(dest, x, gates, gate2, yb, seq):
    t, d = x.shape
    tc = COMBINE_TILE
    per_seq = seq // tc
    grid_spec = pltpu.PrefetchScalarGridSpec(
        num_scalar_prefetch=1,
        grid=(t // tc,),
        in_specs=[
            pl.BlockSpec((tc, d), lambda i, dr: (i, 0)),
            pl.BlockSpec((tc, LANES), lambda i, dr: (i, 0)),
            pl.BlockSpec((None, 1, d), lambda i, dr: (i // per_seq, 0, 0)),
            pl.BlockSpec(memory_space=pl.ANY),
        ],
        out_specs=pl.BlockSpec((tc, d), lambda i, dr: (i, 0)),
        scratch_shapes=[pltpu.VMEM((2, TOP_K, tc, d), F32), pltpu.SemaphoreType.DMA((2,))],
    )
    return pl.pallas_call(
        functools.partial(_combine_kernel, tc=tc),
        out_shape=jax.ShapeDtypeStruct((t, d), F32),
        grid_spec=grid_spec,
        compiler_params=_params("arbitrary"),
        name="moe_combine_residual",
    )(dest, x, gates, gate2, yb)


def _moe_layer(x, g, shift, scale, gate2, rw, rb, wgu, bgu, wd, bd, seq):
    t, d = x.shape
    n_exp = rw.shape[1]
    rw_pad = jnp.zeros((d, LANES), F32).at[:, :n_exp].set(rw)
    rb_pad = jnp.full((1, LANES), NEG_BIG, F32).at[0, :n_exp].set(rb)
    h, e_slab, gt_slab, rk_slab, cnt = _router(x, g, shift, scale, rw_pad, rb_pad, seq)

    tile = EXPERT_TILE
    n_tiles = t * TOP_K // tile + n_exp
    counts = cnt[0, :n_exp].astype(I32)
    padded = (counts + tile - 1) // tile * tile
    pad_end = jnp.cumsum(padded)
    pad_start = pad_end - padded
    dest = (pad_start[e_slab[:, :TOP_K]] + rk_slab[:, :TOP_K]).reshape(-1)
    n_used = pad_end[-1] // tile
    tile_ids = jnp.arange(n_tiles, dtype=I32)
    tile_e = jnp.minimum(jnp.searchsorted(pad_end, tile_ids * tile, side="right"), n_exp - 1).astype(I32)
    tile_e = jnp.where(tile_ids < n_used, tile_e, tile_e[n_used - 1])

    xs = _dispatch(dest, pad_end, padded, h, n_tiles * tile)
    yb = _expert_mlp(tile_e, n_used.reshape(1), xs, wgu.astype(BF16), bgu[:, None, :], wd.astype(BF16),
                     bd[:, None, :])
    return _combine(dest, x, gt_slab, gate2, yb, seq)


def _final_norm_kernel(x_ref, g_ref, o_ref):
    x = x_ref[...]
    ms = jnp.mean(x * x, axis=-1, keepdims=True)
    o_ref[...] = x * lax.rsqrt(ms + RMS_EPS) * g_ref[...]


def _final_norm(x, g):
    t, d = x.shape
    tm = ROW_TILE
    return pl.pallas_call(
        _final_norm_kernel,
        out_shape=jax.ShapeDtypeStruct((t, d), F32),
        grid=(t // tm,),
        in_specs=[pl.BlockSpec((tm, d), lambda i: (i, 0)), pl.BlockSpec((1, d), lambda i: (0, 0))],
        out_specs=pl.BlockSpec((tm, d), lambda i: (i, 0)),
        compiler_params=_params("parallel"),
        name="final_rmsnorm",
    )(x, g)


def kernel(x, c, norm_mix_g, norm_ffn_g, w_mod, b_mod, conv_w_in, conv_w, conv_w_out, lru_w_in, lru_conv_w, lru_conv_b, lru_gate_w, lru_gate_b, lru_a_param, lru_w_out, fox_w_in, fox_b_f, fox_w_out, router_w, router_b, moe_w_gate_up, moe_b_gate_up, moe_w_down, moe_b_down, final_g):
    bsz, seq, d = x.shape
    depth = w_mod.shape[0]
    t = bsz * seq
    mod = _modulation(c, w_mod, b_mod)
    xt = x.reshape(t, d)
    for i in range(depth):
        shift1, scale1, gate1 = mod[i, :, 0], mod[i, :, 1], mod[i, :, 2]
        shift2, scale2, gate2 = mod[i, :, 3], mod[i, :, 4], mod[i, :, 5]
        g_mix = norm_mix_g[i][None, :]
        kind, j = i % 3, i // 3
        if kind == 0:
            bcv = _norm_matmul(xt, g_mix, shift1, scale1, conv_w_in[j].astype(BF16), 3 * d, F32, seq)[0]
            xt = _conv_mixer(bcv, conv_w[j], conv_w_out[j].astype(BF16), xt, gate1, seq)
        elif kind == 1:
            gx = _norm_matmul(xt, g_mix, shift1, scale1, lru_w_in[j].astype(BF16), lru_w_in.shape[2], F32, seq)[0]
            y = _lru_core(gx, lru_conv_w[j], lru_conv_b[j][None, :], lru_gate_w[j].astype(BF16),
                          lru_gate_b[j][:, None, :], lru_a_param[j][None, :], bsz, seq)
            xt = _proj_residual(y, lru_w_out[j].astype(BF16), xt, gate1, seq)
        else:
            w_in = jnp.zeros((d, 3 * d + LANES), F32).at[:, :3 * d + FOX_HEADS].set(fox_w_in[j]).astype(BF16)
            qkv, f = _norm_matmul(xt, g_mix, shift1, scale1, w_in, 3 * d, BF16, seq)
            b_f = jnp.zeros((1, LANES), F32).at[0, :FOX_HEADS].set(fox_b_f[j])
            cum = _forget_cumsum(f, b_f, bsz, seq)
            o = _fox_attention(qkv, cum, bsz, seq)
            xt = _proj_residual(o, fox_w_out[j].astype(BF16), xt, gate1, seq)
        xt = _moe_layer(xt, norm_ffn_g[i][None, :], shift2, scale2, gate2, router_w[i], router_b[i],
                        moe_w_gate_up[i], moe_b_gate_up[i], moe_w_down[i], moe_b_down[i], seq)
    return _final_norm(xt, final_g[None, :]).reshape(bsz, seq, d)
```

```python
import functools

import jax
import jax.numpy as jnp
from jax import lax
from jax.experimental import pallas as pl
from jax.experimental.pallas import tpu as pltpu

F32 = jnp.float32
BF16 = jnp.bfloat16
I32 = jnp.int32
HIGHEST = lax.Precision.HIGHEST

RMS_EPS = 1e-6
LRU_HEADS = 4
LRU_C = 8.0
FOX_HEADS = 16
N_EXPERTS = 32
TOP_K = 4
SWIGLU_LIMIT = 7.0
SWIGLU_ALPHA = 1.702
NEG_BIG = -1e30
GELU_C = 0.7978845608028654
LOG2E = 1.4426950408889634

LANES = 128
SUBLANES = 8
VMEM_LIMIT = 56 * 1024 * 1024

ROW_TILE = 512
EXPERT_TILE = 256
DISPATCH_TILE = 256
COMBINE_TILE = 128
ATTN_TILE = 512


def _params(*sem):
    return pltpu.CompilerParams(dimension_semantics=sem, vmem_limit_bytes=VMEM_LIMIT)


def _sigmoid(z):
    return 1.0 / (1.0 + jnp.exp(-z))


def _norm_mod(x, g, shift, scale):
    ms = jnp.mean(x * x, axis=-1, keepdims=True)
    y = x * lax.rsqrt(ms + RMS_EPS) * g
    return y * (1.0 + scale) + shift


def _shifted_rows(u, carry_ref, j, row):
    sh = pltpu.roll(u, j, 0)
    for r in range(j):
        src = SUBLANES - j + r
        sh = jnp.where(row == r, carry_ref[src:src + 1, :], sh)
    return sh


def _mod_kernel(c_ref, w_ref, b_ref, o_ref):
    c = c_ref[...]
    ca = c * _sigmoid(c)
    o_ref[0] = jnp.dot(ca, w_ref[0], precision=HIGHEST, preferred_element_type=F32) + b_ref[0]


def _modulation(c, w_mod, b_mod):
    depth, d, n = w_mod.shape
    bsz = c.shape[0]
    rows = -(-bsz // SUBLANES) * SUBLANES
    c_pad = jnp.zeros((rows, d), F32).at[:bsz].set(c)
    tn = n // 4
    out = pl.pallas_call(
        _mod_kernel,
        out_shape=jax.ShapeDtypeStruct((depth, rows, n), F32),
        grid=(depth, n // tn),
        in_specs=[
            pl.BlockSpec((rows, d), lambda l, j: (0, 0)),
            pl.BlockSpec((1, d, tn), lambda l, j: (l, 0, j)),
            pl.BlockSpec((1, 1, tn), lambda l, j: (l, 0, j)),
        ],
        out_specs=pl.BlockSpec((1, rows, tn), lambda l, j: (l, 0, j)),
        compiler_params=_params("parallel", "parallel"),
        name="adaln_modulation",
    )(c_pad, w_mod, b_mod.reshape(depth, 1, n))
    return out[:, :bsz].reshape(depth, bsz, 6, 1, d)


def _norm_matmul_kernel(x_ref, g_ref, sh_ref, sc_ref, w_ref, *out_refs, n_main):
    h = _norm_mod(x_ref[...], g_ref[...], sh_ref[...], sc_ref[...]).astype(BF16)
    y = jnp.dot(h, w_ref[...], preferred_element_type=F32)
    out_refs[0][...] = y[:, :n_main].astype(out_refs[0].dtype)
    if len(out_refs) > 1:
        out_refs[1][...] = y[:, n_main:]


def _norm_matmul(x, g, shift, scale, w, n_main, out_dtype, seq):
    t, d = x.shape
    n = w.shape[1]
    tm = ROW_TILE
    per_seq = seq // tm
    out_shape = [jax.ShapeDtypeStruct((t, n_main), out_dtype)]
    out_specs = [pl.BlockSpec((tm, n_main), lambda i: (i, 0))]
    if n > n_main:
        out_shape.append(jax.ShapeDtypeStruct((t, n - n_main), F32))
        out_specs.append(pl.BlockSpec((tm, n - n_main), lambda i: (i, 0)))
    return pl.pallas_call(
        functools.partial(_norm_matmul_kernel, n_main=n_main),
        out_shape=out_shape,
        grid=(t // tm,),
        in_specs=[
            pl.BlockSpec((tm, d), lambda i: (i, 0)),
            pl.BlockSpec((1, d), lambda i: (0, 0)),
            pl.BlockSpec((None, 1, d), lambda i: (i // per_seq, 0, 0)),
            pl.BlockSpec((None, 1, d), lambda i: (i // per_seq, 0, 0)),
            pl.BlockSpec((d, n), lambda i: (0, 0)),
        ],
        out_specs=out_specs,
        compiler_params=_params("parallel"),
        name="norm_mod_in_proj",
    )(x, g, shift, scale, w)


def _proj_residual_kernel(z_ref, w_ref, x_ref, gate_ref, o_ref):
    y = jnp.dot(z_ref[...].astype(BF16), w_ref[...], preferred_element_type=F32)
    o_ref[...] = x_ref[...] + gate_ref[...] * y


def _proj_residual(z, w, x, gate, seq):
    t, d = x.shape
    tm = ROW_TILE
    per_seq = seq // tm
    return pl.pallas_call(
        _proj_residual_kernel,
        out_shape=jax.ShapeDtypeStruct((t, d), F32),
        grid=(t // tm,),
        in_specs=[
            pl.BlockSpec((tm, z.shape[1]), lambda i: (i, 0)),
            pl.BlockSpec(w.shape, lambda i: (0, 0)),
            pl.BlockSpec((tm, d), lambda i: (i, 0)),
            pl.BlockSpec((None, 1, d), lambda i: (i // per_seq, 0, 0)),
        ],
        out_specs=pl.BlockSpec((tm, d), lambda i: (i, 0)),
        compiler_params=_params("parallel"),
        name="out_proj_residual",
    )(z, w, x, gate)


def _conv_mixer_kernel(b_ref, c_ref, v_ref, cw_ref, w_ref, x_ref, gate_ref, o_ref, carry_ref, *, per_seq):
    @pl.when(pl.program_id(0) % per_seq == 0)
    def _():
        carry_ref[...] = jnp.zeros_like(carry_ref)

    cv = c_ref[...] * v_ref[...]
    tm = cv.shape[0]
    k_w = cw_ref.shape[0]
    row = lax.broadcasted_iota(I32, cv.shape, 0)
    conv = cw_ref[k_w - 1:k_w, :] * cv
    for j in range(1, k_w):
        conv = conv + cw_ref[k_w - 1 - j:k_w - j, :] * _shifted_rows(cv, carry_ref, j, row)
    carry_ref[...] = cv[tm - SUBLANES:, :]
    z = (b_ref[...] * conv).astype(BF16)
    y = jnp.dot(z, w_ref[...], preferred_element_type=F32)
    o_ref[...] = x_ref[...] + gate_ref[...] * y


def _conv_mixer(bcv, conv_w, w_out, x, gate, seq):
    t, d = x.shape
    tm = ROW_TILE
    per_seq = seq // tm
    return pl.pallas_call(
        functools.partial(_conv_mixer_kernel, per_seq=per_seq),
        out_shape=jax.ShapeDtypeStruct((t, d), F32),
        grid=(t // tm,),
        in_specs=[
            pl.BlockSpec((tm, d), lambda i: (i, 0)),
            pl.BlockSpec((tm, d), lambda i: (i, 1)),
            pl.BlockSpec((tm, d), lambda i: (i, 2)),
            pl.BlockSpec(conv_w.shape, lambda i: (0, 0)),
            pl.BlockSpec(w_out.shape, lambda i: (0, 0)),
            pl.BlockSpec((tm, d), lambda i: (i, 0)),
            pl.BlockSpec((None, 1, d), lambda i: (i // per_seq, 0, 0)),
        ],
        out_specs=pl.BlockSpec((tm, d), lambda i: (i, 0)),
        scratch_shapes=[pltpu.VMEM((SUBLANES, d), F32)],
        compiler_params=_params("arbitrary"),
        name="short_conv_mixer",
    )(bcv, bcv, bcv, conv_w, w_out, x, gate)


def _lru_kernel(xb_ref, gb_ref, cw_ref, cb_ref, gw_ref, gbias_ref, ap_ref, y_ref, xcarry_ref, hcarry_ref):
    @pl.when(pl.program_id(2) == 0)
    def _():
        xcarry_ref[...] = jnp.zeros_like(xcarry_ref)
        hcarry_ref[...] = jnp.zeros_like(hcarry_ref)

    x = xb_ref[...]
    ts, blk = x.shape
    k_w = cw_ref.shape[0]
    row = lax.broadcasted_iota(I32, x.shape, 0)
    conv = cw_ref[k_w - 1:k_w, :] * x
    for j in range(1, k_w):
        conv = conv + cw_ref[k_w - 1 - j:k_w - j, :] * _shifted_rows(x, xcarry_ref, j, row)
    xcarry_ref[...] = x[ts - SUBLANES:, :]
    xc = conv + cb_ref[...]

    g = jnp.dot(xc.astype(BF16), gw_ref[...], preferred_element_type=F32) + gbias_ref[...]
    r_gate = _sigmoid(g[:, :blk])
    i_gate = _sigmoid(g[:, blk:])
    ap = ap_ref[...]
    softplus = jnp.maximum(ap, 0.0) + jnp.log(1.0 + jnp.exp(-jnp.abs(ap)))
    a = jnp.exp((-LRU_C) * r_gate * softplus)
    u = jnp.sqrt(1.0 - a * a) * (i_gate * xc)

    a_cum, h_loc = a, u
    step = 1
    while step < ts:
        a_prev = jnp.where(row < step, 1.0, pltpu.roll(a_cum, step, 0))
        h_prev = jnp.where(row < step, 0.0, pltpu.roll(h_loc, step, 0))
        h_loc = h_loc + a_cum * h_prev
        a_cum = a_cum * a_prev
        step *= 2
    hs = h_loc + a_cum * hcarry_ref[...]
    hcarry_ref[...] = hs[ts - 1:ts, :]

    gb = gb_ref[...]
    gelu = 0.5 * gb * (1.0 + jnp.tanh(GELU_C * (gb + 0.044715 * (gb * gb * gb))))
    y_ref[...] = (gelu * hs).astype(y_ref.dtype)


def _lru_core(gx, conv_w, conv_b, gate_w, gate_b, a_param, bsz, seq):
    t = gx.shape[0]
    width = gx.shape[1] // 2
    blk = width // LRU_HEADS
    ts = ROW_TILE
    per_seq = seq // ts
    return pl.pallas_call(
        _lru_kernel,
        out_shape=jax.ShapeDtypeStruct((t, width), BF16),
        grid=(bsz, LRU_HEADS, per_seq),
        in_specs=[
            pl.BlockSpec((ts, blk), lambda b, h, s: (b * per_seq + s, LRU_HEADS + h)),
            pl.BlockSpec((ts, blk), lambda b, h, s: (b * per_seq + s, h)),
            pl.BlockSpec((conv_w.shape[0], blk), lambda b, h, s: (0, h)),
            pl.BlockSpec((1, blk), lambda b, h, s: (0, h)),
            pl.BlockSpec((None, blk, 2 * blk), lambda b, h, s: (h, 0, 0)),
            pl.BlockSpec((None, 1, 2 * blk), lambda b, h, s: (h, 0, 0)),
            pl.BlockSpec((1, blk), lambda b, h, s: (0, h)),
        ],
        out_specs=pl.BlockSpec((ts, blk), lambda b, h, s: (b * per_seq + s, h)),
        scratch_shapes=[pltpu.VMEM((SUBLANES, blk), F32), pltpu.VMEM((1, blk), F32)],
        compiler_params=_params("parallel", "parallel", "arbitrary"),
        name="rglru_scan",
    )(gx, gx, conv_w, conv_b, gate_w, gate_b, a_param)


def _forget_cumsum_kernel(f_ref, bf_ref, o_ref):
    x = f_ref[...] + bf_ref[...]
    c = jnp.minimum(x, 0.0) - jnp.log(1.0 + jnp.exp(-jnp.abs(x)))
    seq = c.shape[0]
    row = lax.broadcasted_iota(I32, c.shape, 0)
    step = 1
    while step < seq:
        c = c + jnp.where(row < step, 0.0, pltpu.roll(c, step, 0))
        step *= 2
    o_ref[...] = c.T[:FOX_HEADS, :]


def _forget_cumsum(f, b_f, bsz, seq):
    return pl.pallas_call(
        _forget_cumsum_kernel,
        out_shape=jax.ShapeDtypeStruct((bsz, FOX_HEADS, seq), F32),
        grid=(bsz,),
        in_specs=[
            pl.BlockSpec((seq, LANES), lambda b: (b, 0)),
            pl.BlockSpec((1, LANES), lambda b: (0, 0)),
        ],
        out_specs=pl.BlockSpec((None, FOX_HEADS, seq), lambda b: (b, 0, 0)),
        compiler_params=_params("parallel"),
        name="forget_gate_cumsum",
    )(f, b_f)


def _fox_attn_kernel(q_ref, k_ref, v_ref, ck_ref, o_ref, m_ref, acc_ref, *, scale):
    qi = pl.program_id(2)
    tq = q_ref.shape[0]
    tk = tq
    half = q_ref.shape[1] // 2
    first_head = lax.broadcasted_iota(I32, (1, q_ref.shape[1]), 1) < half

    qs = (q_ref[...].astype(F32) * (scale * LOG2E)).astype(BF16)
    zero = jnp.zeros_like(qs)
    q_heads = (jnp.where(first_head, qs, zero), jnp.where(first_head, zero, qs))
    m_ref[...] = jnp.full_like(m_ref, NEG_BIG)
    acc_ref[...] = jnp.zeros_like(acc_ref)

    def update(j, diagonal):
        start = pl.multiple_of(j * tk, tk)
        k = k_ref[pl.ds(start, tk), :]
        v = v_ref[pl.ds(start, tk), :]
        ck = ck_ref[:, pl.ds(start, tk)] * LOG2E
        ones = jnp.ones_like(v)
        v_heads = (jnp.where(first_head, v, ones), jnp.where(first_head, ones, v))
        for hh in range(2):
            s = lax.dot_general(q_heads[hh], k, (((1,), (1,)), ((), ())), preferred_element_type=F32)
            s = s - ck[hh:hh + 1, :]
            if diagonal:
                row = lax.broadcasted_iota(I32, s.shape, 0)
                col = lax.broadcasted_iota(I32, s.shape, 1)
                s = jnp.where(col <= row, s, NEG_BIG)
            m_prev = m_ref[hh]
            m_new = jnp.maximum(m_prev, jnp.max(s, axis=-1, keepdims=True))
            alpha = jnp.exp2(m_prev - m_new)
            p = jnp.exp2(s - jnp.concatenate([m_new] * (tk // LANES), axis=1))
            m_ref[hh] = m_new
            acc_ref[hh] = alpha * acc_ref[hh] + jnp.dot(p.astype(BF16), v_heads[hh], preferred_element_type=F32)

    def body(j, carry):
        update(j, False)
        return carry

    lax.fori_loop(0, qi, body, 0)
    update(qi, True)
    a0 = acc_ref[0]
    a1 = acc_ref[1]
    o0 = a0 / pltpu.roll(a0, half, 1)
    o1 = a1 / pltpu.roll(a1, half, 1)
    o_ref[...] = jnp.where(first_head, o0, o1).astype(o_ref.dtype)


def _fox_attention(qkv, cum, bsz, seq):
    t = qkv.shape[0]
    d = qkv.shape[1] // 3
    pairs = d // LANES
    tq = ATTN_TILE
    per_seq = seq // tq
    scale = (d // FOX_HEADS) ** -0.5
    ck = cum.reshape(bsz, pairs, 2, seq)
    return pl.pallas_call(
        functools.partial(_fox_attn_kernel, scale=scale),
        out_shape=jax.ShapeDtypeStruct((t, d), BF16),
        grid=(bsz, pairs, per_seq),
        in_specs=[
            pl.BlockSpec((tq, LANES), lambda b, hp, qi: (b * per_seq + qi, hp)),
            pl.BlockSpec((seq, LANES), lambda b, hp, qi: (b, pairs + hp)),
            pl.BlockSpec((seq, LANES), lambda b, hp, qi: (b, 2 * pairs + hp)),
            pl.BlockSpec((None, None, 2, seq), lambda b, hp, qi: (b, hp, 0, 0)),
        ],
        out_specs=pl.BlockSpec((tq, LANES), lambda b, hp, qi: (b * per_seq + qi, hp)),
        scratch_shapes=[
            pltpu.VMEM((2, tq, LANES), F32),
            pltpu.VMEM((2, tq, LANES), F32),
        ],
        compiler_params=_params("parallel", "parallel", "arbitrary"),
        name="forgetting_attention",
    )(qkv, qkv, qkv, ck)


def _router_kernel(x_ref, g_ref, sh_ref, sc_ref, rw_ref, rb_ref,
                   h_ref, e_ref, gt_ref, rk_ref, cnt_ref, carry_ref):
    @pl.when(pl.program_id(0) == 0)
    def _():
        carry_ref[...] = jnp.zeros_like(carry_ref)

    h = _norm_mod(x_ref[...], g_ref[...], sh_ref[...], sc_ref[...])
    h_ref[...] = h
    logits = jnp.dot(h, rw_ref[...], precision=HIGHEST, preferred_element_type=F32) + rb_ref[...]
    tm = logits.shape[0]
    lane = lax.broadcasted_iota(I32, logits.shape, 1).astype(F32)

    work = logits
    picked = jnp.zeros(logits.shape, F32)
    vals, idxs = [], []
    for _ in range(TOP_K):
        mx = jnp.max(work, axis=-1, keepdims=True)
        idx = jnp.min(jnp.where(work == mx, lane, float(LANES)), axis=-1, keepdims=True)
        sel = lane == idx
        vals.append(mx)
        idxs.append(idx)
        picked = jnp.where(sel, 1.0, picked)
        work = jnp.where(sel, -3.0e38, work)

    ex = [jnp.exp(v - vals[0]) for v in vals]
    den = ex[0] + ex[1] + ex[2] + ex[3]

    r_i = lax.broadcasted_iota(I32, (tm, tm), 0)
    c_i = lax.broadcasted_iota(I32, (tm, tm), 1)
    earlier = jnp.where(c_i < r_i, 1.0, 0.0).astype(BF16)
    before = jnp.dot(earlier, picked.astype(BF16), preferred_element_type=F32) + carry_ref[...]
    carry_ref[...] = carry_ref[...] + jnp.sum(picked, axis=0, keepdims=True)
    cnt_ref[...] = carry_ref[...]

    e_out = jnp.zeros(logits.shape, F32)
    g_out = jnp.zeros(logits.shape, F32)
    r_out = jnp.zeros(logits.shape, F32)
    for k in range(TOP_K):
        rank = jnp.sum(jnp.where(lane == idxs[k], before, 0.0), axis=-1, keepdims=True)
        at_k = lane == float(k)
        e_out = jnp.where(at_k, idxs[k], e_out)
        g_out = jnp.where(at_k, ex[k] / den, g_out)
        r_out = jnp.where(at_k, rank, r_out)
    e_ref[...] = e_out.astype(I32)
    gt_ref[...] = g_out
    rk_ref[...] = r_out.astype(I32)


def _router(x, g, shift, scale, rw, rb, seq):
    t, d = x.shape
    tm = ROW_TILE
    per_seq = seq // tm
    slab = lambda dt: jax.ShapeDtypeStruct((t, LANES), dt)
    slab_spec = pl.BlockSpec((tm, LANES), lambda i: (i, 0))
    return pl.pallas_call(
        _router_kernel,
        out_shape=[jax.ShapeDtypeStruct((t, d), F32), slab(I32), slab(F32), slab(I32),
                   jax.ShapeDtypeStruct((1, LANES), F32)],
        grid=(t // tm,),
        in_specs=[
            pl.BlockSpec((tm, d), lambda i: (i, 0)),
            pl.BlockSpec((1, d), lambda i: (0, 0)),
            pl.BlockSpec((None, 1, d), lambda i: (i // per_seq, 0, 0)),
            pl.BlockSpec((None, 1, d), lambda i: (i // per_seq, 0, 0)),
            pl.BlockSpec((d, LANES), lambda i: (0, 0)),
            pl.BlockSpec((1, LANES), lambda i: (0, 0)),
        ],
        out_specs=[pl.BlockSpec((tm, d), lambda i: (i, 0)), slab_spec, slab_spec, slab_spec,
                   pl.BlockSpec((1, LANES), lambda i: (0, 0))],
        scratch_shapes=[pltpu.VMEM((1, LANES), F32)],
        compiler_params=_params("arbitrary"),
        name="moe_router",
    )(x, g, shift, scale, rw, rb)


def _dispatch_kernel(dest_ref, pad_end_ref, padded_ref, h_ref, xs_hbm, zero_ref, zsem, sem, *, td):
    i = pl.program_id(0)
    tile = zero_ref.shape[0]
    n_tiles = xs_hbm.shape[0] // tile
    n_exp = pad_end_ref.shape[0]
    n_used = pad_end_ref[n_exp - 1] // tile

    def zero_tile(tile_idx):
        start = pl.multiple_of(tile_idx * tile, tile)
        return pltpu.make_async_copy(zero_ref, xs_hbm.at[pl.ds(start, tile)], zsem)

    def for_zeroed_tiles(fn):
        def per_expert(e, carry):
            @pl.when(padded_ref[e] > 0)
            def _():
                fn(zero_tile(pad_end_ref[e] // tile - 1))
            return carry
        lax.fori_loop(0, n_exp, per_expert, 0)

        def per_tail(tile_idx, carry):
            fn(zero_tile(tile_idx))
            return carry
        lax.fori_loop(n_used, n_tiles, per_tail, 0)

    @pl.when(i == 0)
    def _():
        zero_ref[...] = jnp.zeros_like(zero_ref)
        for_zeroed_tiles(lambda cp: cp.start())
        for_zeroed_tiles(lambda cp: cp.wait())

    def issue(j, carry):
        for k in range(TOP_K):
            dst = dest_ref[(i * td + j) * TOP_K + k]
            pltpu.make_async_copy(h_ref.at[pl.ds(j, 1)], xs_hbm.at[pl.ds(dst, 1)], sem).start()
        return carry

    lax.fori_loop(0, td, issue, 0)
    n_copied = td * TOP_K
    pltpu.make_async_copy(xs_hbm.at[pl.ds(0, n_copied)], xs_hbm.at[pl.ds(0, n_copied)], sem).wait()


def _dispatch(dest, pad_end, padded, h, n_rows):
    t, d = h.shape
    td = DISPATCH_TILE
    grid_spec = pltpu.PrefetchScalarGridSpec(
        num_scalar_prefetch=3,
        grid=(t // td,),
        in_specs=[pl.BlockSpec((td, d), lambda i, de, pe, pa: (i, 0))],
        out_specs=pl.BlockSpec(memory_space=pl.ANY),
        scratch_shapes=[pltpu.VMEM((EXPERT_TILE, d), F32), pltpu.SemaphoreType.DMA, pltpu.SemaphoreType.DMA],
    )
    return pl.pallas_call(
        functools.partial(_dispatch_kernel, td=td),
        out_shape=jax.ShapeDtypeStruct((n_rows, d), F32),
        grid_spec=grid_spec,
        compiler_params=_params("arbitrary"),
        name="moe_dispatch",
    )(dest, pad_end, padded, h)


def _expert_kernel(te_ref, nu_ref, xs_ref, wgu_ref, bgu_ref, wd_ref, bd_ref, y_ref, wgu_bf, wd_bf):
    i = pl.program_id(0)

    @pl.when(i >= nu_ref[0])
    def _():
        y_ref[...] = jnp.zeros_like(y_ref)

    @pl.when(jnp.logical_or(i == 0, te_ref[i] != te_ref[jnp.maximum(i - 1, 0)]))
    def _():
        wgu_bf[...] = wgu_ref[...].astype(BF16)
        wd_bf[...] = wd_ref[...].astype(BF16)

    @pl.when(i < nu_ref[0])
    def _():
        f = wd_ref.shape[0]
        gu = jnp.dot(xs_ref[...].astype(BF16), wgu_bf[...], preferred_element_type=F32) + bgu_ref[...]
        gate = jnp.minimum(gu[:, :f], SWIGLU_LIMIT)
        up = jnp.clip(gu[:, f:], -SWIGLU_LIMIT, SWIGLU_LIMIT)
        act = gate * _sigmoid(SWIGLU_ALPHA * gate) * (up + 1.0)
        y_ref[...] = jnp.dot(act.astype(BF16), wd_bf[...], preferred_element_type=F32) + bd_ref[...]


def _expert_mlp(tile_e, n_used, xs, layer, wgu, bgu, wd, bd):
    n_rows, d = xs.shape
    tile = EXPERT_TILE
    f = wd.shape[2]
    row_map = lambda i, te, nu: (jnp.minimum(i, nu[0] - 1), 0)
    exp_map = lambda i, te, nu: (layer, te[i], 0, 0)
    grid_spec = pltpu.PrefetchScalarGridSpec(
        num_scalar_prefetch=2,
        grid=(n_rows // tile,),
        in_specs=[
            pl.BlockSpec((tile, d), row_map),
            pl.BlockSpec((None, None, d, 2 * f), exp_map),
            pl.BlockSpec((None, None, 1, 2 * f), exp_map),
            pl.BlockSpec((None, None, f, d), exp_map),
            pl.BlockSpec((None, None, 1, d), exp_map),
        ],
        out_specs=pl.BlockSpec((tile, d), lambda i, te, nu: (i, 0)),
        scratch_shapes=[pltpu.VMEM((d, 2 * f), BF16), pltpu.VMEM((f, d), BF16)],
    )
    return pl.pallas_call(
        _expert_kernel,
        out_shape=jax.ShapeDtypeStruct((n_rows, d), F32),
        grid_spec=grid_spec,
        compiler_params=_params("arbitrary"),
        name="moe_expert_mlp",
    )(tile_e, n_used, xs, wgu, bgu[:, :, None, :], wd, bd[:, :, None, :])


def _combine_kernel(dest_ref, x_ref, gt_ref, gate_ref, yb_hbm, o_ref, buf_ref, sem, *, tc):
    i = pl.program_id(0)
    n = pl.num_programs(0)

    def row_copy(step, slot, j, k):
        tok = step * tc + j
        return pltpu.make_async_copy(yb_hbm.at[pl.ds(dest_ref[tok * TOP_K + k], 1)],
                                     buf_ref.at[slot, k, pl.ds(j, 1)], sem.at[slot])

    def issue(step, slot):
        def body(j, carry):
            for k in range(TOP_K):
                row_copy(step, slot, j, k).start()
            return carry
        lax.fori_loop(0, tc, body, 0)

    @pl.when(i == 0)
    def _():
        issue(0, 0)

    @pl.when(i + 1 < n)
    def _():
        issue(i + 1, (i + 1) % 2)

    slot = i % 2

    for k in range(TOP_K):
        pltpu.make_async_copy(yb_hbm.at[pl.ds(0, tc)], buf_ref.at[slot, k], sem.at[slot]).wait()

    gates = gt_ref[...]
    y = gates[:, 0:1] * buf_ref[slot, 0]
    for k in range(1, TOP_K):
        y = y + gates[:, k:k + 1] * buf_ref[slot, k]
    o_ref[...] = x_ref[...] + gate_ref[...] * y


def _combine(dest, x, gates, gate2, yb, seq):
    t, d = x.shape
    tc = COMBINE_TILE
    per_seq = seq // tc
    grid_spec = pltpu.PrefetchScalarGridSpec(
        num_scalar_prefetch=1,
        grid=(t // tc,),
        in_specs=[
            pl.BlockSpec((tc, d), lambda i, dr: (i, 0)),
            pl.BlockSpec((tc, LANES), lambda i, dr: (i, 0)),
            pl.BlockSpec((None, 1, d), lambda i, dr: (i // per_seq, 0, 0)),
            pl.BlockSpec(memory_space=pl.ANY),
        ],
        out_specs=pl.BlockSpec((tc, d), lambda i, dr: (i, 0)),
        scratch_shapes=[pltpu.VMEM((2, TOP_K, tc, d), F32), pltpu.SemaphoreType.DMA((2,))],
    )
    return pl.pallas_call(
        functools.partial(_combine_kernel, tc=tc),
        out_shape=jax.ShapeDtypeStruct((t, d), F32),
        grid_spec=grid_spec,
        compiler_params=_params("arbitrary"),
        name="moe_combine_residual",
    )(dest, x, gates, gate2, yb)


def _moe_layer(x, g, shift, scale, gate2, rw, rb, layer, wgu, bgu, wd, bd, seq):
    t, d = x.shape
    n_exp = rw.shape[1]
    rw_pad = jnp.zeros((d, LANES), F32).at[:, :n_exp].set(rw)
    rb_pad = jnp.full((1, LANES), NEG_BIG, F32).at[0, :n_exp].set(rb)
    h, e_slab, gt_slab, rk_slab, cnt = _router(x, g, shift, scale, rw_pad, rb_pad, seq)

    tile = EXPERT_TILE
    n_tiles = t * TOP_K // tile + n_exp
    counts = cnt[0, :n_exp].astype(I32)
    padded = (counts + tile - 1) // tile * tile
    pad_end = jnp.cumsum(padded)
    pad_start = pad_end - padded
    dest = (pad_start[e_slab[:, :TOP_K]] + rk_slab[:, :TOP_K]).reshape(-1)
    n_used = pad_end[-1] // tile
    tile_ids = jnp.arange(n_tiles, dtype=I32)
    first_row = jnp.minimum(tile_ids, n_used - 1) * tile
    tile_e = jnp.sum((pad_end[None, :] <= first_row[:, None]).astype(I32), axis=1)

    xs = _dispatch(dest, pad_end, padded, h, n_tiles * tile)
    yb = _expert_mlp(tile_e, n_used.reshape(1), xs, layer, wgu, bgu, wd, bd)
    return _combine(dest, x, gt_slab, gate2, yb, seq)


def _final_norm_kernel(x_ref, g_ref, o_ref):
    x = x_ref[...]
    ms = jnp.mean(x * x, axis=-1, keepdims=True)
    o_ref[...] = x * lax.rsqrt(ms + RMS_EPS) * g_ref[...]


def _final_norm(x, g):
    t, d = x.shape
    tm = ROW_TILE
    return pl.pallas_call(
        _final_norm_kernel,
        out_shape=jax.ShapeDtypeStruct((t, d), F32),
        grid=(t // tm,),
        in_specs=[pl.BlockSpec((tm, d), lambda i: (i, 0)), pl.BlockSpec((1, d), lambda i: (0, 0))],
        out_specs=pl.BlockSpec((tm, d), lambda i: (i, 0)),
        compiler_params=_params("parallel"),
        name="final_rmsnorm",
    )(x, g)


def kernel(x, c, norm_mix_g, norm_ffn_g, w_mod, b_mod, conv_w_in, conv_w, conv_w_out, lru_w_in, lru_conv_w, lru_conv_b, lru_gate_w, lru_gate_b, lru_a_param, lru_w_out, fox_w_in, fox_b_f, fox_w_out, router_w, router_b, moe_w_gate_up, moe_b_gate_up, moe_w_down, moe_b_down, final_g):
    bsz, seq, d = x.shape
    depth = w_mod.shape[0]
    t = bsz * seq
    mod = _modulation(c, w_mod, b_mod)
    xt = x.reshape(t, d)
    for i in range(depth):
        shift1, scale1, gate1 = mod[i, :, 0], mod[i, :, 1], mod[i, :, 2]
        shift2, scale2, gate2 = mod[i, :, 3], mod[i, :, 4], mod[i, :, 5]
        g_mix = norm_mix_g[i][None, :]
        kind, j = i % 3, i // 3
        if kind == 0:
            bcv = _norm_matmul(xt, g_mix, shift1, scale1, conv_w_in[j].astype(BF16), 3 * d, F32, seq)[0]
            xt = _conv_mixer(bcv, conv_w[j], conv_w_out[j].astype(BF16), xt, gate1, seq)
        elif kind == 1:
            gx = _norm_matmul(xt, g_mix, shift1, scale1, lru_w_in[j].astype(BF16), lru_w_in.shape[2], F32, seq)[0]
            y = _lru_core(gx, lru_conv_w[j], lru_conv_b[j][None, :], lru_gate_w[j].astype(BF16),
                          lru_gate_b[j][:, None, :], lru_a_param[j][None, :], bsz, seq)
            xt = _proj_residual(y, lru_w_out[j].astype(BF16), xt, gate1, seq)
        else:
            w_in = jnp.zeros((d, 3 * d + LANES), F32).at[:, :3 * d + FOX_HEADS].set(fox_w_in[j]).astype(BF16)
            qkv, f = _norm_matmul(xt, g_mix, shift1, scale1, w_in, 3 * d, BF16, seq)
            b_f = jnp.zeros((1, LANES), F32).at[0, :FOX_HEADS].set(fox_b_f[j])
            cum = _forget_cumsum(f, b_f, bsz, seq)
            o = _fox_attention(qkv, cum, bsz, seq)
            xt = _proj_residual(o, fox_w_out[j].astype(BF16), xt, gate1, seq)
        xt = _moe_layer(xt, norm_ffn_g[i][None, :], shift2, scale2, gate2, router_w[i], router_b[i],
                        i, moe_w_gate_up, moe_b_gate_up, moe_w_down, moe_b_down, seq)
    return _final_norm(xt, final_g[None, :]).reshape(bsz, seq, d)
```

```python
import functools

import jax
import jax.numpy as jnp
from jax import lax
from jax.experimental import pallas as pl
from jax.experimental.pallas import tpu as pltpu

F32 = jnp.float32
BF16 = jnp.bfloat16
I32 = jnp.int32
HIGHEST = lax.Precision.HIGHEST

RMS_EPS = 1e-6
LRU_HEADS = 4
LRU_C = 8.0
FOX_HEADS = 16
N_EXPERTS = 32
TOP_K = 4
SWIGLU_LIMIT = 7.0
SWIGLU_ALPHA = 1.702
NEG_BIG = -1e30
GELU_C = 0.7978845608028654
LOG2E = 1.4426950408889634

LANES = 128
SUBLANES = 8
VMEM_LIMIT = 56 * 1024 * 1024

ROW_TILE = 512
EXPERT_TILE = 256
COMBINE_TILE = 256
ATTN_TILE = 512


def _params(*sem):
    return pltpu.CompilerParams(dimension_semantics=sem, vmem_limit_bytes=VMEM_LIMIT)


def _sigmoid(z):
    return 1.0 / (1.0 + jnp.exp(-z))


def _norm_mod(x, g, shift, scale):
    ms = jnp.mean(x * x, axis=-1, keepdims=True)
    y = x * lax.rsqrt(ms + RMS_EPS) * g
    return y * (1.0 + scale) + shift


def _shifted_rows(u, carry_ref, j, row):
    sh = pltpu.roll(u, j, 0)
    for r in range(j):
        src = SUBLANES - j + r
        sh = jnp.where(row == r, carry_ref[src:src + 1, :], sh)
    return sh


def _mod_kernel(c_ref, w_ref, b_ref, o_ref):
    c = c_ref[...]
    ca = c * _sigmoid(c)
    o_ref[0] = jnp.dot(ca, w_ref[0], precision=HIGHEST, preferred_element_type=F32) + b_ref[0]


def _modulation(c, w_mod, b_mod):
    depth, d, n = w_mod.shape
    bsz = c.shape[0]
    rows = -(-bsz // SUBLANES) * SUBLANES
    c_pad = jnp.zeros((rows, d), F32).at[:bsz].set(c)
    tn = n // 4
    out = pl.pallas_call(
        _mod_kernel,
        out_shape=jax.ShapeDtypeStruct((depth, rows, n), F32),
        grid=(depth, n // tn),
        in_specs=[
            pl.BlockSpec((rows, d), lambda l, j: (0, 0)),
            pl.BlockSpec((1, d, tn), lambda l, j: (l, 0, j)),
            pl.BlockSpec((1, 1, tn), lambda l, j: (l, 0, j)),
        ],
        out_specs=pl.BlockSpec((1, rows, tn), lambda l, j: (l, 0, j)),
        compiler_params=_params("parallel", "parallel"),
        name="adaln_modulation",
    )(c_pad, w_mod, b_mod.reshape(depth, 1, n))
    return out[:, :bsz].reshape(depth, bsz, 6, 1, d)


def _norm_matmul_kernel(x_ref, g_ref, sh_ref, sc_ref, w_ref, *out_refs, n_main):
    h = _norm_mod(x_ref[...], g_ref[...], sh_ref[...], sc_ref[...]).astype(BF16)
    y = jnp.dot(h, w_ref[...], preferred_element_type=F32)
    out_refs[0][...] = y[:, :n_main].astype(out_refs[0].dtype)
    if len(out_refs) > 1:
        out_refs[1][...] = y[:, n_main:]


def _norm_matmul(x, g, shift, scale, w, n_main, out_dtype, seq):
    t, d = x.shape
    n = w.shape[1]
    tm = ROW_TILE
    per_seq = seq // tm
    out_shape = [jax.ShapeDtypeStruct((t, n_main), out_dtype)]
    out_specs = [pl.BlockSpec((tm, n_main), lambda i: (i, 0))]
    if n > n_main:
        out_shape.append(jax.ShapeDtypeStruct((t, n - n_main), F32))
        out_specs.append(pl.BlockSpec((tm, n - n_main), lambda i: (i, 0)))
    return pl.pallas_call(
        functools.partial(_norm_matmul_kernel, n_main=n_main),
        out_shape=out_shape,
        grid=(t // tm,),
        in_specs=[
            pl.BlockSpec((tm, d), lambda i: (i, 0)),
            pl.BlockSpec((1, d), lambda i: (0, 0)),
            pl.BlockSpec((None, 1, d), lambda i: (i // per_seq, 0, 0)),
            pl.BlockSpec((None, 1, d), lambda i: (i // per_seq, 0, 0)),
            pl.BlockSpec((d, n), lambda i: (0, 0)),
        ],
        out_specs=out_specs,
        compiler_params=_params("parallel"),
        name="norm_mod_in_proj",
    )(x, g, shift, scale, w)


def _proj_residual_kernel(z_ref, w_ref, x_ref, gate_ref, o_ref):
    y = jnp.dot(z_ref[...].astype(BF16), w_ref[...], preferred_element_type=F32)
    o_ref[...] = x_ref[...] + gate_ref[...] * y


def _proj_residual(z, w, x, gate, seq):
    t, d = x.shape
    tm = ROW_TILE
    per_seq = seq // tm
    return pl.pallas_call(
        _proj_residual_kernel,
        out_shape=jax.ShapeDtypeStruct((t, d), F32),
        grid=(t // tm,),
        in_specs=[
            pl.BlockSpec((tm, z.shape[1]), lambda i: (i, 0)),
            pl.BlockSpec(w.shape, lambda i: (0, 0)),
            pl.BlockSpec((tm, d), lambda i: (i, 0)),
            pl.BlockSpec((None, 1, d), lambda i: (i // per_seq, 0, 0)),
        ],
        out_specs=pl.BlockSpec((tm, d), lambda i: (i, 0)),
        compiler_params=_params("parallel"),
        name="out_proj_residual",
    )(z, w, x, gate)


def _conv_mixer_kernel(b_ref, c_ref, v_ref, cw_ref, w_ref, x_ref, gate_ref, o_ref, carry_ref, *, per_seq):
    @pl.when(pl.program_id(0) % per_seq == 0)
    def _():
        carry_ref[...] = jnp.zeros_like(carry_ref)

    cv = c_ref[...] * v_ref[...]
    tm = cv.shape[0]
    k_w = cw_ref.shape[0]
    row = lax.broadcasted_iota(I32, cv.shape, 0)
    conv = cw_ref[k_w - 1:k_w, :] * cv
    for j in range(1, k_w):
        conv = conv + cw_ref[k_w - 1 - j:k_w - j, :] * _shifted_rows(cv, carry_ref, j, row)
    carry_ref[...] = cv[tm - SUBLANES:, :]
    z = (b_ref[...] * conv).astype(BF16)
    y = jnp.dot(z, w_ref[...], preferred_element_type=F32)
    o_ref[...] = x_ref[...] + gate_ref[...] * y


def _conv_mixer(bcv, conv_w, w_out, x, gate, seq):
    t, d = x.shape
    tm = ROW_TILE
    per_seq = seq // tm
    return pl.pallas_call(
        functools.partial(_conv_mixer_kernel, per_seq=per_seq),
        out_shape=jax.ShapeDtypeStruct((t, d), F32),
        grid=(t // tm,),
        in_specs=[
            pl.BlockSpec((tm, d), lambda i: (i, 0)),
            pl.BlockSpec((tm, d), lambda i: (i, 1)),
            pl.BlockSpec((tm, d), lambda i: (i, 2)),
            pl.BlockSpec(conv_w.shape, lambda i: (0, 0)),
            pl.BlockSpec(w_out.shape, lambda i: (0, 0)),
            pl.BlockSpec((tm, d), lambda i: (i, 0)),
            pl.BlockSpec((None, 1, d), lambda i: (i // per_seq, 0, 0)),
        ],
        out_specs=pl.BlockSpec((tm, d), lambda i: (i, 0)),
        scratch_shapes=[pltpu.VMEM((SUBLANES, d), F32)],
        compiler_params=_params("arbitrary"),
        name="short_conv_mixer",
    )(bcv, bcv, bcv, conv_w, w_out, x, gate)


def _lru_kernel(xb_ref, gb_ref, cw_ref, cb_ref, gw_ref, gbias_ref, ap_ref, y_ref, xcarry_ref, hcarry_ref):
    @pl.when(pl.program_id(2) == 0)
    def _():
        xcarry_ref[...] = jnp.zeros_like(xcarry_ref)
        hcarry_ref[...] = jnp.zeros_like(hcarry_ref)

    x = xb_ref[...]
    ts, blk = x.shape
    k_w = cw_ref.shape[0]
    row = lax.broadcasted_iota(I32, x.shape, 0)
    conv = cw_ref[k_w - 1:k_w, :] * x
    for j in range(1, k_w):
        conv = conv + cw_ref[k_w - 1 - j:k_w - j, :] * _shifted_rows(x, xcarry_ref, j, row)
    xcarry_ref[...] = x[ts - SUBLANES:, :]
    xc = conv + cb_ref[...]

    g = jnp.dot(xc.astype(BF16), gw_ref[...], preferred_element_type=F32) + gbias_ref[...]
    r_gate = _sigmoid(g[:, :blk])
    i_gate = _sigmoid(g[:, blk:])
    ap = ap_ref[...]
    softplus = jnp.maximum(ap, 0.0) + jnp.log(1.0 + jnp.exp(-jnp.abs(ap)))
    a = jnp.exp((-LRU_C) * r_gate * softplus)
    u = jnp.sqrt(1.0 - a * a) * (i_gate * xc)

    a_cum, h_loc = a, u
    step = 1
    while step < ts:
        a_prev = jnp.where(row < step, 1.0, pltpu.roll(a_cum, step, 0))
        h_prev = jnp.where(row < step, 0.0, pltpu.roll(h_loc, step, 0))
        h_loc = h_loc + a_cum * h_prev
        a_cum = a_cum * a_prev
        step *= 2
    hs = h_loc + a_cum * hcarry_ref[...]
    hcarry_ref[...] = hs[ts - 1:ts, :]

    gb = gb_ref[...]
    gelu = 0.5 * gb * (1.0 + jnp.tanh(GELU_C * (gb + 0.044715 * (gb * gb * gb))))
    y_ref[...] = (gelu * hs).astype(y_ref.dtype)


def _lru_core(gx, conv_w, conv_b, gate_w, gate_b, a_param, bsz, seq):
    t = gx.shape[0]
    width = gx.shape[1] // 2
    blk = width // LRU_HEADS
    ts = ROW_TILE
    per_seq = seq // ts
    return pl.pallas_call(
        _lru_kernel,
        out_shape=jax.ShapeDtypeStruct((t, width), BF16),
        grid=(bsz, LRU_HEADS, per_seq),
        in_specs=[
            pl.BlockSpec((ts, blk), lambda b, h, s: (b * per_seq + s, LRU_HEADS + h)),
            pl.BlockSpec((ts, blk), lambda b, h, s: (b * per_seq + s, h)),
            pl.BlockSpec((conv_w.shape[0], blk), lambda b, h, s: (0, h)),
            pl.BlockSpec((1, blk), lambda b, h, s: (0, h)),
            pl.BlockSpec((None, blk, 2 * blk), lambda b, h, s: (h, 0, 0)),
            pl.BlockSpec((None, 1, 2 * blk), lambda b, h, s: (h, 0, 0)),
            pl.BlockSpec((1, blk), lambda b, h, s: (0, h)),
        ],
        out_specs=pl.BlockSpec((ts, blk), lambda b, h, s: (b * per_seq + s, h)),
        scratch_shapes=[pltpu.VMEM((SUBLANES, blk), F32), pltpu.VMEM((1, blk), F32)],
        compiler_params=_params("parallel", "parallel", "arbitrary"),
        name="rglru_scan",
    )(gx, gx, conv_w, conv_b, gate_w, gate_b, a_param)


def _forget_cumsum_kernel(f_ref, bf_ref, o_ref):
    x = f_ref[...] + bf_ref[...]
    c = jnp.minimum(x, 0.0) - jnp.log(1.0 + jnp.exp(-jnp.abs(x)))
    seq = c.shape[0]
    row = lax.broadcasted_iota(I32, c.shape, 0)
    step = 1
    while step < seq:
        c = c + jnp.where(row < step, 0.0, pltpu.roll(c, step, 0))
        step *= 2
    o_ref[...] = c.T[:FOX_HEADS, :]


def _forget_cumsum(f, b_f, bsz, seq):
    return pl.pallas_call(
        _forget_cumsum_kernel,
        out_shape=jax.ShapeDtypeStruct((bsz, FOX_HEADS, seq), F32),
        grid=(bsz,),
        in_specs=[
            pl.BlockSpec((seq, LANES), lambda b: (b, 0)),
            pl.BlockSpec((1, LANES), lambda b: (0, 0)),
        ],
        out_specs=pl.BlockSpec((None, FOX_HEADS, seq), lambda b: (b, 0, 0)),
        compiler_params=_params("parallel"),
        name="forget_gate_cumsum",
    )(f, b_f)


def _fox_attn_kernel(q_ref, k_ref, v_ref, ck_ref, o_ref, m_ref, acc_ref, *, scale):
    qi = pl.program_id(2)
    tq = q_ref.shape[0]
    tk = tq
    half = q_ref.shape[1] // 2
    first_head = lax.broadcasted_iota(I32, (1, q_ref.shape[1]), 1) < half

    qs = (q_ref[...].astype(F32) * (scale * LOG2E)).astype(BF16)
    zero = jnp.zeros_like(qs)
    q_heads = (jnp.where(first_head, qs, zero), jnp.where(first_head, zero, qs))
    m_ref[...] = jnp.full_like(m_ref, NEG_BIG)
    acc_ref[...] = jnp.zeros_like(acc_ref)

    def update(j, diagonal):
        start = pl.multiple_of(j * tk, tk)
        k = k_ref[pl.ds(start, tk), :]
        v = v_ref[pl.ds(start, tk), :]
        ck = ck_ref[:, pl.ds(start, tk)] * LOG2E
        ones = jnp.ones_like(v)
        v_heads = (jnp.where(first_head, v, ones), jnp.where(first_head, ones, v))
        for hh in range(2):
            s = lax.dot_general(q_heads[hh], k, (((1,), (1,)), ((), ())), preferred_element_type=F32)
            s = s - ck[hh:hh + 1, :]
            if diagonal:
                row = lax.broadcasted_iota(I32, s.shape, 0)
                col = lax.broadcasted_iota(I32, s.shape, 1)
                s = jnp.where(col <= row, s, NEG_BIG)
            m_prev = m_ref[hh]
            m_new = jnp.maximum(m_prev, jnp.max(s, axis=-1, keepdims=True))
            alpha = jnp.exp2(m_prev - m_new)
            p = jnp.exp2(s - jnp.concatenate([m_new] * (tk // LANES), axis=1))
            m_ref[hh] = m_new
            acc_ref[hh] = alpha * acc_ref[hh] + jnp.dot(p.astype(BF16), v_heads[hh], preferred_element_type=F32)

    def body(j, carry):
        update(j, False)
        return carry

    lax.fori_loop(0, qi, body, 0)
    update(qi, True)
    a0 = acc_ref[0]
    a1 = acc_ref[1]
    o0 = a0 / pltpu.roll(a0, half, 1)
    o1 = a1 / pltpu.roll(a1, half, 1)
    o_ref[...] = jnp.where(first_head, o0, o1).astype(o_ref.dtype)


def _fox_attention(qkv, cum, bsz, seq):
    t = qkv.shape[0]
    d = qkv.shape[1] // 3
    pairs = d // LANES
    tq = ATTN_TILE
    per_seq = seq // tq
    scale = (d // FOX_HEADS) ** -0.5
    ck = cum.reshape(bsz, pairs, 2, seq)
    return pl.pallas_call(
        functools.partial(_fox_attn_kernel, scale=scale),
        out_shape=jax.ShapeDtypeStruct((t, d), BF16),
        grid=(bsz, pairs, per_seq),
        in_specs=[
            pl.BlockSpec((tq, LANES), lambda b, hp, qi: (b * per_seq + qi, hp)),
            pl.BlockSpec((seq, LANES), lambda b, hp, qi: (b, pairs + hp)),
            pl.BlockSpec((seq, LANES), lambda b, hp, qi: (b, 2 * pairs + hp)),
            pl.BlockSpec((None, None, 2, seq), lambda b, hp, qi: (b, hp, 0, 0)),
        ],
        out_specs=pl.BlockSpec((tq, LANES), lambda b, hp, qi: (b * per_seq + qi, hp)),
        scratch_shapes=[
            pltpu.VMEM((2, tq, LANES), F32),
            pltpu.VMEM((2, tq, LANES), F32),
        ],
        compiler_params=_params("parallel", "parallel", "arbitrary"),
        name="forgetting_attention",
    )(qkv, qkv, qkv, ck)


def _router_kernel(x_ref, g_ref, sh_ref, sc_ref, rw_ref, rb_ref,
                   h_ref, e_ref, gt_ref, rk_ref, cnt_ref, carry_ref):
    @pl.when(pl.program_id(0) == 0)
    def _():
        carry_ref[...] = jnp.zeros_like(carry_ref)

    h = _norm_mod(x_ref[...], g_ref[...], sh_ref[...], sc_ref[...])
    h_ref[...] = h
    logits = jnp.dot(h, rw_ref[...], precision=HIGHEST, preferred_element_type=F32) + rb_ref[...]
    tm = logits.shape[0]
    lane = lax.broadcasted_iota(I32, logits.shape, 1).astype(F32)

    work = logits
    picked = jnp.zeros(logits.shape, F32)
    vals, idxs = [], []
    for _ in range(TOP_K):
        mx = jnp.max(work, axis=-1, keepdims=True)
        idx = jnp.min(jnp.where(work == mx, lane, float(LANES)), axis=-1, keepdims=True)
        sel = lane == idx
        vals.append(mx)
        idxs.append(idx)
        picked = jnp.where(sel, 1.0, picked)
        work = jnp.where(sel, -3.0e38, work)

    ex = [jnp.exp(v - vals[0]) for v in vals]
    den = ex[0] + ex[1] + ex[2] + ex[3]

    r_i = lax.broadcasted_iota(I32, (tm, tm), 0)
    c_i = lax.broadcasted_iota(I32, (tm, tm), 1)
    earlier = jnp.where(c_i < r_i, 1.0, 0.0).astype(BF16)
    before = jnp.dot(earlier, picked.astype(BF16), preferred_element_type=F32) + carry_ref[...]
    carry_ref[...] = carry_ref[...] + jnp.sum(picked, axis=0, keepdims=True)
    cnt_ref[...] = carry_ref[...]

    e_out = jnp.zeros(logits.shape, F32)
    g_out = jnp.zeros(logits.shape, F32)
    r_out = jnp.zeros(logits.shape, F32)
    for k in range(TOP_K):
        rank = jnp.sum(jnp.where(lane == idxs[k], before, 0.0), axis=-1, keepdims=True)
        at_k = lane == float(k)
        e_out = jnp.where(at_k, idxs[k], e_out)
        g_out = jnp.where(at_k, ex[k] / den, g_out)
        r_out = jnp.where(at_k, rank, r_out)
    e_ref[...] = e_out.astype(I32)
    gt_ref[...] = g_out
    rk_ref[...] = r_out.astype(I32)


def _router(x, g, shift, scale, rw, rb, seq):
    t, d = x.shape
    tm = ROW_TILE
    per_seq = seq // tm
    slab = lambda dt: jax.ShapeDtypeStruct((t, LANES), dt)
    slab_spec = pl.BlockSpec((tm, LANES), lambda i: (i, 0))
    return pl.pallas_call(
        _router_kernel,
        out_shape=[jax.ShapeDtypeStruct((t, d), F32), slab(I32), slab(F32), slab(I32),
                   jax.ShapeDtypeStruct((1, LANES), F32)],
        grid=(t // tm,),
        in_specs=[
            pl.BlockSpec((tm, d), lambda i: (i, 0)),
            pl.BlockSpec((1, d), lambda i: (0, 0)),
            pl.BlockSpec((None, 1, d), lambda i: (i // per_seq, 0, 0)),
            pl.BlockSpec((None, 1, d), lambda i: (i // per_seq, 0, 0)),
            pl.BlockSpec((d, LANES), lambda i: (0, 0)),
            pl.BlockSpec((1, LANES), lambda i: (0, 0)),
        ],
        out_specs=[pl.BlockSpec((tm, d), lambda i: (i, 0)), slab_spec, slab_spec, slab_spec,
                   pl.BlockSpec((1, LANES), lambda i: (0, 0))],
        scratch_shapes=[pltpu.VMEM((1, LANES), F32)],
        compiler_params=_params("arbitrary"),
        name="moe_router",
    )(x, g, shift, scale, rw, rb)


def _expert_kernel(te_ref, nu_ref, slot_ref, h_hbm, wgu_ref, bgu_ref, wd_ref, bd_ref, y_hbm,
                   xs_a, xs_b, y_a, y_b, wgu_bf, wd_bf, gsem_a, gsem_b, ssem_a, ssem_b, zsem, *, n_tok):
    i = pl.program_id(0)
    n_steps = pl.num_programs(0)
    n_used = nu_ref[0]
    tile = xs_a.shape[0]
    n_slots = n_tok * TOP_K
    slot_shift = TOP_K.bit_length() - 1

    def start_gather(tile_idx, xs, sem):
        base = (tile_idx + 1) * tile
        for j in range(tile):
            tok = jnp.minimum(lax.shift_right_logical(slot_ref[base + j], slot_shift), n_tok - 1)
            pltpu.make_async_copy(h_hbm.at[pl.ds(tok, 1)], xs.at[pl.ds(j, 1)], sem).start()

    def wait_gather(xs, sem):
        pltpu.make_async_copy(h_hbm.at[pl.ds(0, tile)], xs, sem).wait()

    def start_scatter(tile_idx, y, sem):
        base = (tile_idx + 1) * tile
        for j in range(tile):
            pltpu.make_async_copy(y.at[pl.ds(j, 1)], y_hbm.at[pl.ds(slot_ref[base + j], 1)], sem).start()

    def wait_scatter(y, sem):
        pltpu.make_async_copy(y, y_hbm.at[pl.ds(0, tile)], sem).wait()

    @pl.when(i == 0)
    def _():
        start_gather(0, xs_a, gsem_a)
        y_a[...] = jnp.zeros_like(y_a)
        y_b[...] = jnp.zeros_like(y_b)
        n_fill = (y_hbm.shape[0] - n_slots) // tile
        for z in range(n_fill):
            pltpu.make_async_copy(y_a, y_hbm.at[pl.ds(n_slots + z * tile, tile)], zsem).start()
        for z in range(n_fill):
            pltpu.make_async_copy(y_a, y_hbm.at[pl.ds(n_slots + z * tile, tile)], zsem).wait()

    @pl.when(jnp.logical_or(i == 0, te_ref[i] != te_ref[jnp.maximum(i - 1, 0)]))
    def _():
        wgu_bf[...] = wgu_ref[...].astype(BF16)
        wd_bf[...] = wd_ref[...].astype(BF16)

    def tile_step(xs, y, gsem, ssem, xs_next, y_prev, gsem_next, ssem_prev):
        @pl.when(i >= 1)
        def _():
            wait_scatter(y, ssem)

        wait_gather(xs, gsem)
        start_gather(jnp.minimum(i + 1, n_steps - 1), xs_next, gsem_next)
        start_scatter(i - 1, y_prev, ssem_prev)
        f = wd_ref.shape[0]
        gu = jnp.dot(xs[...].astype(BF16), wgu_bf[...], preferred_element_type=F32) + bgu_ref[...]
        gate = jnp.minimum(gu[:, :f], SWIGLU_LIMIT)
        up = jnp.clip(gu[:, f:], -SWIGLU_LIMIT, SWIGLU_LIMIT)
        act = gate * _sigmoid(SWIGLU_ALPHA * gate) * (up + 1.0)
        y[...] = jnp.dot(act.astype(BF16), wd_bf[...], preferred_element_type=F32) + bd_ref[...]

        @pl.when(i == n_used - 1)
        def _():
            start_scatter(i, y, ssem)
            wait_scatter(y, ssem)
            wait_scatter(y_prev, ssem_prev)
            wait_gather(xs_next, gsem_next)

    @pl.when(jnp.logical_and(i < n_used, i % 2 == 0))
    def _():
        tile_step(xs_a, y_a, gsem_a, ssem_a, xs_b, y_b, gsem_b, ssem_b)

    @pl.when(jnp.logical_and(i < n_used, i % 2 == 1))
    def _():
        tile_step(xs_b, y_b, gsem_b, ssem_b, xs_a, y_a, gsem_a, ssem_a)


def _expert_mlp(tile_e, n_used, row_slot, h, layer, wgu, bgu, wd, bd):
    n_tok, d = h.shape
    tile = EXPERT_TILE
    f = wd.shape[2]
    n_tiles = row_slot.shape[0] // tile
    n_out = n_tok * TOP_K + N_EXPERTS * tile + tile
    spare = n_out - tile + jnp.arange(tile, dtype=I32)
    row_slot = jnp.concatenate([spare, row_slot])
    exp_map = lambda i, te, nu, sl: (layer, te[i], 0, 0)
    grid_spec = pltpu.PrefetchScalarGridSpec(
        num_scalar_prefetch=3,
        grid=(n_tiles,),
        in_specs=[
            pl.BlockSpec(memory_space=pl.ANY),
            pl.BlockSpec((None, None, d, 2 * f), exp_map),
            pl.BlockSpec((None, None, 1, 2 * f), exp_map),
            pl.BlockSpec((None, None, f, d), exp_map),
            pl.BlockSpec((None, None, 1, d), exp_map),
        ],
        out_specs=pl.BlockSpec(memory_space=pl.ANY),
        scratch_shapes=[
            pltpu.VMEM((tile, d), F32),
            pltpu.VMEM((tile, d), F32),
            pltpu.VMEM((tile, d), F32),
            pltpu.VMEM((tile, d), F32),
            pltpu.VMEM((d, 2 * f), BF16),
            pltpu.VMEM((f, d), BF16),
        ] + [pltpu.SemaphoreType.DMA] * 5,
    )
    return pl.pallas_call(
        functools.partial(_expert_kernel, n_tok=n_tok),
        out_shape=jax.ShapeDtypeStruct((n_out, d), F32),
        grid_spec=grid_spec,
        compiler_params=_params("arbitrary"),
        name="moe_expert_mlp",
    )(tile_e, n_used, row_slot, h, wgu, bgu[:, :, None, :], wd, bd[:, :, None, :])


def _combine_kernel(x_ref, gt_ref, gate_ref, y_ref, o_ref):
    d = x_ref.shape[1]
    gates = gt_ref[...]
    y = gates[:, 0:1] * y_ref[:, 0:d]
    for k in range(1, TOP_K):
        y = y + gates[:, k:k + 1] * y_ref[:, k * d:(k + 1) * d]
    o_ref[...] = x_ref[...] + gate_ref[...] * y


def _combine(x, gates, gate2, y_slots, seq):
    t, d = x.shape
    tc = COMBINE_TILE
    per_seq = seq // tc
    y_rows = y_slots.reshape(y_slots.shape[0] // TOP_K, TOP_K * d)
    return pl.pallas_call(
        _combine_kernel,
        out_shape=jax.ShapeDtypeStruct((t, d), F32),
        grid=(t // tc,),
        in_specs=[
            pl.BlockSpec((tc, d), lambda i: (i, 0)),
            pl.BlockSpec((tc, LANES), lambda i: (i, 0)),
            pl.BlockSpec((None, 1, d), lambda i: (i // per_seq, 0, 0)),
            pl.BlockSpec((tc, TOP_K * d), lambda i: (i, 0)),
        ],
        out_specs=pl.BlockSpec((tc, d), lambda i: (i, 0)),
        compiler_params=_params("parallel"),
        name="moe_combine_residual",
    )(x, gates, gate2, y_rows)


def _moe_layer(x, g, shift, scale, gate2, rw, rb, layer, wgu, bgu, wd, bd, seq):
    t, d = x.shape
    n_exp = rw.shape[1]
    rw_pad = jnp.zeros((d, LANES), F32).at[:, :n_exp].set(rw)
    rb_pad = jnp.full((1, LANES), NEG_BIG, F32).at[0, :n_exp].set(rb)
    h, e_slab, gt_slab, rk_slab, cnt = _router(x, g, shift, scale, rw_pad, rb_pad, seq)

    tile = EXPERT_TILE
    n_tiles = t * TOP_K // tile + n_exp
    counts = cnt[0, :n_exp].astype(I32)
    padded = (counts + tile - 1) // tile * tile
    pad_end = jnp.cumsum(padded)
    pad_start = pad_end - padded
    dest = (pad_start[e_slab[:, :TOP_K]] + rk_slab[:, :TOP_K]).reshape(-1)
    n_used = pad_end[-1] // tile
    tile_ids = jnp.arange(n_tiles, dtype=I32)
    first_row = jnp.minimum(tile_ids, n_used - 1) * tile
    tile_e = jnp.sum((pad_end[None, :] <= first_row[:, None]).astype(I32), axis=1)

    n_slots = t * TOP_K
    row_slot = jnp.full((n_tiles * tile,), -1, I32).at[dest].set(jnp.arange(n_slots, dtype=I32))
    is_pad = row_slot < 0
    row_slot = jnp.where(is_pad, n_slots - 1 + jnp.cumsum(is_pad.astype(I32)), row_slot)

    y_slots = _expert_mlp(tile_e, n_used.reshape(1), row_slot, h, layer, wgu, bgu, wd, bd)
    return _combine(x, gt_slab, gate2, y_slots, seq)


def _final_norm_kernel(x_ref, g_ref, o_ref):
    x = x_ref[...]
    ms = jnp.mean(x * x, axis=-1, keepdims=True)
    o_ref[...] = x * lax.rsqrt(ms + RMS_EPS) * g_ref[...]


def _final_norm(x, g):
    t, d = x.shape
    tm = ROW_TILE
    return pl.pallas_call(
        _final_norm_kernel,
        out_shape=jax.ShapeDtypeStruct((t, d), F32),
        grid=(t // tm,),
        in_specs=[pl.BlockSpec((tm, d), lambda i: (i, 0)), pl.BlockSpec((1, d), lambda i: (0, 0))],
        out_specs=pl.BlockSpec((tm, d), lambda i: (i, 0)),
        compiler_params=_params("parallel"),
        name="final_rmsnorm",
    )(x, g)


def kernel(x, c, norm_mix_g, norm_ffn_g, w_mod, b_mod, conv_w_in, conv_w, conv_w_out, lru_w_in, lru_conv_w, lru_conv_b, lru_gate_w, lru_gate_b, lru_a_param, lru_w_out, fox_w_in, fox_b_f, fox_w_out, router_w, router_b, moe_w_gate_up, moe_b_gate_up, moe_w_down, moe_b_down, final_g):
    bsz, seq, d = x.shape
    depth = w_mod.shape[0]
    t = bsz * seq
    mod = _modulation(c, w_mod, b_mod)
    xt = x.reshape(t, d)
    for i in range(depth):
        shift1, scale1, gate1 = mod[i, :, 0], mod[i, :, 1], mod[i, :, 2]
        shift2, scale2, gate2 = mod[i, :, 3], mod[i, :, 4], mod[i, :, 5]
        g_mix = norm_mix_g[i][None, :]
        kind, j = i % 3, i // 3
        if kind == 0:
            bcv = _norm_matmul(xt, g_mix, shift1, scale1, conv_w_in[j].astype(BF16), 3 * d, F32, seq)[0]
            xt = _conv_mixer(bcv, conv_w[j], conv_w_out[j].astype(BF16), xt, gate1, seq)
        elif kind == 1:
            gx = _norm_matmul(xt, g_mix, shift1, scale1, lru_w_in[j].astype(BF16), lru_w_in.shape[2], F32, seq)[0]
            y = _lru_core(gx, lru_conv_w[j], lru_conv_b[j][None, :], lru_gate_w[j].astype(BF16),
                          lru_gate_b[j][:, None, :], lru_a_param[j][None, :], bsz, seq)
            xt = _proj_residual(y, lru_w_out[j].astype(BF16), xt, gate1, seq)
        else:
            w_in = jnp.zeros((d, 3 * d + LANES), F32).at[:, :3 * d + FOX_HEADS].set(fox_w_in[j]).astype(BF16)
            qkv, f = _norm_matmul(xt, g_mix, shift1, scale1, w_in, 3 * d, BF16, seq)
            b_f = jnp.zeros((1, LANES), F32).at[0, :FOX_HEADS].set(fox_b_f[j])
            cum = _forget_cumsum(f, b_f, bsz, seq)
            o = _fox_attention(qkv, cum, bsz, seq)
            xt = _proj_residual(o, fox_w_out[j].astype(BF16), xt, gate1, seq)
        xt = _moe_layer(xt, norm_ffn_g[i][None, :], shift2, scale2, gate2, router_w[i], router_b[i],
                        i, moe_w_gate_up, moe_b_gate_up, moe_w_down, moe_b_down, seq)
    return _final_norm(xt, final_g[None, :]).reshape(bsz, seq, d)
```

```python
import functools

import jax
import jax.numpy as jnp
from jax import lax
from jax.experimental import pallas as pl
from jax.experimental.pallas import tpu as pltpu

F32 = jnp.float32
BF16 = jnp.bfloat16
I32 = jnp.int32
HIGHEST = lax.Precision.HIGHEST

RMS_EPS = 1e-6
LRU_HEADS = 4
LRU_C = 8.0
FOX_HEADS = 16
N_EXPERTS = 32
TOP_K = 4
SWIGLU_LIMIT = 7.0
SWIGLU_ALPHA = 1.702
NEG_BIG = -1e30
GELU_C = 0.7978845608028654
LOG2E = 1.4426950408889634

LANES = 128
SUBLANES = 8
VMEM_LIMIT = 56 * 1024 * 1024

ROW_TILE = 512
EXPERT_TILE = 256
COMBINE_TILE = 256
ATTN_TILE = 512


def _params(*sem):
    return pltpu.CompilerParams(dimension_semantics=sem, vmem_limit_bytes=VMEM_LIMIT)


def _sigmoid(z):
    return 1.0 / (1.0 + jnp.exp(-z))


def _norm_mod(x, g, shift, scale):
    ms = jnp.mean(x * x, axis=-1, keepdims=True)
    y = x * lax.rsqrt(ms + RMS_EPS) * g
    return y * (1.0 + scale) + shift


def _store_row_tiles(ref, x):
    for s in range(ref.shape[1]):
        ref[:, s, :] = x[:, s * LANES:(s + 1) * LANES]


def _load_row_tiles(ref):
    return jnp.concatenate([ref[:, s, :] for s in range(ref.shape[1])], axis=1)


def _shifted_rows(u, carry_ref, j, row):
    sh = pltpu.roll(u, j, 0)
    for r in range(j):
        src = SUBLANES - j + r
        sh = jnp.where(row == r, carry_ref[src:src + 1, :], sh)
    return sh


def _mod_kernel(c_ref, w_ref, b_ref, o_ref):
    c = c_ref[...]
    ca = c * _sigmoid(c)
    o_ref[0] = jnp.dot(ca, w_ref[0], precision=HIGHEST, preferred_element_type=F32) + b_ref[0]


def _modulation(c, w_mod, b_mod):
    depth, d, n = w_mod.shape
    bsz = c.shape[0]
    rows = -(-bsz // SUBLANES) * SUBLANES
    c_pad = jnp.zeros((rows, d), F32).at[:bsz].set(c)
    tn = n // 4
    out = pl.pallas_call(
        _mod_kernel,
        out_shape=jax.ShapeDtypeStruct((depth, rows, n), F32),
        grid=(depth, n // tn),
        in_specs=[
            pl.BlockSpec((rows, d), lambda l, j: (0, 0)),
            pl.BlockSpec((1, d, tn), lambda l, j: (l, 0, j)),
            pl.BlockSpec((1, 1, tn), lambda l, j: (l, 0, j)),
        ],
        out_specs=pl.BlockSpec((1, rows, tn), lambda l, j: (l, 0, j)),
        compiler_params=_params("parallel", "parallel"),
        name="adaln_modulation",
    )(c_pad, w_mod, b_mod.reshape(depth, 1, n))
    return out[:, :bsz].reshape(depth, bsz, 6, 1, d)


def _norm_matmul_kernel(x_ref, g_ref, sh_ref, sc_ref, w_ref, *out_refs, n_main):
    h = _norm_mod(x_ref[...], g_ref[...], sh_ref[...], sc_ref[...]).astype(BF16)
    y = jnp.dot(h, w_ref[...], preferred_element_type=F32)
    out_refs[0][...] = y[:, :n_main].astype(out_refs[0].dtype)
    if len(out_refs) > 1:
        out_refs[1][...] = y[:, n_main:]


def _norm_matmul(x, g, shift, scale, w, n_main, out_dtype, seq):
    t, d = x.shape
    n = w.shape[1]
    tm = ROW_TILE
    per_seq = seq // tm
    out_shape = [jax.ShapeDtypeStruct((t, n_main), out_dtype)]
    out_specs = [pl.BlockSpec((tm, n_main), lambda i: (i, 0))]
    if n > n_main:
        out_shape.append(jax.ShapeDtypeStruct((t, n - n_main), F32))
        out_specs.append(pl.BlockSpec((tm, n - n_main), lambda i: (i, 0)))
    return pl.pallas_call(
        functools.partial(_norm_matmul_kernel, n_main=n_main),
        out_shape=out_shape,
        grid=(t // tm,),
        in_specs=[
            pl.BlockSpec((tm, d), lambda i: (i, 0)),
            pl.BlockSpec((1, d), lambda i: (0, 0)),
            pl.BlockSpec((None, 1, d), lambda i: (i // per_seq, 0, 0)),
            pl.BlockSpec((None, 1, d), lambda i: (i // per_seq, 0, 0)),
            pl.BlockSpec((d, n), lambda i: (0, 0)),
        ],
        out_specs=out_specs,
        compiler_params=_params("parallel"),
        name="norm_mod_in_proj",
    )(x, g, shift, scale, w)


def _proj_residual_kernel(z_ref, w_ref, x_ref, gate_ref, o_ref):
    y = jnp.dot(z_ref[...].astype(BF16), w_ref[...], preferred_element_type=F32)
    o_ref[...] = x_ref[...] + gate_ref[...] * y


def _proj_residual(z, w, x, gate, seq):
    t, d = x.shape
    tm = ROW_TILE
    per_seq = seq // tm
    return pl.pallas_call(
        _proj_residual_kernel,
        out_shape=jax.ShapeDtypeStruct((t, d), F32),
        grid=(t // tm,),
        in_specs=[
            pl.BlockSpec((tm, z.shape[1]), lambda i: (i, 0)),
            pl.BlockSpec(w.shape, lambda i: (0, 0)),
            pl.BlockSpec((tm, d), lambda i: (i, 0)),
            pl.BlockSpec((None, 1, d), lambda i: (i // per_seq, 0, 0)),
        ],
        out_specs=pl.BlockSpec((tm, d), lambda i: (i, 0)),
        compiler_params=_params("parallel"),
        name="out_proj_residual",
    )(z, w, x, gate)


def _conv_mixer_kernel(b_ref, c_ref, v_ref, cw_ref, w_ref, x_ref, gate_ref, o_ref, carry_ref, *, per_seq):
    @pl.when(pl.program_id(0) % per_seq == 0)
    def _():
        carry_ref[...] = jnp.zeros_like(carry_ref)

    cv = c_ref[...] * v_ref[...]
    tm = cv.shape[0]
    k_w = cw_ref.shape[0]
    row = lax.broadcasted_iota(I32, cv.shape, 0)
    conv = cw_ref[k_w - 1:k_w, :] * cv
    for j in range(1, k_w):
        conv = conv + cw_ref[k_w - 1 - j:k_w - j, :] * _shifted_rows(cv, carry_ref, j, row)
    carry_ref[...] = cv[tm - SUBLANES:, :]
    z = (b_ref[...] * conv).astype(BF16)
    y = jnp.dot(z, w_ref[...], preferred_element_type=F32)
    o_ref[...] = x_ref[...] + gate_ref[...] * y


def _conv_mixer(bcv, conv_w, w_out, x, gate, seq):
    t, d = x.shape
    tm = ROW_TILE
    per_seq = seq // tm
    return pl.pallas_call(
        functools.partial(_conv_mixer_kernel, per_seq=per_seq),
        out_shape=jax.ShapeDtypeStruct((t, d), F32),
        grid=(t // tm,),
        in_specs=[
            pl.BlockSpec((tm, d), lambda i: (i, 0)),
            pl.BlockSpec((tm, d), lambda i: (i, 1)),
            pl.BlockSpec((tm, d), lambda i: (i, 2)),
            pl.BlockSpec(conv_w.shape, lambda i: (0, 0)),
            pl.BlockSpec(w_out.shape, lambda i: (0, 0)),
            pl.BlockSpec((tm, d), lambda i: (i, 0)),
            pl.BlockSpec((None, 1, d), lambda i: (i // per_seq, 0, 0)),
        ],
        out_specs=pl.BlockSpec((tm, d), lambda i: (i, 0)),
        scratch_shapes=[pltpu.VMEM((SUBLANES, d), F32)],
        compiler_params=_params("arbitrary"),
        name="short_conv_mixer",
    )(bcv, bcv, bcv, conv_w, w_out, x, gate)


def _lru_kernel(xb_ref, gb_ref, cw_ref, cb_ref, gw_ref, gbias_ref, ap_ref, y_ref, xcarry_ref, hcarry_ref):
    @pl.when(pl.program_id(2) == 0)
    def _():
        xcarry_ref[...] = jnp.zeros_like(xcarry_ref)
        hcarry_ref[...] = jnp.zeros_like(hcarry_ref)

    x = xb_ref[...]
    ts, blk = x.shape
    k_w = cw_ref.shape[0]
    row = lax.broadcasted_iota(I32, x.shape, 0)
    conv = cw_ref[k_w - 1:k_w, :] * x
    for j in range(1, k_w):
        conv = conv + cw_ref[k_w - 1 - j:k_w - j, :] * _shifted_rows(x, xcarry_ref, j, row)
    xcarry_ref[...] = x[ts - SUBLANES:, :]
    xc = conv + cb_ref[...]

    g = jnp.dot(xc.astype(BF16), gw_ref[...], preferred_element_type=F32) + gbias_ref[...]
    r_gate = _sigmoid(g[:, :blk])
    i_gate = _sigmoid(g[:, blk:])
    ap = ap_ref[...]
    softplus = jnp.maximum(ap, 0.0) + jnp.log(1.0 + jnp.exp(-jnp.abs(ap)))
    a = jnp.exp((-LRU_C) * r_gate * softplus)
    u = jnp.sqrt(1.0 - a * a) * (i_gate * xc)

    a_cum, h_loc = a, u
    step = 1
    while step < ts:
        a_prev = jnp.where(row < step, 1.0, pltpu.roll(a_cum, step, 0))
        h_prev = jnp.where(row < step, 0.0, pltpu.roll(h_loc, step, 0))
        h_loc = h_loc + a_cum * h_prev
        a_cum = a_cum * a_prev
        step *= 2
    hs = h_loc + a_cum * hcarry_ref[...]
    hcarry_ref[...] = hs[ts - 1:ts, :]

    gb = gb_ref[...]
    gelu = 0.5 * gb * (1.0 + jnp.tanh(GELU_C * (gb + 0.044715 * (gb * gb * gb))))
    y_ref[...] = (gelu * hs).astype(y_ref.dtype)


def _lru_core(gx, conv_w, conv_b, gate_w, gate_b, a_param, bsz, seq):
    t = gx.shape[0]
    width = gx.shape[1] // 2
    blk = width // LRU_HEADS
    ts = ROW_TILE
    per_seq = seq // ts
    return pl.pallas_call(
        _lru_kernel,
        out_shape=jax.ShapeDtypeStruct((t, width), BF16),
        grid=(bsz, LRU_HEADS, per_seq),
        in_specs=[
            pl.BlockSpec((ts, blk), lambda b, h, s: (b * per_seq + s, LRU_HEADS + h)),
            pl.BlockSpec((ts, blk), lambda b, h, s: (b * per_seq + s, h)),
            pl.BlockSpec((conv_w.shape[0], blk), lambda b, h, s: (0, h)),
            pl.BlockSpec((1, blk), lambda b, h, s: (0, h)),
            pl.BlockSpec((None, blk, 2 * blk), lambda b, h, s: (h, 0, 0)),
            pl.BlockSpec((None, 1, 2 * blk), lambda b, h, s: (h, 0, 0)),
            pl.BlockSpec((1, blk), lambda b, h, s: (0, h)),
        ],
        out_specs=pl.BlockSpec((ts, blk), lambda b, h, s: (b * per_seq + s, h)),
        scratch_shapes=[pltpu.VMEM((SUBLANES, blk), F32), pltpu.VMEM((1, blk), F32)],
        compiler_params=_params("parallel", "parallel", "arbitrary"),
        name="rglru_scan",
    )(gx, gx, conv_w, conv_b, gate_w, gate_b, a_param)


def _forget_cumsum_kernel(f_ref, bf_ref, o_ref):
    x = f_ref[...] + bf_ref[...]
    c = jnp.minimum(x, 0.0) - jnp.log(1.0 + jnp.exp(-jnp.abs(x)))
    seq = c.shape[0]
    row = lax.broadcasted_iota(I32, c.shape, 0)
    step = 1
    while step < seq:
        c = c + jnp.where(row < step, 0.0, pltpu.roll(c, step, 0))
        step *= 2
    o_ref[...] = c.T[:FOX_HEADS, :]


def _forget_cumsum(f, b_f, bsz, seq):
    return pl.pallas_call(
        _forget_cumsum_kernel,
        out_shape=jax.ShapeDtypeStruct((bsz, FOX_HEADS, seq), F32),
        grid=(bsz,),
        in_specs=[
            pl.BlockSpec((seq, LANES), lambda b: (b, 0)),
            pl.BlockSpec((1, LANES), lambda b: (0, 0)),
        ],
        out_specs=pl.BlockSpec((None, FOX_HEADS, seq), lambda b: (b, 0, 0)),
        compiler_params=_params("parallel"),
        name="forget_gate_cumsum",
    )(f, b_f)


def _fox_attn_kernel(q_ref, k_ref, v_ref, ck_ref, o_ref, m_ref, acc_ref, *, scale):
    qi = pl.program_id(2)
    tq = q_ref.shape[0]
    tk = tq
    half = q_ref.shape[1] // 2
    first_head = lax.broadcasted_iota(I32, (1, q_ref.shape[1]), 1) < half

    qs = (q_ref[...].astype(F32) * (scale * LOG2E)).astype(BF16)
    zero = jnp.zeros_like(qs)
    q_heads = (jnp.where(first_head, qs, zero), jnp.where(first_head, zero, qs))
    m_ref[...] = jnp.full_like(m_ref, NEG_BIG)
    acc_ref[...] = jnp.zeros_like(acc_ref)

    def update(j, diagonal):
        start = pl.multiple_of(j * tk, tk)
        k = k_ref[pl.ds(start, tk), :]
        v = v_ref[pl.ds(start, tk), :]
        ck = ck_ref[:, pl.ds(start, tk)] * LOG2E
        ones = jnp.ones_like(v)
        v_heads = (jnp.where(first_head, v, ones), jnp.where(first_head, ones, v))
        for hh in range(2):
            s = lax.dot_general(q_heads[hh], k, (((1,), (1,)), ((), ())), preferred_element_type=F32)
            s = s - ck[hh:hh + 1, :]
            if diagonal:
                row = lax.broadcasted_iota(I32, s.shape, 0)
                col = lax.broadcasted_iota(I32, s.shape, 1)
                s = jnp.where(col <= row, s, NEG_BIG)
            m_prev = m_ref[hh]
            m_new = jnp.maximum(m_prev, jnp.max(s, axis=-1, keepdims=True))
            alpha = jnp.exp2(m_prev - m_new)
            p = jnp.exp2(s - jnp.concatenate([m_new] * (tk // LANES), axis=1))
            m_ref[hh] = m_new
            acc_ref[hh] = alpha * acc_ref[hh] + jnp.dot(p.astype(BF16), v_heads[hh], preferred_element_type=F32)

    def body(j, carry):
        update(j, False)
        return carry

    lax.fori_loop(0, qi, body, 0)
    update(qi, True)
    a0 = acc_ref[0]
    a1 = acc_ref[1]
    o0 = a0 / pltpu.roll(a0, half, 1)
    o1 = a1 / pltpu.roll(a1, half, 1)
    o_ref[...] = jnp.where(first_head, o0, o1).astype(o_ref.dtype)


def _fox_attention(qkv, cum, bsz, seq):
    t = qkv.shape[0]
    d = qkv.shape[1] // 3
    pairs = d // LANES
    tq = ATTN_TILE
    per_seq = seq // tq
    scale = (d // FOX_HEADS) ** -0.5
    ck = cum.reshape(bsz, pairs, 2, seq)
    return pl.pallas_call(
        functools.partial(_fox_attn_kernel, scale=scale),
        out_shape=jax.ShapeDtypeStruct((t, d), BF16),
        grid=(bsz, pairs, per_seq),
        in_specs=[
            pl.BlockSpec((tq, LANES), lambda b, hp, qi: (b * per_seq + qi, hp)),
            pl.BlockSpec((seq, LANES), lambda b, hp, qi: (b, pairs + hp)),
            pl.BlockSpec((seq, LANES), lambda b, hp, qi: (b, 2 * pairs + hp)),
            pl.BlockSpec((None, None, 2, seq), lambda b, hp, qi: (b, hp, 0, 0)),
        ],
        out_specs=pl.BlockSpec((tq, LANES), lambda b, hp, qi: (b * per_seq + qi, hp)),
        scratch_shapes=[
            pltpu.VMEM((2, tq, LANES), F32),
            pltpu.VMEM((2, tq, LANES), F32),
        ],
        compiler_params=_params("parallel", "parallel", "arbitrary"),
        name="forgetting_attention",
    )(qkv, qkv, qkv, ck)


def _router_kernel(x_ref, g_ref, sh_ref, sc_ref, rw_ref, rb_ref,
                   h_ref, e_ref, gt_ref, rk_ref, cnt_ref, carry_ref):
    @pl.when(pl.program_id(0) == 0)
    def _():
        carry_ref[...] = jnp.zeros_like(carry_ref)

    h = _norm_mod(x_ref[...], g_ref[...], sh_ref[...], sc_ref[...])
    _store_row_tiles(h_ref, h)
    logits = jnp.dot(h, rw_ref[...], precision=HIGHEST, preferred_element_type=F32) + rb_ref[...]
    tm = logits.shape[0]
    lane = lax.broadcasted_iota(I32, logits.shape, 1).astype(F32)

    work = logits
    picked = jnp.zeros(logits.shape, F32)
    vals, idxs = [], []
    for _ in range(TOP_K):
        mx = jnp.max(work, axis=-1, keepdims=True)
        idx = jnp.min(jnp.where(work == mx, lane, float(LANES)), axis=-1, keepdims=True)
        sel = lane == idx
        vals.append(mx)
        idxs.append(idx)
        picked = jnp.where(sel, 1.0, picked)
        work = jnp.where(sel, -3.0e38, work)

    ex = [jnp.exp(v - vals[0]) for v in vals]
    den = ex[0] + ex[1] + ex[2] + ex[3]

    r_i = lax.broadcasted_iota(I32, (tm, tm), 0)
    c_i = lax.broadcasted_iota(I32, (tm, tm), 1)
    earlier = jnp.where(c_i < r_i, 1.0, 0.0).astype(BF16)
    before = jnp.dot(earlier, picked.astype(BF16), preferred_element_type=F32) + carry_ref[...]
    carry_ref[...] = carry_ref[...] + jnp.sum(picked, axis=0, keepdims=True)
    cnt_ref[...] = carry_ref[...]

    e_out = jnp.zeros(logits.shape, F32)
    g_out = jnp.zeros(logits.shape, F32)
    r_out = jnp.zeros(logits.shape, F32)
    for k in range(TOP_K):
        rank = jnp.sum(jnp.where(lane == idxs[k], before, 0.0), axis=-1, keepdims=True)
        at_k = lane == float(k)
        e_out = jnp.where(at_k, idxs[k], e_out)
        g_out = jnp.where(at_k, ex[k] / den, g_out)
        r_out = jnp.where(at_k, rank, r_out)
    e_ref[...] = e_out.astype(I32)
    gt_ref[...] = g_out
    rk_ref[...] = r_out.astype(I32)


def _router(x, g, shift, scale, rw, rb, seq):
    t, d = x.shape
    tm = ROW_TILE
    per_seq = seq // tm
    slab = lambda dt: jax.ShapeDtypeStruct((t, LANES), dt)
    slab_spec = pl.BlockSpec((tm, LANES), lambda i: (i, 0))
    return pl.pallas_call(
        _router_kernel,
        out_shape=[jax.ShapeDtypeStruct((t, d // LANES, LANES), F32), slab(I32), slab(F32), slab(I32),
                   jax.ShapeDtypeStruct((1, LANES), F32)],
        grid=(t // tm,),
        in_specs=[
            pl.BlockSpec((tm, d), lambda i: (i, 0)),
            pl.BlockSpec((1, d), lambda i: (0, 0)),
            pl.BlockSpec((None, 1, d), lambda i: (i // per_seq, 0, 0)),
            pl.BlockSpec((None, 1, d), lambda i: (i // per_seq, 0, 0)),
            pl.BlockSpec((d, LANES), lambda i: (0, 0)),
            pl.BlockSpec((1, LANES), lambda i: (0, 0)),
        ],
        out_specs=[pl.BlockSpec((tm, d // LANES, LANES), lambda i: (i, 0, 0)), slab_spec, slab_spec, slab_spec,
                   pl.BlockSpec((1, LANES), lambda i: (0, 0))],
        scratch_shapes=[pltpu.VMEM((1, LANES), F32)],
        compiler_params=_params("arbitrary"),
        name="moe_router",
    )(x, g, shift, scale, rw, rb)


def _expert_kernel(te_ref, nu_ref, slot_ref, h_hbm, wgu_ref, bgu_ref, wd_ref, bd_ref, y_hbm,
                   xs_a, xs_b, y_a, y_b, wgu_bf, wd_bf, gsem_a, gsem_b, ssem_a, ssem_b, zsem, *, n_tok):
    i = pl.program_id(0)
    n_steps = pl.num_programs(0)
    n_used = nu_ref[0]
    tile = xs_a.shape[0]
    n_slots = n_tok * TOP_K

    def token_of(slot):
        if n_tok & (n_tok - 1) == 0:
            return jnp.bitwise_and(slot, n_tok - 1)
        return lax.rem(slot, n_tok)

    def start_gather(tile_idx, xs, sem):
        base = (tile_idx + 1) * tile
        for j in range(tile):
            tok = token_of(slot_ref[base + j])
            pltpu.make_async_copy(h_hbm.at[tok], xs.at[j], sem).start(priority=j % 2)

    def wait_gather(xs, sem):
        pltpu.make_async_copy(h_hbm.at[pl.ds(0, tile)], xs, sem).wait()

    def start_scatter(tile_idx, y, sem):
        base = (tile_idx + 1) * tile
        for j in range(tile):
            pltpu.make_async_copy(y.at[j], y_hbm.at[slot_ref[base + j]], sem).start(priority=j % 2)

    def wait_scatter(y, sem):
        pltpu.make_async_copy(y, y_hbm.at[pl.ds(0, tile)], sem).wait()

    @pl.when(i == 0)
    def _():
        start_gather(0, xs_a, gsem_a)
        y_a[...] = jnp.zeros_like(y_a)
        y_b[...] = jnp.zeros_like(y_b)
        n_fill = (y_hbm.shape[0] - n_slots) // tile
        for z in range(n_fill):
            pltpu.make_async_copy(y_a, y_hbm.at[pl.ds(n_slots + z * tile, tile)], zsem).start()
        for z in range(n_fill):
            pltpu.make_async_copy(y_a, y_hbm.at[pl.ds(n_slots + z * tile, tile)], zsem).wait()

    @pl.when(jnp.logical_or(i == 0, te_ref[i] != te_ref[jnp.maximum(i - 1, 0)]))
    def _():
        wgu_bf[...] = wgu_ref[...].astype(BF16)
        wd_bf[...] = wd_ref[...].astype(BF16)

    def tile_step(xs, y, gsem, ssem, xs_next, y_prev, gsem_next, ssem_prev):
        @pl.when(i >= 1)
        def _():
            wait_scatter(y, ssem)

        wait_gather(xs, gsem)
        start_gather(jnp.minimum(i + 1, n_steps - 1), xs_next, gsem_next)
        start_scatter(i - 1, y_prev, ssem_prev)
        f = wd_ref.shape[0]
        gu = jnp.dot(_load_row_tiles(xs).astype(BF16), wgu_bf[...], preferred_element_type=F32) + bgu_ref[...]
        gate = jnp.minimum(gu[:, :f], SWIGLU_LIMIT)
        up = jnp.clip(gu[:, f:], -SWIGLU_LIMIT, SWIGLU_LIMIT)
        act = gate * _sigmoid(SWIGLU_ALPHA * gate) * (up + 1.0)
        _store_row_tiles(y, jnp.dot(act.astype(BF16), wd_bf[...], preferred_element_type=F32) + bd_ref[...])

        @pl.when(i == n_used - 1)
        def _():
            start_scatter(i, y, ssem)
            wait_scatter(y, ssem)
            wait_scatter(y_prev, ssem_prev)
            wait_gather(xs_next, gsem_next)

    @pl.when(jnp.logical_and(i < n_used, i % 2 == 0))
    def _():
        tile_step(xs_a, y_a, gsem_a, ssem_a, xs_b, y_b, gsem_b, ssem_b)

    @pl.when(jnp.logical_and(i < n_used, i % 2 == 1))
    def _():
        tile_step(xs_b, y_b, gsem_b, ssem_b, xs_a, y_a, gsem_a, ssem_a)


def _expert_mlp(tile_e, n_used, row_slot, h, layer, wgu, bgu, wd, bd):
    n_tok = h.shape[0]
    d = h.shape[1] * h.shape[2]
    row_shape = h.shape[1:]
    tile = EXPERT_TILE
    f = wd.shape[2]
    n_tiles = row_slot.shape[0] // tile
    n_out = n_tok * TOP_K + N_EXPERTS * tile + tile
    spare = n_out - tile + jnp.arange(tile, dtype=I32)
    row_slot = jnp.concatenate([spare, row_slot])
    exp_map = lambda i, te, nu, sl: (layer, te[i], 0, 0)
    grid_spec = pltpu.PrefetchScalarGridSpec(
        num_scalar_prefetch=3,
        grid=(n_tiles,),
        in_specs=[
            pl.BlockSpec(memory_space=pl.ANY),
            pl.BlockSpec((None, None, d, 2 * f), exp_map),
            pl.BlockSpec((None, None, 1, 2 * f), exp_map),
            pl.BlockSpec((None, None, f, d), exp_map),
            pl.BlockSpec((None, None, 1, d), exp_map),
        ],
        out_specs=pl.BlockSpec(memory_space=pl.ANY),
        scratch_shapes=[pltpu.VMEM((tile,) + row_shape, F32)] * 4 + [
            pltpu.VMEM((d, 2 * f), BF16),
            pltpu.VMEM((f, d), BF16),
        ] + [pltpu.SemaphoreType.DMA] * 5,
    )
    return pl.pallas_call(
        functools.partial(_expert_kernel, n_tok=n_tok),
        out_shape=jax.ShapeDtypeStruct((n_out,) + row_shape, F32),
        grid_spec=grid_spec,
        compiler_params=_params("arbitrary"),
        name="moe_expert_mlp",
    )(tile_e, n_used, row_slot, h, wgu, bgu[:, :, None, :], wd, bd[:, :, None, :])


def _combine_kernel(x_ref, gt_ref, gate_ref, *refs):
    y_refs, o_ref = refs[:-1], refs[-1]
    gates = gt_ref[...]
    y = gates[:, 0:1] * _load_row_tiles(y_refs[0])
    for k in range(1, len(y_refs)):
        y = y + gates[:, k:k + 1] * _load_row_tiles(y_refs[k])
    o_ref[...] = x_ref[...] + gate_ref[...] * y


def _combine(x, gates, gate2, y_slots, seq):
    t, d = x.shape
    tc = COMBINE_TILE
    per_seq = seq // tc
    per_k = t // tc
    row_block = (tc,) + y_slots.shape[1:]
    return pl.pallas_call(
        _combine_kernel,
        out_shape=jax.ShapeDtypeStruct((t, d), F32),
        grid=(t // tc,),
        in_specs=[
            pl.BlockSpec((tc, d), lambda i: (i, 0)),
            pl.BlockSpec((tc, LANES), lambda i: (i, 0)),
            pl.BlockSpec((None, 1, d), lambda i: (i // per_seq, 0, 0)),
        ] + [pl.BlockSpec(row_block, functools.partial(lambda i, k: (k * per_k + i, 0, 0), k=k))
             for k in range(TOP_K)],
        out_specs=pl.BlockSpec((tc, d), lambda i: (i, 0)),
        compiler_params=_params("parallel"),
        name="moe_combine_residual",
    )(x, gates, gate2, *([y_slots] * TOP_K))


INVERT_CHUNK = 1024


def _invert_kernel(dest_ref, out_ref, *, clear_steps):
    i = pl.program_id(0)
    group = SUBLANES

    @pl.when(i < clear_steps)
    def _():
        def clear(g, carry):
            for u in range(group):
                out_ref[i * INVERT_CHUNK + g * group + u] = -1
            return carry
        lax.fori_loop(0, INVERT_CHUNK // group, clear, 0)

    @pl.when(i >= clear_steps)
    def _():
        def place(g, carry):
            for u in range(group):
                s = (i - clear_steps) * INVERT_CHUNK + g * group + u
                out_ref[dest_ref[s]] = s
            return carry
        lax.fori_loop(0, INVERT_CHUNK // group, place, 0)


def _invert_rows(dest, n_rows):
    clear_steps = n_rows // INVERT_CHUNK
    return pl.pallas_call(
        functools.partial(_invert_kernel, clear_steps=clear_steps),
        out_shape=jax.ShapeDtypeStruct((n_rows,), I32),
        grid=(clear_steps + dest.shape[0] // INVERT_CHUNK,),
        in_specs=[pl.BlockSpec(memory_space=pltpu.SMEM)],
        out_specs=pl.BlockSpec(memory_space=pltpu.SMEM),
        compiler_params=_params("arbitrary"),
        name="moe_row_slot_map",
    )(dest)


def _moe_layer(x, g, shift, scale, gate2, rw, rb, layer, wgu, bgu, wd, bd, seq):
    t, d = x.shape
    n_exp = rw.shape[1]
    rw_pad = jnp.zeros((d, LANES), F32).at[:, :n_exp].set(rw)
    rb_pad = jnp.full((1, LANES), NEG_BIG, F32).at[0, :n_exp].set(rb)
    h, e_slab, gt_slab, rk_slab, cnt = _router(x, g, shift, scale, rw_pad, rb_pad, seq)

    tile = EXPERT_TILE
    n_tiles = t * TOP_K // tile + n_exp
    counts = cnt[0, :n_exp].astype(I32)
    padded = (counts + tile - 1) // tile * tile
    pad_end = jnp.cumsum(padded)
    pad_start = pad_end - padded
    dest = (pad_start[e_slab[:, :TOP_K]] + rk_slab[:, :TOP_K]).T.reshape(-1)
    n_used = pad_end[-1] // tile
    tile_ids = jnp.arange(n_tiles, dtype=I32)
    first_row = jnp.minimum(tile_ids, n_used - 1) * tile
    tile_e = jnp.sum((pad_end[None, :] <= first_row[:, None]).astype(I32), axis=1)

    n_slots = t * TOP_K
    row_slot = _invert_rows(dest, n_tiles * tile)
    is_pad = row_slot < 0
    row_slot = jnp.where(is_pad, n_slots - 1 + jnp.cumsum(is_pad.astype(I32)), row_slot)

    y_slots = _expert_mlp(tile_e, n_used.reshape(1), row_slot, h, layer, wgu, bgu, wd, bd)
    return _combine(x, gt_slab, gate2, y_slots, seq)


def _final_norm_kernel(x_ref, g_ref, o_ref):
    x = x_ref[...]
    ms = jnp.mean(x * x, axis=-1, keepdims=True)
    o_ref[...] = x * lax.rsqrt(ms + RMS_EPS) * g_ref[...]


def _final_norm(x, g):
    t, d = x.shape
    tm = ROW_TILE
    return pl.pallas_call(
        _final_norm_kernel,
        out_shape=jax.ShapeDtypeStruct((t, d), F32),
        grid=(t // tm,),
        in_specs=[pl.BlockSpec((tm, d), lambda i: (i, 0)), pl.BlockSpec((1, d), lambda i: (0, 0))],
        out_specs=pl.BlockSpec((tm, d), lambda i: (i, 0)),
        compiler_params=_params("parallel"),
        name="final_rmsnorm",
    )(x, g)


def kernel(x, c, norm_mix_g, norm_ffn_g, w_mod, b_mod, conv_w_in, conv_w, conv_w_out, lru_w_in, lru_conv_w, lru_conv_b, lru_gate_w, lru_gate_b, lru_a_param, lru_w_out, fox_w_in, fox_b_f, fox_w_out, router_w, router_b, moe_w_gate_up, moe_b_gate_up, moe_w_down, moe_b_down, final_g):
    bsz, seq, d = x.shape
    depth = w_mod.shape[0]
    t = bsz * seq
    mod = _modulation(c, w_mod, b_mod)
    xt = x.reshape(t, d)
    for i in range(depth):
        shift1, scale1, gate1 = mod[i, :, 0], mod[i, :, 1], mod[i, :, 2]
        shift2, scale2, gate2 = mod[i, :, 3], mod[i, :, 4], mod[i, :, 5]
        g_mix = norm_mix_g[i][None, :]
        kind, j = i % 3, i // 3
        if kind == 0:
            bcv = _norm_matmul(xt, g_mix, shift1, scale1, conv_w_in[j].astype(BF16), 3 * d, F32, seq)[0]
            xt = _conv_mixer(bcv, conv_w[j], conv_w_out[j].astype(BF16), xt, gate1, seq)
        elif kind == 1:
            gx = _norm_matmul(xt, g_mix, shift1, scale1, lru_w_in[j].astype(BF16), lru_w_in.shape[2], F32, seq)[0]
            y = _lru_core(gx, lru_conv_w[j], lru_conv_b[j][None, :], lru_gate_w[j].astype(BF16),
                          lru_gate_b[j][:, None, :], lru_a_param[j][None, :], bsz, seq)
            xt = _proj_residual(y, lru_w_out[j].astype(BF16), xt, gate1, seq)
        else:
            w_in = jnp.zeros((d, 3 * d + LANES), F32).at[:, :3 * d + FOX_HEADS].set(fox_w_in[j]).astype(BF16)
            qkv, f = _norm_matmul(xt, g_mix, shift1, scale1, w_in, 3 * d, BF16, seq)
            b_f = jnp.zeros((1, LANES), F32).at[0, :FOX_HEADS].set(fox_b_f[j])
            cum = _forget_cumsum(f, b_f, bsz, seq)
            o = _fox_attention(qkv, cum, bsz, seq)
            xt = _proj_residual(o, fox_w_out[j].astype(BF16), xt, gate1, seq)
        xt = _moe_layer(xt, norm_ffn_g[i][None, :], shift2, scale2, gate2, router_w[i], router_b[i],
                        i, moe_w_gate_up, moe_b_gate_up, moe_w_down, moe_b_down, seq)
    return _final_norm(xt, final_g[None, :]).reshape(bsz, seq, d)
```

```python
import functools

import jax
import jax.numpy as jnp
from jax import lax
from jax.experimental import pallas as pl
from jax.experimental.pallas import tpu as pltpu

F32 = jnp.float32
BF16 = jnp.bfloat16
I32 = jnp.int32
HIGHEST = lax.Precision.HIGHEST

RMS_EPS = 1e-6
LRU_HEADS = 4
LRU_C = 8.0
FOX_HEADS = 16
N_EXPERTS = 32
TOP_K = 4
SWIGLU_LIMIT = 7.0
SWIGLU_ALPHA = 1.702
NEG_BIG = -1e30
GELU_C = 0.7978845608028654
LOG2E = 1.4426950408889634

LANES = 128
SUBLANES = 8
VMEM_LIMIT = 56 * 1024 * 1024

ROW_TILE = 512
EXPERT_TILE = 256
COMBINE_TILE = 256
ATTN_TILE = 512


def _params(*sem):
    return pltpu.CompilerParams(dimension_semantics=sem, vmem_limit_bytes=VMEM_LIMIT)


def _sigmoid(z):
    return 1.0 / (1.0 + jnp.exp(-z))


def _norm_mod(x, g, shift, scale):
    ms = jnp.mean(x * x, axis=-1, keepdims=True)
    y = x * lax.rsqrt(ms + RMS_EPS) * g
    return y * (1.0 + scale) + shift


def _shifted_rows(u, carry_ref, j, row):
    sh = pltpu.roll(u, j, 0)
    for r in range(j):
        src = SUBLANES - j + r
        sh = jnp.where(row == r, carry_ref[src:src + 1, :], sh)
    return sh


def _mod_kernel(c_ref, w_ref, b_ref, o_ref):
    c = c_ref[...]
    ca = c * _sigmoid(c)
    o_ref[0] = jnp.dot(ca, w_ref[0], precision=HIGHEST, preferred_element_type=F32) + b_ref[0]


def _modulation(c, w_mod, b_mod):
    depth, d, n = w_mod.shape
    bsz = c.shape[0]
    rows = -(-bsz // SUBLANES) * SUBLANES
    c_pad = jnp.zeros((rows, d), F32).at[:bsz].set(c)
    tn = n // 4
    out = pl.pallas_call(
        _mod_kernel,
        out_shape=jax.ShapeDtypeStruct((depth, rows, n), F32),
        grid=(depth, n // tn),
        in_specs=[
            pl.BlockSpec((rows, d), lambda l, j: (0, 0)),
            pl.BlockSpec((1, d, tn), lambda l, j: (l, 0, j)),
            pl.BlockSpec((1, 1, tn), lambda l, j: (l, 0, j)),
        ],
        out_specs=pl.BlockSpec((1, rows, tn), lambda l, j: (l, 0, j)),
        compiler_params=_params("parallel", "parallel"),
        name="adaln_modulation",
    )(c_pad, w_mod, b_mod.reshape(depth, 1, n))
    return out[:, :bsz].reshape(depth, bsz, 6, 1, d)


def _norm_matmul_kernel(x_ref, g_ref, sh_ref, sc_ref, w_ref, *out_refs, n_main):
    h = _norm_mod(x_ref[...], g_ref[...], sh_ref[...], sc_ref[...]).astype(BF16)
    y = jnp.dot(h, w_ref[...], preferred_element_type=F32)
    out_refs[0][...] = y[:, :n_main].astype(out_refs[0].dtype)
    if len(out_refs) > 1:
        out_refs[1][...] = y[:, n_main:]


def _norm_matmul(x, g, shift, scale, w, n_main, out_dtype, seq):
    t, d = x.shape
    n = w.shape[1]
    tm = ROW_TILE
    per_seq = seq // tm
    out_shape = [jax.ShapeDtypeStruct((t, n_main), out_dtype)]
    out_specs = [pl.BlockSpec((tm, n_main), lambda i: (i, 0))]
    if n > n_main:
        out_shape.append(jax.ShapeDtypeStruct((t, n - n_main), F32))
        out_specs.append(pl.BlockSpec((tm, n - n_main), lambda i: (i, 0)))
    return pl.pallas_call(
        functools.partial(_norm_matmul_kernel, n_main=n_main),
        out_shape=out_shape,
        grid=(t // tm,),
        in_specs=[
            pl.BlockSpec((tm, d), lambda i: (i, 0)),
            pl.BlockSpec((1, d), lambda i: (0, 0)),
            pl.BlockSpec((None, 1, d), lambda i: (i // per_seq, 0, 0)),
            pl.BlockSpec((None, 1, d), lambda i: (i // per_seq, 0, 0)),
            pl.BlockSpec((d, n), lambda i: (0, 0)),
        ],
        out_specs=out_specs,
        compiler_params=_params("parallel"),
        name="norm_mod_in_proj",
    )(x, g, shift, scale, w)


def _proj_residual_kernel(z_ref, w_ref, x_ref, gate_ref, o_ref):
    y = jnp.dot(z_ref[...].astype(BF16), w_ref[...], preferred_element_type=F32)
    o_ref[...] = x_ref[...] + gate_ref[...] * y


def _proj_residual(z, w, x, gate, seq):
    t, d = x.shape
    tm = ROW_TILE
    per_seq = seq // tm
    return pl.pallas_call(
        _proj_residual_kernel,
        out_shape=jax.ShapeDtypeStruct((t, d), F32),
        grid=(t // tm,),
        in_specs=[
            pl.BlockSpec((tm, z.shape[1]), lambda i: (i, 0)),
            pl.BlockSpec(w.shape, lambda i: (0, 0)),
            pl.BlockSpec((tm, d), lambda i: (i, 0)),
            pl.BlockSpec((None, 1, d), lambda i: (i // per_seq, 0, 0)),
        ],
        out_specs=pl.BlockSpec((tm, d), lambda i: (i, 0)),
        compiler_params=_params("parallel"),
        name="out_proj_residual",
    )(z, w, x, gate)


def _conv_mixer_kernel(b_ref, c_ref, v_ref, cw_ref, w_ref, x_ref, gate_ref, o_ref, carry_ref, *, per_seq):
    @pl.when(pl.program_id(0) % per_seq == 0)
    def _():
        carry_ref[...] = jnp.zeros_like(carry_ref)

    cv = c_ref[...] * v_ref[...]
    tm = cv.shape[0]
    k_w = cw_ref.shape[0]
    row = lax.broadcasted_iota(I32, cv.shape, 0)
    conv = cw_ref[k_w - 1:k_w, :] * cv
    for j in range(1, k_w):
        conv = conv + cw_ref[k_w - 1 - j:k_w - j, :] * _shifted_rows(cv, carry_ref, j, row)
    carry_ref[...] = cv[tm - SUBLANES:, :]
    z = (b_ref[...] * conv).astype(BF16)
    y = jnp.dot(z, w_ref[...], preferred_element_type=F32)
    o_ref[...] = x_ref[...] + gate_ref[...] * y


def _conv_mixer(bcv, conv_w, w_out, x, gate, seq):
    t, d = x.shape
    tm = ROW_TILE
    per_seq = seq // tm
    return pl.pallas_call(
        functools.partial(_conv_mixer_kernel, per_seq=per_seq),
        out_shape=jax.ShapeDtypeStruct((t, d), F32),
        grid=(t // tm,),
        in_specs=[
            pl.BlockSpec((tm, d), lambda i: (i, 0)),
            pl.BlockSpec((tm, d), lambda i: (i, 1)),
            pl.BlockSpec((tm, d), lambda i: (i, 2)),
            pl.BlockSpec(conv_w.shape, lambda i: (0, 0)),
            pl.BlockSpec(w_out.shape, lambda i: (0, 0)),
            pl.BlockSpec((tm, d), lambda i: (i, 0)),
            pl.BlockSpec((None, 1, d), lambda i: (i // per_seq, 0, 0)),
        ],
        out_specs=pl.BlockSpec((tm, d), lambda i: (i, 0)),
        scratch_shapes=[pltpu.VMEM((SUBLANES, d), F32)],
        compiler_params=_params("arbitrary"),
        name="short_conv_mixer",
    )(bcv, bcv, bcv, conv_w, w_out, x, gate)


def _lru_kernel(xb_ref, gb_ref, cw_ref, cb_ref, gw_ref, gbias_ref, ap_ref, y_ref, xcarry_ref, hcarry_ref):
    @pl.when(pl.program_id(2) == 0)
    def _():
        xcarry_ref[...] = jnp.zeros_like(xcarry_ref)
        hcarry_ref[...] = jnp.zeros_like(hcarry_ref)

    x = xb_ref[...]
    ts, blk = x.shape
    k_w = cw_ref.shape[0]
    row = lax.broadcasted_iota(I32, x.shape, 0)
    conv = cw_ref[k_w - 1:k_w, :] * x
    for j in range(1, k_w):
        conv = conv + cw_ref[k_w - 1 - j:k_w - j, :] * _shifted_rows(x, xcarry_ref, j, row)
    xcarry_ref[...] = x[ts - SUBLANES:, :]
    xc = conv + cb_ref[...]

    g = jnp.dot(xc.astype(BF16), gw_ref[...], preferred_element_type=F32) + gbias_ref[...]
    r_gate = _sigmoid(g[:, :blk])
    i_gate = _sigmoid(g[:, blk:])
    ap = ap_ref[...]
    softplus = jnp.maximum(ap, 0.0) + jnp.log(1.0 + jnp.exp(-jnp.abs(ap)))
    a = jnp.exp((-LRU_C) * r_gate * softplus)
    u = jnp.sqrt(1.0 - a * a) * (i_gate * xc)

    a_cum, h_loc = a, u
    step = 1
    while step < ts:
        a_prev = jnp.where(row < step, 1.0, pltpu.roll(a_cum, step, 0))
        h_prev = jnp.where(row < step, 0.0, pltpu.roll(h_loc, step, 0))
        h_loc = h_loc + a_cum * h_prev
        a_cum = a_cum * a_prev
        step *= 2
    hs = h_loc + a_cum * hcarry_ref[...]
    hcarry_ref[...] = hs[ts - 1:ts, :]

    gb = gb_ref[...]
    gelu = 0.5 * gb * (1.0 + jnp.tanh(GELU_C * (gb + 0.044715 * (gb * gb * gb))))
    y_ref[...] = (gelu * hs).astype(y_ref.dtype)


def _lru_core(gx, conv_w, conv_b, gate_w, gate_b, a_param, bsz, seq):
    t = gx.shape[0]
    width = gx.shape[1] // 2
    blk = width // LRU_HEADS
    ts = ROW_TILE
    per_seq = seq // ts
    return pl.pallas_call(
        _lru_kernel,
        out_shape=jax.ShapeDtypeStruct((t, width), BF16),
        grid=(bsz, LRU_HEADS, per_seq),
        in_specs=[
            pl.BlockSpec((ts, blk), lambda b, h, s: (b * per_seq + s, LRU_HEADS + h)),
            pl.BlockSpec((ts, blk), lambda b, h, s: (b * per_seq + s, h)),
            pl.BlockSpec((conv_w.shape[0], blk), lambda b, h, s: (0, h)),
            pl.BlockSpec((1, blk), lambda b, h, s: (0, h)),
            pl.BlockSpec((None, blk, 2 * blk), lambda b, h, s: (h, 0, 0)),
            pl.BlockSpec((None, 1, 2 * blk), lambda b, h, s: (h, 0, 0)),
            pl.BlockSpec((1, blk), lambda b, h, s: (0, h)),
        ],
        out_specs=pl.BlockSpec((ts, blk), lambda b, h, s: (b * per_seq + s, h)),
        scratch_shapes=[pltpu.VMEM((SUBLANES, blk), F32), pltpu.VMEM((1, blk), F32)],
        compiler_params=_params("parallel", "parallel", "arbitrary"),
        name="rglru_scan",
    )(gx, gx, conv_w, conv_b, gate_w, gate_b, a_param)


def _forget_cumsum_kernel(f_ref, bf_ref, o_ref):
    x = f_ref[...] + bf_ref[...]
    c = jnp.minimum(x, 0.0) - jnp.log(1.0 + jnp.exp(-jnp.abs(x)))
    seq = c.shape[0]
    row = lax.broadcasted_iota(I32, c.shape, 0)
    step = 1
    while step < seq:
        c = c + jnp.where(row < step, 0.0, pltpu.roll(c, step, 0))
        step *= 2
    o_ref[...] = c.T[:FOX_HEADS, :]


def _forget_cumsum(f, b_f, bsz, seq):
    return pl.pallas_call(
        _forget_cumsum_kernel,
        out_shape=jax.ShapeDtypeStruct((bsz, FOX_HEADS, seq), F32),
        grid=(bsz,),
        in_specs=[
            pl.BlockSpec((seq, LANES), lambda b: (b, 0)),
            pl.BlockSpec((1, LANES), lambda b: (0, 0)),
        ],
        out_specs=pl.BlockSpec((None, FOX_HEADS, seq), lambda b: (b, 0, 0)),
        compiler_params=_params("parallel"),
        name="forget_gate_cumsum",
    )(f, b_f)


def _fox_attn_kernel(q_ref, k_ref, v_ref, ck_ref, o_ref, m_ref, acc_ref, *, scale):
    qi = pl.program_id(2)
    tq = q_ref.shape[0]
    tk = tq
    half = q_ref.shape[1] // 2
    first_head = lax.broadcasted_iota(I32, (1, q_ref.shape[1]), 1) < half

    qs = (q_ref[...].astype(F32) * (scale * LOG2E)).astype(BF16)
    zero = jnp.zeros_like(qs)
    q_heads = (jnp.where(first_head, qs, zero), jnp.where(first_head, zero, qs))
    m_ref[...] = jnp.full_like(m_ref, NEG_BIG)
    acc_ref[...] = jnp.zeros_like(acc_ref)

    def update(j, diagonal):
        start = pl.multiple_of(j * tk, tk)
        k = k_ref[pl.ds(start, tk), :]
        v = v_ref[pl.ds(start, tk), :]
        ck = ck_ref[:, pl.ds(start, tk)] * LOG2E
        ones = jnp.ones_like(v)
        v_heads = (jnp.where(first_head, v, ones), jnp.where(first_head, ones, v))
        for hh in range(2):
            s = lax.dot_general(q_heads[hh], k, (((1,), (1,)), ((), ())), preferred_element_type=F32)
            s = s - ck[hh:hh + 1, :]
            if diagonal:
                row = lax.broadcasted_iota(I32, s.shape, 0)
                col = lax.broadcasted_iota(I32, s.shape, 1)
                s = jnp.where(col <= row, s, NEG_BIG)
            m_prev = m_ref[hh]
            m_new = jnp.maximum(m_prev, jnp.max(s, axis=-1, keepdims=True))
            alpha = jnp.exp2(m_prev - m_new)
            p = jnp.exp2(s - jnp.concatenate([m_new] * (tk // LANES), axis=1))
            m_ref[hh] = m_new
            acc_ref[hh] = alpha * acc_ref[hh] + jnp.dot(p.astype(BF16), v_heads[hh], preferred_element_type=F32)

    def body(j, carry):
        update(j, False)
        return carry

    lax.fori_loop(0, qi, body, 0)
    update(qi, True)
    a0 = acc_ref[0]
    a1 = acc_ref[1]
    o0 = a0 / pltpu.roll(a0, half, 1)
    o1 = a1 / pltpu.roll(a1, half, 1)
    o_ref[...] = jnp.where(first_head, o0, o1).astype(o_ref.dtype)


def _fox_attention(qkv, cum, bsz, seq):
    t = qkv.shape[0]
    d = qkv.shape[1] // 3
    pairs = d // LANES
    tq = ATTN_TILE
    per_seq = seq // tq
    scale = (d // FOX_HEADS) ** -0.5
    ck = cum.reshape(bsz, pairs, 2, seq)
    return pl.pallas_call(
        functools.partial(_fox_attn_kernel, scale=scale),
        out_shape=jax.ShapeDtypeStruct((t, d), BF16),
        grid=(bsz, pairs, per_seq),
        in_specs=[
            pl.BlockSpec((tq, LANES), lambda b, hp, qi: (b * per_seq + qi, hp)),
            pl.BlockSpec((seq, LANES), lambda b, hp, qi: (b, pairs + hp)),
            pl.BlockSpec((seq, LANES), lambda b, hp, qi: (b, 2 * pairs + hp)),
            pl.BlockSpec((None, None, 2, seq), lambda b, hp, qi: (b, hp, 0, 0)),
        ],
        out_specs=pl.BlockSpec((tq, LANES), lambda b, hp, qi: (b * per_seq + qi, hp)),
        scratch_shapes=[
            pltpu.VMEM((2, tq, LANES), F32),
            pltpu.VMEM((2, tq, LANES), F32),
        ],
        compiler_params=_params("parallel", "parallel", "arbitrary"),
        name="forgetting_attention",
    )(qkv, qkv, qkv, ck)


def _router_kernel(x_ref, g_ref, sh_ref, sc_ref, rw_ref, rb_ref,
                   h_ref, e_ref, gt_ref, rk_ref, cnt_ref, carry_ref):
    @pl.when(pl.program_id(0) == 0)
    def _():
        carry_ref[...] = jnp.zeros_like(carry_ref)

    h = _norm_mod(x_ref[...], g_ref[...], sh_ref[...], sc_ref[...])
    h_ref[...] = h
    logits = jnp.dot(h, rw_ref[...], precision=HIGHEST, preferred_element_type=F32) + rb_ref[...]
    tm = logits.shape[0]
    lane = lax.broadcasted_iota(I32, logits.shape, 1).astype(F32)

    work = logits
    picked = jnp.zeros(logits.shape, F32)
    vals, idxs = [], []
    for _ in range(TOP_K):
        mx = jnp.max(work, axis=-1, keepdims=True)
        idx = jnp.min(jnp.where(work == mx, lane, float(LANES)), axis=-1, keepdims=True)
        sel = lane == idx
        vals.append(mx)
        idxs.append(idx)
        picked = jnp.where(sel, 1.0, picked)
        work = jnp.where(sel, -3.0e38, work)

    ex = [jnp.exp(v - vals[0]) for v in vals]
    den = ex[0] + ex[1] + ex[2] + ex[3]

    r_i = lax.broadcasted_iota(I32, (tm, tm), 0)
    c_i = lax.broadcasted_iota(I32, (tm, tm), 1)
    earlier = jnp.where(c_i < r_i, 1.0, 0.0).astype(BF16)
    before = jnp.dot(earlier, picked.astype(BF16), preferred_element_type=F32) + carry_ref[...]
    carry_ref[...] = carry_ref[...] + jnp.sum(picked, axis=0, keepdims=True)
    cnt_ref[...] = carry_ref[...]

    e_out = jnp.zeros(logits.shape, F32)
    g_out = jnp.zeros(logits.shape, F32)
    r_out = jnp.zeros(logits.shape, F32)
    for k in range(TOP_K):
        rank = jnp.sum(jnp.where(lane == idxs[k], before, 0.0), axis=-1, keepdims=True)
        at_k = lane == float(k)
        e_out = jnp.where(at_k, idxs[k], e_out)
        g_out = jnp.where(at_k, ex[k] / den, g_out)
        r_out = jnp.where(at_k, rank, r_out)
    e_ref[...] = e_out.astype(I32)
    gt_ref[...] = g_out
    rk_ref[...] = r_out.astype(I32)


def _router(x, g, shift, scale, rw, rb, seq):
    t, d = x.shape
    tm = ROW_TILE
    per_seq = seq // tm
    slab = lambda dt: jax.ShapeDtypeStruct((t, LANES), dt)
    slab_spec = pl.BlockSpec((tm, LANES), lambda i: (i, 0))
    return pl.pallas_call(
        _router_kernel,
        out_shape=[jax.ShapeDtypeStruct((t, d), F32), slab(I32), slab(F32), slab(I32),
                   jax.ShapeDtypeStruct((1, LANES), F32)],
        grid=(t // tm,),
        in_specs=[
            pl.BlockSpec((tm, d), lambda i: (i, 0)),
            pl.BlockSpec((1, d), lambda i: (0, 0)),
            pl.BlockSpec((None, 1, d), lambda i: (i // per_seq, 0, 0)),
            pl.BlockSpec((None, 1, d), lambda i: (i // per_seq, 0, 0)),
            pl.BlockSpec((d, LANES), lambda i: (0, 0)),
            pl.BlockSpec((1, LANES), lambda i: (0, 0)),
        ],
        out_specs=[pl.BlockSpec((tm, d), lambda i: (i, 0)), slab_spec, slab_spec, slab_spec,
                   pl.BlockSpec((1, LANES), lambda i: (0, 0))],
        scratch_shapes=[pltpu.VMEM((1, LANES), F32)],
        compiler_params=_params("arbitrary"),
        name="moe_router",
    )(x, g, shift, scale, rw, rb)


def _expert_kernel(te_ref, nu_ref, tok_ref, h_hbm, wgu_ref, bgu_ref, wd_ref, bd_ref, y_ref,
                   xs_a, xs_b, wgu_bf, wd_bf, gsem_a, gsem_b):
    i = pl.program_id(0)
    n_steps = pl.num_programs(0)
    n_used = nu_ref[0]
    tile = xs_a.shape[0]

    def start_gather(tile_idx, xs, sem):
        for j in range(tile):
            tok = tok_ref[tile_idx * tile + j]
            pltpu.make_async_copy(h_hbm.at[pl.ds(tok, 1)], xs.at[pl.ds(j, 1)], sem).start(priority=j % 2)

    def wait_gather(xs, sem):
        pltpu.make_async_copy(h_hbm.at[pl.ds(0, tile)], xs, sem).wait()

    @pl.when(i == 0)
    def _():
        start_gather(0, xs_a, gsem_a)

    @pl.when(i >= n_used)
    def _():
        y_ref[...] = jnp.zeros_like(y_ref)

    @pl.when(jnp.logical_or(i == 0, te_ref[i] != te_ref[jnp.maximum(i - 1, 0)]))
    def _():
        wgu_bf[...] = wgu_ref[...].astype(BF16)
        wd_bf[...] = wd_ref[...].astype(BF16)

    def tile_step(xs, gsem, xs_next, gsem_next):
        wait_gather(xs, gsem)
        start_gather(jnp.minimum(i + 1, n_steps - 1), xs_next, gsem_next)
        f = wd_ref.shape[0]
        gu = jnp.dot(xs[...].astype(BF16), wgu_bf[...], preferred_element_type=F32) + bgu_ref[...]
        gate = jnp.minimum(gu[:, :f], SWIGLU_LIMIT)
        up = jnp.clip(gu[:, f:], -SWIGLU_LIMIT, SWIGLU_LIMIT)
        act = gate * _sigmoid(SWIGLU_ALPHA * gate) * (up + 1.0)
        y_ref[...] = jnp.dot(act.astype(BF16), wd_bf[...], preferred_element_type=F32) + bd_ref[...]

        @pl.when(i == n_used - 1)
        def _():
            wait_gather(xs_next, gsem_next)

    @pl.when(jnp.logical_and(i < n_used, i % 2 == 0))
    def _():
        tile_step(xs_a, gsem_a, xs_b, gsem_b)

    @pl.when(jnp.logical_and(i < n_used, i % 2 == 1))
    def _():
        tile_step(xs_b, gsem_b, xs_a, gsem_a)


def _expert_mlp(tile_e, n_used, row_tok, h, layer, wgu, bgu, wd, bd):
    d = h.shape[1]
    tile = EXPERT_TILE
    f = wd.shape[2]
    n_rows = row_tok.shape[0]
    exp_map = lambda i, te, nu, rt: (layer, te[i], 0, 0)
    grid_spec = pltpu.PrefetchScalarGridSpec(
        num_scalar_prefetch=3,
        grid=(n_rows // tile,),
        in_specs=[
            pl.BlockSpec(memory_space=pl.ANY),
            pl.BlockSpec((None, None, d, 2 * f), exp_map),
            pl.BlockSpec((None, None, 1, 2 * f), exp_map),
            pl.BlockSpec((None, None, f, d), exp_map),
            pl.BlockSpec((None, None, 1, d), exp_map),
        ],
        out_specs=pl.BlockSpec((tile, d), lambda i, te, nu, rt: (i, 0)),
        scratch_shapes=[
            pltpu.VMEM((tile, d), F32),
            pltpu.VMEM((tile, d), F32),
            pltpu.VMEM((d, 2 * f), BF16),
            pltpu.VMEM((f, d), BF16),
            pltpu.SemaphoreType.DMA,
            pltpu.SemaphoreType.DMA,
        ],
    )
    return pl.pallas_call(
        _expert_kernel,
        out_shape=jax.ShapeDtypeStruct((n_rows, d), F32),
        grid_spec=grid_spec,
        compiler_params=_params("arbitrary"),
        name="moe_expert_mlp",
    )(tile_e, n_used, row_tok, h, wgu, bgu[:, :, None, :], wd, bd[:, :, None, :])


def _combine_kernel(dest_ref, x_ref, gt_ref, gate_ref, yb_hbm, o_ref, buf_ref, sem, *, tc):
    i = pl.program_id(0)
    n = pl.num_programs(0)

    def issue(step, slot):
        def body(j, carry):
            tok = step * tc + j
            for k in range(TOP_K):
                pltpu.make_async_copy(yb_hbm.at[pl.ds(dest_ref[tok * TOP_K + k], 1)],
                                      buf_ref.at[slot, k, pl.ds(j, 1)], sem.at[slot]).start(priority=k % 2)
            return carry
        lax.fori_loop(0, tc, body, 0)

    @pl.when(i == 0)
    def _():
        issue(0, 0)

    @pl.when(i + 1 < n)
    def _():
        issue(i + 1, (i + 1) % 2)

    slot = i % 2

    for k in range(TOP_K):
        pltpu.make_async_copy(yb_hbm.at[pl.ds(0, tc)], buf_ref.at[slot, k], sem.at[slot]).wait()

    gates = gt_ref[...]
    y = gates[:, 0:1] * buf_ref[slot, 0]
    for k in range(1, TOP_K):
        y = y + gates[:, k:k + 1] * buf_ref[slot, k]
    o_ref[...] = x_ref[...] + gate_ref[...] * y


def _combine(dest, x, gates, gate2, yb, seq):
    t, d = x.shape
    tc = COMBINE_TILE
    per_seq = seq // tc
    grid_spec = pltpu.PrefetchScalarGridSpec(
        num_scalar_prefetch=1,
        grid=(t // tc,),
        in_specs=[
            pl.BlockSpec((tc, d), lambda i, dr: (i, 0)),
            pl.BlockSpec((tc, LANES), lambda i, dr: (i, 0)),
            pl.BlockSpec((None, 1, d), lambda i, dr: (i // per_seq, 0, 0)),
            pl.BlockSpec(memory_space=pl.ANY),
        ],
        out_specs=pl.BlockSpec((tc, d), lambda i, dr: (i, 0)),
        scratch_shapes=[pltpu.VMEM((2, TOP_K, tc, d), F32), pltpu.SemaphoreType.DMA((2,))],
    )
    return pl.pallas_call(
        functools.partial(_combine_kernel, tc=tc),
        out_shape=jax.ShapeDtypeStruct((t, d), F32),
        grid_spec=grid_spec,
        compiler_params=_params("arbitrary"),
        name="moe_combine_residual",
    )(dest, x, gates, gate2, yb)


INVERT_CHUNK = 4096


def _invert_kernel(dest_ref, out_ref, *, clear_steps):
    i = pl.program_id(0)
    group = SUBLANES

    @pl.when(i < clear_steps)
    def _():
        def clear(g, carry):
            for u in range(group):
                out_ref[i * INVERT_CHUNK + g * group + u] = 0
            return carry
        lax.fori_loop(0, INVERT_CHUNK // group, clear, 0)

    @pl.when(i >= clear_steps)
    def _():
        def place(g, carry):
            first = (i - clear_steps) * INVERT_CHUNK + g * group
            for u in range(group):
                out_ref[dest_ref[first + u]] = first // TOP_K + u // TOP_K
            return carry
        lax.fori_loop(0, INVERT_CHUNK // group, place, 0)


def _invert_rows(dest, n_rows):
    clear_steps = n_rows // INVERT_CHUNK
    return pl.pallas_call(
        functools.partial(_invert_kernel, clear_steps=clear_steps),
        out_shape=jax.ShapeDtypeStruct((n_rows,), I32),
        grid=(clear_steps + dest.shape[0] // INVERT_CHUNK,),
        in_specs=[pl.BlockSpec(memory_space=pltpu.SMEM)],
        out_specs=pl.BlockSpec(memory_space=pltpu.SMEM),
        compiler_params=_params("arbitrary"),
        name="moe_row_token_map",
    )(dest)


def _moe_layer(x, g, shift, scale, gate2, rw, rb, layer, wgu, bgu, wd, bd, seq):
    t, d = x.shape
    n_exp = rw.shape[1]
    rw_pad = jnp.zeros((d, LANES), F32).at[:, :n_exp].set(rw)
    rb_pad = jnp.full((1, LANES), NEG_BIG, F32).at[0, :n_exp].set(rb)
    h, e_slab, gt_slab, rk_slab, cnt = _router(x, g, shift, scale, rw_pad, rb_pad, seq)

    tile = EXPERT_TILE
    n_tiles = t * TOP_K // tile + n_exp
    counts = cnt[0, :n_exp].astype(I32)
    padded = (counts + tile - 1) // tile * tile
    pad_end = jnp.cumsum(padded)
    pad_start = pad_end - padded
    dest = (pad_start[e_slab[:, :TOP_K]] + rk_slab[:, :TOP_K]).reshape(-1)
    n_used = pad_end[-1] // tile
    tile_ids = jnp.arange(n_tiles, dtype=I32)
    first_row = jnp.minimum(tile_ids, n_used - 1) * tile
    tile_e = jnp.sum((pad_end[None, :] <= first_row[:, None]).astype(I32), axis=1)

    row_tok = _invert_rows(dest, n_tiles * tile)
    yb = _expert_mlp(tile_e, n_used.reshape(1), row_tok, h, layer, wgu, bgu, wd, bd)
    return _combine(dest, x, gt_slab, gate2, yb, seq)


def _final_norm_kernel(x_ref, g_ref, o_ref):
    x = x_ref[...]
    ms = jnp.mean(x * x, axis=-1, keepdims=True)
    o_ref[...] = x * lax.rsqrt(ms + RMS_EPS) * g_ref[...]


def _final_norm(x, g):
    t, d = x.shape
    tm = ROW_TILE
    return pl.pallas_call(
        _final_norm_kernel,
        out_shape=jax.ShapeDtypeStruct((t, d), F32),
        grid=(t // tm,),
        in_specs=[pl.BlockSpec((tm, d), lambda i: (i, 0)), pl.BlockSpec((1, d), lambda i: (0, 0))],
        out_specs=pl.BlockSpec((tm, d), lambda i: (i, 0)),
        compiler_params=_params("parallel"),
        name="final_rmsnorm",
    )(x, g)


def kernel(x, c, norm_mix_g, norm_ffn_g, w_mod, b_mod, conv_w_in, conv_w, conv_w_out, lru_w_in, lru_conv_w, lru_conv_b, lru_gate_w, lru_gate_b, lru_a_param, lru_w_out, fox_w_in, fox_b_f, fox_w_out, router_w, router_b, moe_w_gate_up, moe_b_gate_up, moe_w_down, moe_b_down, final_g):
    bsz, seq, d = x.shape
    depth = w_mod.shape[0]
    t = bsz * seq
    mod = _modulation(c, w_mod, b_mod)
    xt = x.reshape(t, d)
    for i in range(depth):
        shift1, scale1, gate1 = mod[i, :, 0], mod[i, :, 1], mod[i, :, 2]
        shift2, scale2, gate2 = mod[i, :, 3], mod[i, :, 4], mod[i, :, 5]
        g_mix = norm_mix_g[i][None, :]
        kind, j = i % 3, i // 3
        if kind == 0:
            bcv = _norm_matmul(xt, g_mix, shift1, scale1, conv_w_in[j].astype(BF16), 3 * d, F32, seq)[0]
            xt = _conv_mixer(bcv, conv_w[j], conv_w_out[j].astype(BF16), xt, gate1, seq)
        elif kind == 1:
            gx = _norm_matmul(xt, g_mix, shift1, scale1, lru_w_in[j].astype(BF16), lru_w_in.shape[2], F32, seq)[0]
            y = _lru_core(gx, lru_conv_w[j], lru_conv_b[j][None, :], lru_gate_w[j].astype(BF16),
                          lru_gate_b[j][:, None, :], lru_a_param[j][None, :], bsz, seq)
            xt = _proj_residual(y, lru_w_out[j].astype(BF16), xt, gate1, seq)
        else:
            w_in = jnp.zeros((d, 3 * d + LANES), F32).at[:, :3 * d + FOX_HEADS].set(fox_w_in[j]).astype(BF16)
            qkv, f = _norm_matmul(xt, g_mix, shift1, scale1, w_in, 3 * d, BF16, seq)
            b_f = jnp.zeros((1, LANES), F32).at[0, :FOX_HEADS].set(fox_b_f[j])
            cum = _forget_cumsum(f, b_f, bsz, seq)
            o = _fox_attention(qkv, cum, bsz, seq)
            xt = _proj_residual(o, fox_w_out[j].astype(BF16), xt, gate1, seq)
        xt = _moe_layer(xt, norm_ffn_g[i][None, :], shift2, scale2, gate2, router_w[i], router_b[i],
                        i, moe_w_gate_up, moe_b_gate_up, moe_w_down, moe_b_down, seq)
    return _final_norm(xt, final_g[None, :]).reshape(bsz, seq, d)
```

```python
import functools

import jax
import jax.numpy as jnp
from jax import lax
from jax.experimental import pallas as pl
from jax.experimental.pallas import tpu as pltpu

F32 = jnp.float32
BF16 = jnp.bfloat16
I32 = jnp.int32
HIGHEST = lax.Precision.HIGHEST

RMS_EPS = 1e-6
LRU_HEADS = 4
LRU_C = 8.0
FOX_HEADS = 16
N_EXPERTS = 32
TOP_K = 4
SWIGLU_LIMIT = 7.0
SWIGLU_ALPHA = 1.702
NEG_BIG = -1e30
GELU_C = 0.7978845608028654
LOG2E = 1.4426950408889634

LANES = 128
SUBLANES = 8
VMEM_LIMIT = 56 * 1024 * 1024

ROW_TILE = 512
EXPERT_TILE = 256
DISPATCH_TILE = 256
COMBINE_TILE = 128
ATTN_TILE = 512


def _params(*sem):
    return pltpu.CompilerParams(dimension_semantics=sem, vmem_limit_bytes=VMEM_LIMIT)


def _sigmoid(z):
    return 1.0 / (1.0 + jnp.exp(-z))


def _norm_mod(x, g, shift, scale):
    ms = jnp.mean(x * x, axis=-1, keepdims=True)
    y = x * lax.rsqrt(ms + RMS_EPS) * g
    return y * (1.0 + scale) + shift


def _shifted_rows(u, carry_ref, j, row):
    sh = pltpu.roll(u, j, 0)
    for r in range(j):
        src = SUBLANES - j + r
        sh = jnp.where(row == r, carry_ref[src:src + 1, :], sh)
    return sh


def _mod_kernel(c_ref, w_ref, b_ref, o_ref):
    c = c_ref[...]
    ca = c * _sigmoid(c)
    o_ref[0] = jnp.dot(ca, w_ref[0], precision=HIGHEST, preferred_element_type=F32) + b_ref[0]


def _modulation(c, w_mod, b_mod):
    depth, d, n = w_mod.shape
    bsz = c.shape[0]
    rows = -(-bsz // SUBLANES) * SUBLANES
    c_pad = jnp.zeros((rows, d), F32).at[:bsz].set(c)
    tn = n // 4
    out = pl.pallas_call(
        _mod_kernel,
        out_shape=jax.ShapeDtypeStruct((depth, rows, n), F32),
        grid=(depth, n // tn),
        in_specs=[
            pl.BlockSpec((rows, d), lambda l, j: (0, 0)),
            pl.BlockSpec((1, d, tn), lambda l, j: (l, 0, j)),
            pl.BlockSpec((1, 1, tn), lambda l, j: (l, 0, j)),
        ],
        out_specs=pl.BlockSpec((1, rows, tn), lambda l, j: (l, 0, j)),
        compiler_params=_params("parallel", "parallel"),
        name="adaln_modulation",
    )(c_pad, w_mod, b_mod.reshape(depth, 1, n))
    return out[:, :bsz].reshape(depth, bsz, 6, 1, d)


def _norm_matmul_kernel(x_ref, g_ref, sh_ref, sc_ref, w_ref, *out_refs, n_main):
    h = _norm_mod(x_ref[...], g_ref[...], sh_ref[...], sc_ref[...]).astype(BF16)
    y = jnp.dot(h, w_ref[...], preferred_element_type=F32)
    out_refs[0][...] = y[:, :n_main].astype(out_refs[0].dtype)
    if len(out_refs) > 1:
        out_refs[1][...] = y[:, n_main:]


def _norm_matmul(x, g, shift, scale, w, n_main, out_dtype, seq):
    t, d = x.shape
    n = w.shape[1]
    tm = ROW_TILE
    per_seq = seq // tm
    out_shape = [jax.ShapeDtypeStruct((t, n_main), out_dtype)]
    out_specs = [pl.BlockSpec((tm, n_main), lambda i: (i, 0))]
    if n > n_main:
        out_shape.append(jax.ShapeDtypeStruct((t, n - n_main), F32))
        out_specs.append(pl.BlockSpec((tm, n - n_main), lambda i: (i, 0)))
    return pl.pallas_call(
        functools.partial(_norm_matmul_kernel, n_main=n_main),
        out_shape=out_shape,
        grid=(t // tm,),
        in_specs=[
            pl.BlockSpec((tm, d), lambda i: (i, 0)),
            pl.BlockSpec((1, d), lambda i: (0, 0)),
            pl.BlockSpec((None, 1, d), lambda i: (i // per_seq, 0, 0)),
            pl.BlockSpec((None, 1, d), lambda i: (i // per_seq, 0, 0)),
            pl.BlockSpec((d, n), lambda i: (0, 0)),
        ],
        out_specs=out_specs,
        compiler_params=_params("parallel"),
        name="norm_mod_in_proj",
    )(x, g, shift, scale, w)


def _proj_residual_kernel(z_ref, w_ref, x_ref, gate_ref, o_ref):
    y = jnp.dot(z_ref[...].astype(BF16), w_ref[...], preferred_element_type=F32)
    o_ref[...] = x_ref[...] + gate_ref[...] * y


def _proj_residual(z, w, x, gate, seq):
    t, d = x.shape
    tm = ROW_TILE
    per_seq = seq // tm
    return pl.pallas_call(
        _proj_residual_kernel,
        out_shape=jax.ShapeDtypeStruct((t, d), F32),
        grid=(t // tm,),
        in_specs=[
            pl.BlockSpec((tm, z.shape[1]), lambda i: (i, 0)),
            pl.BlockSpec(w.shape, lambda i: (0, 0)),
            pl.BlockSpec((tm, d), lambda i: (i, 0)),
            pl.BlockSpec((None, 1, d), lambda i: (i // per_seq, 0, 0)),
        ],
        out_specs=pl.BlockSpec((tm, d), lambda i: (i, 0)),
        compiler_params=_params("parallel"),
        name="out_proj_residual",
    )(z, w, x, gate)


def _conv_mixer_kernel(b_ref, c_ref, v_ref, cw_ref, w_ref, x_ref, gate_ref, o_ref, carry_ref, *, per_seq):
    @pl.when(pl.program_id(0) % per_seq == 0)
    def _():
        carry_ref[...] = jnp.zeros_like(carry_ref)

    cv = c_ref[...] * v_ref[...]
    tm = cv.shape[0]
    k_w = cw_ref.shape[0]
    row = lax.broadcasted_iota(I32, cv.shape, 0)
    conv = cw_ref[k_w - 1:k_w, :] * cv
    for j in range(1, k_w):
        conv = conv + cw_ref[k_w - 1 - j:k_w - j, :] * _shifted_rows(cv, carry_ref, j, row)
    carry_ref[...] = cv[tm - SUBLANES:, :]
    z = (b_ref[...] * conv).astype(BF16)
    y = jnp.dot(z, w_ref[...], preferred_element_type=F32)
    o_ref[...] = x_ref[...] + gate_ref[...] * y


def _conv_mixer(bcv, conv_w, w_out, x, gate, seq):
    t, d = x.shape
    tm = ROW_TILE
    per_seq = seq // tm
    return pl.pallas_call(
        functools.partial(_conv_mixer_kernel, per_seq=per_seq),
        out_shape=jax.ShapeDtypeStruct((t, d), F32),
        grid=(t // tm,),
        in_specs=[
            pl.BlockSpec((tm, d), lambda i: (i, 0)),
            pl.BlockSpec((tm, d), lambda i: (i, 1)),
            pl.BlockSpec((tm, d), lambda i: (i, 2)),
            pl.BlockSpec(conv_w.shape, lambda i: (0, 0)),
            pl.BlockSpec(w_out.shape, lambda i: (0, 0)),
            pl.BlockSpec((tm, d), lambda i: (i, 0)),
            pl.BlockSpec((None, 1, d), lambda i: (i // per_seq, 0, 0)),
        ],
        out_specs=pl.BlockSpec((tm, d), lambda i: (i, 0)),
        scratch_shapes=[pltpu.VMEM((SUBLANES, d), F32)],
        compiler_params=_params("arbitrary"),
        name="short_conv_mixer",
    )(bcv, bcv, bcv, conv_w, w_out, x, gate)


def _lru_kernel(xb_ref, gb_ref, cw_ref, cb_ref, gw_ref, gbias_ref, ap_ref, y_ref, xcarry_ref, hcarry_ref):
    @pl.when(pl.program_id(2) == 0)
    def _():
        xcarry_ref[...] = jnp.zeros_like(xcarry_ref)
        hcarry_ref[...] = jnp.zeros_like(hcarry_ref)

    x = xb_ref[...]
    ts, blk = x.shape
    k_w = cw_ref.shape[0]
    row = lax.broadcasted_iota(I32, x.shape, 0)
    conv = cw_ref[k_w - 1:k_w, :] * x
    for j in range(1, k_w):
        conv = conv + cw_ref[k_w - 1 - j:k_w - j, :] * _shifted_rows(x, xcarry_ref, j, row)
    xcarry_ref[...] = x[ts - SUBLANES:, :]
    xc = conv + cb_ref[...]

    g = jnp.dot(xc.astype(BF16), gw_ref[...], preferred_element_type=F32) + gbias_ref[...]
    r_gate = _sigmoid(g[:, :blk])
    i_gate = _sigmoid(g[:, blk:])
    ap = ap_ref[...]
    softplus = jnp.maximum(ap, 0.0) + jnp.log(1.0 + jnp.exp(-jnp.abs(ap)))
    a = jnp.exp((-LRU_C) * r_gate * softplus)
    u = jnp.sqrt(1.0 - a * a) * (i_gate * xc)

    a_cum, h_loc = a, u
    in_group = jnp.bitwise_and(row, SUBLANES - 1)
    step = 1
    while step < SUBLANES:
        a_prev = jnp.where(in_group < step, 1.0, pltpu.roll(a_cum, step, 0))
        h_prev = jnp.where(in_group < step, 0.0, pltpu.roll(h_loc, step, 0))
        h_loc = h_loc + a_cum * h_prev
        a_cum = a_cum * a_prev
        step *= 2
    carry = hcarry_ref[...]
    groups = []
    for g in range(ts // SUBLANES):
        lo = g * SUBLANES
        h_group = h_loc[lo:lo + SUBLANES, :] + a_cum[lo:lo + SUBLANES, :] * carry
        carry = h_group[SUBLANES - 1:SUBLANES, :]
        groups.append(h_group)
    hs = jnp.concatenate(groups, axis=0)
    hcarry_ref[...] = carry

    gb = gb_ref[...]
    gelu = 0.5 * gb * (1.0 + jnp.tanh(GELU_C * (gb + 0.044715 * (gb * gb * gb))))
    y_ref[...] = (gelu * hs).astype(y_ref.dtype)


def _lru_core(gx, conv_w, conv_b, gate_w, gate_b, a_param, bsz, seq):
    t = gx.shape[0]
    width = gx.shape[1] // 2
    blk = width // LRU_HEADS
    ts = ROW_TILE
    per_seq = seq // ts
    return pl.pallas_call(
        _lru_kernel,
        out_shape=jax.ShapeDtypeStruct((t, width), BF16),
        grid=(bsz, LRU_HEADS, per_seq),
        in_specs=[
            pl.BlockSpec((ts, blk), lambda b, h, s: (b * per_seq + s, LRU_HEADS + h)),
            pl.BlockSpec((ts, blk), lambda b, h, s: (b * per_seq + s, h)),
            pl.BlockSpec((conv_w.shape[0], blk), lambda b, h, s: (0, h)),
            pl.BlockSpec((1, blk), lambda b, h, s: (0, h)),
            pl.BlockSpec((None, blk, 2 * blk), lambda b, h, s: (h, 0, 0)),
            pl.BlockSpec((None, 1, 2 * blk), lambda b, h, s: (h, 0, 0)),
            pl.BlockSpec((1, blk), lambda b, h, s: (0, h)),
        ],
        out_specs=pl.BlockSpec((ts, blk), lambda b, h, s: (b * per_seq + s, h)),
        scratch_shapes=[pltpu.VMEM((SUBLANES, blk), F32), pltpu.VMEM((1, blk), F32)],
        compiler_params=_params("parallel", "parallel", "arbitrary"),
        name="rglru_scan",
    )(gx, gx, conv_w, conv_b, gate_w, gate_b, a_param)


def _forget_cumsum_kernel(f_ref, bf_ref, o_ref):
    x = f_ref[...] + bf_ref[...]
    c = jnp.minimum(x, 0.0) - jnp.log(1.0 + jnp.exp(-jnp.abs(x)))
    seq = c.shape[0]
    row = lax.broadcasted_iota(I32, c.shape, 0)
    step = 1
    while step < seq:
        c = c + jnp.where(row < step, 0.0, pltpu.roll(c, step, 0))
        step *= 2
    o_ref[...] = c.T[:FOX_HEADS, :]


def _forget_cumsum(f, b_f, bsz, seq):
    return pl.pallas_call(
        _forget_cumsum_kernel,
        out_shape=jax.ShapeDtypeStruct((bsz, FOX_HEADS, seq), F32),
        grid=(bsz,),
        in_specs=[
            pl.BlockSpec((seq, LANES), lambda b: (b, 0)),
            pl.BlockSpec((1, LANES), lambda b: (0, 0)),
        ],
        out_specs=pl.BlockSpec((None, FOX_HEADS, seq), lambda b: (b, 0, 0)),
        compiler_params=_params("parallel"),
        name="forget_gate_cumsum",
    )(f, b_f)


def _fox_attn_kernel(q_ref, k_ref, v_ref, ck_ref, o_ref, m_ref, acc_ref, sa_ref, sb_ref, *, scale):
    qi = pl.program_id(2)
    tq = q_ref.shape[0]
    tk = tq
    half = q_ref.shape[1] // 2
    first_head = lax.broadcasted_iota(I32, (1, q_ref.shape[1]), 1) < half

    qs = (q_ref[...].astype(F32) * (scale * LOG2E)).astype(BF16)
    zero = jnp.zeros_like(qs)
    q_heads = (jnp.where(first_head, qs, zero), jnp.where(first_head, zero, qs))
    m_ref[...] = jnp.full_like(m_ref, NEG_BIG)
    acc_ref[...] = jnp.zeros_like(acc_ref)

    def scores(j, s_ref):
        start = pl.multiple_of(j * tk, tk)
        k = k_ref[pl.ds(start, tk), :]
        ck = ck_ref[:, pl.ds(start, tk)] * LOG2E
        for hh in range(2):
            s = lax.dot_general(q_heads[hh], k, (((1,), (1,)), ((), ())), preferred_element_type=F32)
            s_ref[hh] = s - ck[hh:hh + 1, :]

    def accumulate(j, s_ref, diagonal):
        start = pl.multiple_of(j * tk, tk)
        v = v_ref[pl.ds(start, tk), :]
        ones = jnp.ones_like(v)
        v_heads = (jnp.where(first_head, v, ones), jnp.where(first_head, ones, v))
        for hh in range(2):
            s = s_ref[hh]
            if diagonal:
                row = lax.broadcasted_iota(I32, s.shape, 0)
                col = lax.broadcasted_iota(I32, s.shape, 1)
                s = jnp.where(col <= row, s, NEG_BIG)
            m_prev = m_ref[hh]
            m_new = jnp.maximum(m_prev, jnp.max(s, axis=-1, keepdims=True))
            alpha = jnp.exp2(m_prev - m_new)
            p = jnp.exp2(s - jnp.concatenate([m_new] * (tk // LANES), axis=1))
            m_ref[hh] = m_new
            acc_ref[hh] = alpha * acc_ref[hh] + jnp.dot(p.astype(BF16), v_heads[hh], preferred_element_type=F32)

    scores(0, sa_ref)

    def pair(p, carry):
        j = 2 * p
        scores(j + 1, sb_ref)
        accumulate(j, sa_ref, False)
        scores(j + 2, sa_ref)
        accumulate(j + 1, sb_ref, False)
        return carry

    lax.fori_loop(0, qi // 2, pair, 0)

    @pl.when(qi % 2 == 0)
    def _():
        accumulate(qi, sa_ref, True)

    @pl.when(qi % 2 == 1)
    def _():
        scores(qi, sb_ref)
        accumulate(qi - 1, sa_ref, False)
        accumulate(qi, sb_ref, True)

    a0 = acc_ref[0]
    a1 = acc_ref[1]
    o0 = a0 / pltpu.roll(a0, half, 1)
    o1 = a1 / pltpu.roll(a1, half, 1)
    o_ref[...] = jnp.where(first_head, o0, o1).astype(o_ref.dtype)


def _fox_attention(qkv, cum, bsz, seq):
    t = qkv.shape[0]
    d = qkv.shape[1] // 3
    pairs = d // LANES
    tq = ATTN_TILE
    per_seq = seq // tq
    scale = (d // FOX_HEADS) ** -0.5
    ck = cum.reshape(bsz, pairs, 2, seq)
    return pl.pallas_call(
        functools.partial(_fox_attn_kernel, scale=scale),
        out_shape=jax.ShapeDtypeStruct((t, d), BF16),
        grid=(bsz, pairs, per_seq),
        in_specs=[
            pl.BlockSpec((tq, LANES), lambda b, hp, qi: (b * per_seq + qi, hp)),
            pl.BlockSpec((seq, LANES), lambda b, hp, qi: (b, pairs + hp)),
            pl.BlockSpec((seq, LANES), lambda b, hp, qi: (b, 2 * pairs + hp)),
            pl.BlockSpec((None, None, 2, seq), lambda b, hp, qi: (b, hp, 0, 0)),
        ],
        out_specs=pl.BlockSpec((tq, LANES), lambda b, hp, qi: (b * per_seq + qi, hp)),
        scratch_shapes=[
            pltpu.VMEM((2, tq, LANES), F32),
            pltpu.VMEM((2, tq, LANES), F32),
            pltpu.VMEM((2, tq, tq), F32),
            pltpu.VMEM((2, tq, tq), F32),
        ],
        compiler_params=_params("parallel", "parallel", "arbitrary"),
        name="forgetting_attention",
    )(qkv, qkv, qkv, ck)


def _router_kernel(x_ref, g_ref, sh_ref, sc_ref, rw_ref, rb_ref,
                   h_ref, e_ref, gt_ref, rk_ref, cnt_ref, carry_ref):
    @pl.when(pl.program_id(0) == 0)
    def _():
        carry_ref[...] = jnp.zeros_like(carry_ref)

    h = _norm_mod(x_ref[...], g_ref[...], sh_ref[...], sc_ref[...])
    h_ref[...] = h
    logits = jnp.dot(h, rw_ref[...], precision=HIGHEST, preferred_element_type=F32) + rb_ref[...]
    tm = logits.shape[0]
    lane = lax.broadcasted_iota(I32, logits.shape, 1).astype(F32)

    work = logits
    picked = jnp.zeros(logits.shape, F32)
    vals, idxs = [], []
    for _ in range(TOP_K):
        mx = jnp.max(work, axis=-1, keepdims=True)
        idx = jnp.min(jnp.where(work == mx, lane, float(LANES)), axis=-1, keepdims=True)
        sel = lane == idx
        vals.append(mx)
        idxs.append(idx)
        picked = jnp.where(sel, 1.0, picked)
        work = jnp.where(sel, -3.0e38, work)

    ex = [jnp.exp(v - vals[0]) for v in vals]
    den = ex[0] + ex[1] + ex[2] + ex[3]

    r_i = lax.broadcasted_iota(I32, (tm, tm), 0)
    c_i = lax.broadcasted_iota(I32, (tm, tm), 1)
    earlier = jnp.where(c_i < r_i, 1.0, 0.0).astype(BF16)
    before = jnp.dot(earlier, picked.astype(BF16), preferred_element_type=F32) + carry_ref[...]
    carry_ref[...] = carry_ref[...] + jnp.sum(picked, axis=0, keepdims=True)
    cnt_ref[...] = carry_ref[...]

    e_out = jnp.zeros(logits.shape, F32)
    g_out = jnp.zeros(logits.shape, F32)
    r_out = jnp.zeros(logits.shape, F32)
    for k in range(TOP_K):
        rank = jnp.sum(jnp.where(lane == idxs[k], before, 0.0), axis=-1, keepdims=True)
        at_k = lane == float(k)
        e_out = jnp.where(at_k, idxs[k], e_out)
        g_out = jnp.where(at_k, ex[k] / den, g_out)
        r_out = jnp.where(at_k, rank, r_out)
    e_ref[...] = e_out.astype(I32)
    gt_ref[...] = g_out
    rk_ref[...] = r_out.astype(I32)


def _router(x, g, shift, scale, rw, rb, seq):
    t, d = x.shape
    tm = ROW_TILE
    per_seq = seq // tm
    slab = lambda dt: jax.ShapeDtypeStruct((t, LANES), dt)
    slab_spec = pl.BlockSpec((tm, LANES), lambda i: (i, 0))
    return pl.pallas_call(
        _router_kernel,
        out_shape=[jax.ShapeDtypeStruct((t, d), F32), slab(I32), slab(F32), slab(I32),
                   jax.ShapeDtypeStruct((1, LANES), F32)],
        grid=(t // tm,),
        in_specs=[
            pl.BlockSpec((tm, d), lambda i: (i, 0)),
            pl.BlockSpec((1, d), lambda i: (0, 0)),
            pl.BlockSpec((None, 1, d), lambda i: (i // per_seq, 0, 0)),
            pl.BlockSpec((None, 1, d), lambda i: (i // per_seq, 0, 0)),
            pl.BlockSpec((d, LANES), lambda i: (0, 0)),
            pl.BlockSpec((1, LANES), lambda i: (0, 0)),
        ],
        out_specs=[pl.BlockSpec((tm, d), lambda i: (i, 0)), slab_spec, slab_spec, slab_spec,
                   pl.BlockSpec((1, LANES), lambda i: (0, 0))],
        scratch_shapes=[pltpu.VMEM((1, LANES), F32)],
        compiler_params=_params("arbitrary"),
        name="moe_router",
    )(x, g, shift, scale, rw, rb)


def _dispatch_kernel(dest_ref, pad_end_ref, padded_ref, h_ref, xs_hbm, zero_ref, zsem, sem, *, td):
    i = pl.program_id(0)
    tile = zero_ref.shape[0]
    n_tiles = xs_hbm.shape[0] // tile
    n_exp = pad_end_ref.shape[0]
    n_used = pad_end_ref[n_exp - 1] // tile

    def zero_tile(tile_idx):
        start = pl.multiple_of(tile_idx * tile, tile)
        return pltpu.make_async_copy(zero_ref, xs_hbm.at[pl.ds(start, tile)], zsem)

    def for_zeroed_tiles(fn):
        def per_expert(e, carry):
            @pl.when(padded_ref[e] > 0)
            def _():
                fn(zero_tile(pad_end_ref[e] // tile - 1))
            return carry
        lax.fori_loop(0, n_exp, per_expert, 0)

        def per_tail(tile_idx, carry):
            fn(zero_tile(tile_idx))
            return carry
        lax.fori_loop(n_used, n_tiles, per_tail, 0)

    @pl.when(i == 0)
    def _():
        zero_ref[...] = jnp.zeros_like(zero_ref)
        for_zeroed_tiles(lambda cp: cp.start())
        for_zeroed_tiles(lambda cp: cp.wait())

    def issue(j, carry):
        for k in range(TOP_K):
            dst = dest_ref[(i * td + j) * TOP_K + k]
            pltpu.make_async_copy(h_ref.at[pl.ds(j, 1)], xs_hbm.at[pl.ds(dst, 1)], sem).start()
        return carry

    lax.fori_loop(0, td, issue, 0)
    n_copied = td * TOP_K
    pltpu.make_async_copy(xs_hbm.at[pl.ds(0, n_copied)], xs_hbm.at[pl.ds(0, n_copied)], sem).wait()


def _dispatch(dest, pad_end, padded, h, n_rows):
    t, d = h.shape
    td = DISPATCH_TILE
    grid_spec = pltpu.PrefetchScalarGridSpec(
        num_scalar_prefetch=3,
        grid=(t // td,),
        in_specs=[pl.BlockSpec((td, d), lambda i, de, pe, pa: (i, 0))],
        out_specs=pl.BlockSpec(memory_space=pl.ANY),
        scratch_shapes=[pltpu.VMEM((EXPERT_TILE, d), F32), pltpu.SemaphoreType.DMA, pltpu.SemaphoreType.DMA],
    )
    return pl.pallas_call(
        functools.partial(_dispatch_kernel, td=td),
        out_shape=jax.ShapeDtypeStruct((n_rows, d), F32),
        grid_spec=grid_spec,
        compiler_params=_params("arbitrary"),
        name="moe_dispatch",
    )(dest, pad_end, padded, h)


def _expert_kernel(te_ref, nu_ref, xs_ref, wgu_ref, bgu_ref, wd_ref, bd_ref, y_ref, wgu_bf, wd_bf):
    i = pl.program_id(0)

    @pl.when(i >= nu_ref[0])
    def _():
        y_ref[...] = jnp.zeros_like(y_ref)

    @pl.when(jnp.logical_or(i == 0, te_ref[i] != te_ref[jnp.maximum(i - 1, 0)]))
    def _():
        wgu_bf[...] = wgu_ref[...].astype(BF16)
        wd_bf[...] = wd_ref[...].astype(BF16)

    @pl.when(i < nu_ref[0])
    def _():
        f = wd_ref.shape[0]
        gu = jnp.dot(xs_ref[...].astype(BF16), wgu_bf[...], preferred_element_type=F32) + bgu_ref[...]
        gate = jnp.minimum(gu[:, :f], SWIGLU_LIMIT)
        up = jnp.clip(gu[:, f:], -SWIGLU_LIMIT, SWIGLU_LIMIT)
        act = gate * _sigmoid(SWIGLU_ALPHA * gate) * (up + 1.0)
        y_ref[...] = jnp.dot(act.astype(BF16), wd_bf[...], preferred_element_type=F32) + bd_ref[...]


def _expert_mlp(tile_e, n_used, xs, layer, wgu, bgu, wd, bd):
    n_rows, d = xs.shape
    tile = EXPERT_TILE
    f = wd.shape[2]
    row_map = lambda i, te, nu: (jnp.minimum(i, nu[0] - 1), 0)
    exp_map = lambda i, te, nu: (layer, te[i], 0, 0)
    grid_spec = pltpu.PrefetchScalarGridSpec(
        num_scalar_prefetch=2,
        grid=(n_rows // tile,),
        in_specs=[
            pl.BlockSpec((tile, d), row_map),
            pl.BlockSpec((None, None, d, 2 * f), exp_map),
            pl.BlockSpec((None, None, 1, 2 * f), exp_map),
            pl.BlockSpec((None, None, f, d), exp_map),
            pl.BlockSpec((None, None, 1, d), exp_map),
        ],
        out_specs=pl.BlockSpec((tile, d), lambda i, te, nu: (i, 0)),
        scratch_shapes=[pltpu.VMEM((d, 2 * f), BF16), pltpu.VMEM((f, d), BF16)],
    )
    return pl.pallas_call(
        _expert_kernel,
        out_shape=jax.ShapeDtypeStruct((n_rows, d), F32),
        grid_spec=grid_spec,
        compiler_params=_params("arbitrary"),
        name="moe_expert_mlp",
    )(tile_e, n_used, xs, wgu, bgu[:, :, None, :], wd, bd[:, :, None, :])


def _combine_kernel(dest_ref, x_ref, gt_ref, gate_ref, yb_hbm, o_ref, buf_ref, sem, *, tc):
    i = pl.program_id(0)
    n = pl.num_programs(0)

    def issue(step, slot):
        def body(j, carry):
            tok = step * tc + j
            for k in range(TOP_K):
                pltpu.make_async_copy(yb_hbm.at[pl.ds(dest_ref[tok * TOP_K + k], 1)],
                                      buf_ref.at[slot, k, pl.ds(j, 1)], sem.at[slot]).start()
            return carry
        lax.fori_loop(0, tc, body, 0)

    @pl.when(i == 0)
    def _():
        issue(0, 0)

    @pl.when(i + 1 < n)
    def _():
        issue(i + 1, (i + 1) % 2)

    slot = i % 2

    for k in range(TOP_K):
        pltpu.make_async_copy(yb_hbm.at[pl.ds(0, tc)], buf_ref.at[slot, k], sem.at[slot]).wait()

    gates = gt_ref[...]
    y = gates[:, 0:1] * buf_ref[slot, 0]
    for k in range(1, TOP_K):
        y = y + gates[:, k:k + 1] * buf_ref[slot, k]
    o_ref[...] = x_ref[...] + gate_ref[...] * y


def _combine(dest, x, gates, gate2, yb, seq):
    t, d = x.shape
    tc = COMBINE_TILE
    per_seq = seq // tc
    grid_spec = pltpu.PrefetchScalarGridSpec(
        num_scalar_prefetch=1,
        grid=(t // tc,),
        in_specs=[
            pl.BlockSpec((tc, d), lambda i, dr: (i, 0)),
            pl.BlockSpec((tc, LANES), lambda i, dr: (i, 0)),
            pl.BlockSpec((None, 1, d), lambda i, dr: (i // per_seq, 0, 0)),
            pl.BlockSpec(memory_space=pl.ANY),
        ],
        out_specs=pl.BlockSpec((tc, d), lambda i, dr: (i, 0)),
        scratch_shapes=[pltpu.VMEM((2, TOP_K, tc, d), F32), pltpu.SemaphoreType.DMA((2,))],
    )
    return pl.pallas_call(
        functools.partial(_combine_kernel, tc=tc),
        out_shape=jax.ShapeDtypeStruct((t, d), F32),
        grid_spec=grid_spec,
        compiler_params=_params("arbitrary"),
        name="moe_combine_residual",
    )(dest, x, gates, gate2, yb)


def _moe_layer(x, g, shift, scale, gate2, rw, rb, layer, wgu, bgu, wd, bd, seq):
    t, d = x.shape
    n_exp = rw.shape[1]
    rw_pad = jnp.zeros((d, LANES), F32).at[:, :n_exp].set(rw)
    rb_pad = jnp.full((1, LANES), NEG_BIG, F32).at[0, :n_exp].set(rb)
    h, e_slab, gt_slab, rk_slab, cnt = _router(x, g, shift, scale, rw_pad, rb_pad, seq)

    tile = EXPERT_TILE
    n_tiles = t * TOP_K // tile + n_exp
    counts = cnt[0, :n_exp].astype(I32)
    padded = (counts + tile - 1) // tile * tile
    pad_end = jnp.cumsum(padded)
    pad_start = pad_end - padded
    dest = (pad_start[e_slab[:, :TOP_K]] + rk_slab[:, :TOP_K]).reshape(-1)
    n_used = pad_end[-1] // tile
    tile_ids = jnp.arange(n_tiles, dtype=I32)
    first_row = jnp.minimum(tile_ids, n_used - 1) * tile
    tile_e = jnp.sum((pad_end[None, :] <= first_row[:, None]).astype(I32), axis=1)

    xs = _dispatch(dest, pad_end, padded, h, n_tiles * tile)
    yb = _expert_mlp(tile_e, n_used.reshape(1), xs, layer, wgu, bgu, wd, bd)
    return _combine(dest, x, gt_slab, gate2, yb, seq)


def _final_norm_kernel(x_ref, g_ref, o_ref):
    x = x_ref[...]
    ms = jnp.mean(x * x, axis=-1, keepdims=True)
    o_ref[...] = x * lax.rsqrt(ms + RMS_EPS) * g_ref[...]


def _final_norm(x, g):
    t, d = x.shape
    tm = ROW_TILE
    return pl.pallas_call(
        _final_norm_kernel,
        out_shape=jax.ShapeDtypeStruct((t, d), F32),
        grid=(t // tm,),
        in_specs=[pl.BlockSpec((tm, d), lambda i: (i, 0)), pl.BlockSpec((1, d), lambda i: (0, 0))],
        out_specs=pl.BlockSpec((tm, d), lambda i: (i, 0)),
        compiler_params=_params("parallel"),
        name="final_rmsnorm",
    )(x, g)


def kernel(x, c, norm_mix_g, norm_ffn_g, w_mod, b_mod, conv_w_in, conv_w, conv_w_out, lru_w_in, lru_conv_w, lru_conv_b, lru_gate_w, lru_gate_b, lru_a_param, lru_w_out, fox_w_in, fox_b_f, fox_w_out, router_w, router_b, moe_w_gate_up, moe_b_gate_up, moe_w_down, moe_b_down, final_g):
    bsz, seq, d = x.shape
    depth = w_mod.shape[0]
    t = bsz * seq
    mod = _modulation(c, w_mod, b_mod)
    xt = x.reshape(t, d)
    for i in range(depth):
        shift1, scale1, gate1 = mod[i, :, 0], mod[i, :, 1], mod[i, :, 2]
        shift2, scale2, gate2 = mod[i, :, 3], mod[i, :, 4], mod[i, :, 5]
        g_mix = norm_mix_g[i][None, :]
        kind, j = i % 3, i // 3
        if kind == 0:
            bcv = _norm_matmul(xt, g_mix, shift1, scale1, conv_w_in[j].astype(BF16), 3 * d, F32, seq)[0]
            xt = _conv_mixer(bcv, conv_w[j], conv_w_out[j].astype(BF16), xt, gate1, seq)
        elif kind == 1:
            gx = _norm_matmul(xt, g_mix, shift1, scale1, lru_w_in[j].astype(BF16), lru_w_in.shape[2], F32, seq)[0]
            y = _lru_core(gx, lru_conv_w[j], lru_conv_b[j][None, :], lru_gate_w[j].astype(BF16),
                          lru_gate_b[j][:, None, :], lru_a_param[j][None, :], bsz, seq)
            xt = _proj_residual(y, lru_w_out[j].astype(BF16), xt, gate1, seq)
        else:
            w_in = jnp.zeros((d, 3 * d + LANES), F32).at[:, :3 * d + FOX_HEADS].set(fox_w_in[j]).astype(BF16)
            qkv, f = _norm_matmul(xt, g_mix, shift1, scale1, w_in, 3 * d, BF16, seq)
            b_f = jnp.zeros((1, LANES), F32).at[0, :FOX_HEADS].set(fox_b_f[j])
            cum = _forget_cumsum(f, b_f, bsz, seq)
            o = _fox_attention(qkv, cum, bsz, seq)
            xt = _proj_residual(o, fox_w_out[j].astype(BF16), xt, gate1, seq)
        xt = _moe_layer(xt, norm_ffn_g[i][None, :], shift2, scale2, gate2, router_w[i], router_b[i],
                        i, moe_w_gate_up, moe_b_gate_up, moe_w_down, moe_b_down, seq)
    return _final_norm(xt, final_g[None, :]).reshape(bsz, seq, d)
```

```python
import functools

import jax
import jax.numpy as jnp
from jax import lax
from jax.experimental import pallas as pl
from jax.experimental.pallas import tpu as pltpu

F32 = jnp.float32
BF16 = jnp.bfloat16
I32 = jnp.int32
HIGHEST = lax.Precision.HIGHEST

RMS_EPS = 1e-6
LRU_HEADS = 4
LRU_C = 8.0
FOX_HEADS = 16
N_EXPERTS = 32
TOP_K = 4
SWIGLU_LIMIT = 7.0
SWIGLU_ALPHA = 1.702
NEG_BIG = -1e30
GELU_C = 0.7978845608028654
LOG2E = 1.4426950408889634

LANES = 128
SUBLANES = 8
VMEM_LIMIT = 56 * 1024 * 1024

ROW_TILE = 512
EXPERT_TILE = 256
DISPATCH_TILE = 256
COMBINE_TILE = 128
ATTN_TILE = 512


def _params(*sem):
    return pltpu.CompilerParams(dimension_semantics=sem, vmem_limit_bytes=VMEM_LIMIT)


def _sigmoid(z):
    return 1.0 / (1.0 + jnp.exp(-z))


def _norm_mod(x, g, shift, scale):
    ms = jnp.mean(x * x, axis=-1, keepdims=True)
    y = x * lax.rsqrt(ms + RMS_EPS) * g
    return y * (1.0 + scale) + shift


def _shifted_rows(u, carry_ref, j, row):
    sh = pltpu.roll(u, j, 0)
    for r in range(j):
        src = SUBLANES - j + r
        sh = jnp.where(row == r, carry_ref[src:src + 1, :], sh)
    return sh


def _mod_kernel(c_ref, w_ref, b_ref, o_ref):
    c = c_ref[...]
    ca = c * _sigmoid(c)
    o_ref[0] = jnp.dot(ca, w_ref[0], precision=HIGHEST, preferred_element_type=F32) + b_ref[0]


def _modulation(c, w_mod, b_mod):
    depth, d, n = w_mod.shape
    bsz = c.shape[0]
    rows = -(-bsz // SUBLANES) * SUBLANES
    c_pad = jnp.zeros((rows, d), F32).at[:bsz].set(c)
    tn = n // 4
    out = pl.pallas_call(
        _mod_kernel,
        out_shape=jax.ShapeDtypeStruct((depth, rows, n), F32),
        grid=(depth, n // tn),
        in_specs=[
            pl.BlockSpec((rows, d), lambda l, j: (0, 0)),
            pl.BlockSpec((1, d, tn), lambda l, j: (l, 0, j)),
            pl.BlockSpec((1, 1, tn), lambda l, j: (l, 0, j)),
        ],
        out_specs=pl.BlockSpec((1, rows, tn), lambda l, j: (l, 0, j)),
        compiler_params=_params("parallel", "parallel"),
        name="adaln_modulation",
    )(c_pad, w_mod, b_mod.reshape(depth, 1, n))
    return out[:, :bsz].reshape(depth, bsz, 6, 1, d)


def _norm_matmul_kernel(x_ref, g_ref, sh_ref, sc_ref, w_ref, *out_refs, n_main):
    h = _norm_mod(x_ref[...], g_ref[...], sh_ref[...], sc_ref[...]).astype(BF16)
    y = jnp.dot(h, w_ref[...], preferred_element_type=F32)
    out_refs[0][...] = y[:, :n_main].astype(out_refs[0].dtype)
    if len(out_refs) > 1:
        out_refs[1][...] = y[:, n_main:]


def _norm_matmul(x, g, shift, scale, w, n_main, out_dtype, seq):
    t, d = x.shape
    n = w.shape[1]
    tm = ROW_TILE
    per_seq = seq // tm
    out_shape = [jax.ShapeDtypeStruct((t, n_main), out_dtype)]
    out_specs = [pl.BlockSpec((tm, n_main), lambda i: (i, 0))]
    if n > n_main:
        out_shape.append(jax.ShapeDtypeStruct((t, n - n_main), F32))
        out_specs.append(pl.BlockSpec((tm, n - n_main), lambda i: (i, 0)))
    return pl.pallas_call(
        functools.partial(_norm_matmul_kernel, n_main=n_main),
        out_shape=out_shape,
        grid=(t // tm,),
        in_specs=[
            pl.BlockSpec((tm, d), lambda i: (i, 0)),
            pl.BlockSpec((1, d), lambda i: (0, 0)),
            pl.BlockSpec((None, 1, d), lambda i: (i // per_seq, 0, 0)),
            pl.BlockSpec((None, 1, d), lambda i: (i // per_seq, 0, 0)),
            pl.BlockSpec((d, n), lambda i: (0, 0)),
        ],
        out_specs=out_specs,
        compiler_params=_params("parallel"),
        name="norm_mod_in_proj",
    )(x, g, shift, scale, w)


def _proj_residual_kernel(z_ref, w_ref, x_ref, gate_ref, o_ref):
    y = jnp.dot(z_ref[...].astype(BF16), w_ref[...], preferred_element_type=F32)
    o_ref[...] = x_ref[...] + gate_ref[...] * y


def _proj_residual(z, w, x, gate, seq):
    t, d = x.shape
    tm = ROW_TILE
    per_seq = seq // tm
    return pl.pallas_call(
        _proj_residual_kernel,
        out_shape=jax.ShapeDtypeStruct((t, d), F32),
        grid=(t // tm,),
        in_specs=[
            pl.BlockSpec((tm, z.shape[1]), lambda i: (i, 0)),
            pl.BlockSpec(w.shape, lambda i: (0, 0)),
            pl.BlockSpec((tm, d), lambda i: (i, 0)),
            pl.BlockSpec((None, 1, d), lambda i: (i // per_seq, 0, 0)),
        ],
        out_specs=pl.BlockSpec((tm, d), lambda i: (i, 0)),
        compiler_params=_params("parallel"),
        name="out_proj_residual",
    )(z, w, x, gate)


def _conv_mixer_kernel(b_ref, c_ref, v_ref, cw_ref, w_ref, x_ref, gate_ref, o_ref, carry_ref, *, per_seq):
    @pl.when(pl.program_id(0) % per_seq == 0)
    def _():
        carry_ref[...] = jnp.zeros_like(carry_ref)

    cv = c_ref[...] * v_ref[...]
    tm = cv.shape[0]
    k_w = cw_ref.shape[0]
    row = lax.broadcasted_iota(I32, cv.shape, 0)
    conv = cw_ref[k_w - 1:k_w, :] * cv
    for j in range(1, k_w):
        conv = conv + cw_ref[k_w - 1 - j:k_w - j, :] * _shifted_rows(cv, carry_ref, j, row)
    carry_ref[...] = cv[tm - SUBLANES:, :]
    z = (b_ref[...] * conv).astype(BF16)
    y = jnp.dot(z, w_ref[...], preferred_element_type=F32)
    o_ref[...] = x_ref[...] + gate_ref[...] * y


def _conv_mixer(bcv, conv_w, w_out, x, gate, seq):
    t, d = x.shape
    tm = ROW_TILE
    per_seq = seq // tm
    return pl.pallas_call(
        functools.partial(_conv_mixer_kernel, per_seq=per_seq),
        out_shape=jax.ShapeDtypeStruct((t, d), F32),
        grid=(t // tm,),
        in_specs=[
            pl.BlockSpec((tm, d), lambda i: (i, 0)),
            pl.BlockSpec((tm, d), lambda i: (i, 1)),
            pl.BlockSpec((tm, d), lambda i: (i, 2)),
            pl.BlockSpec(conv_w.shape, lambda i: (0, 0)),
            pl.BlockSpec(w_out.shape, lambda i: (0, 0)),
            pl.BlockSpec((tm, d), lambda i: (i, 0)),
            pl.BlockSpec((None, 1, d), lambda i: (i // per_seq, 0, 0)),
        ],
        out_specs=pl.BlockSpec((tm, d), lambda i: (i, 0)),
        scratch_shapes=[pltpu.VMEM((SUBLANES, d), F32)],
        compiler_params=_params("arbitrary"),
        name="short_conv_mixer",
    )(bcv, bcv, bcv, conv_w, w_out, x, gate)


def _lru_kernel(xb_ref, gb_ref, cw_ref, cb_ref, gw_ref, gbias_ref, ap_ref, y_ref, xcarry_ref, hcarry_ref):
    @pl.when(pl.program_id(2) == 0)
    def _():
        xcarry_ref[...] = jnp.zeros_like(xcarry_ref)
        hcarry_ref[...] = jnp.zeros_like(hcarry_ref)

    x = xb_ref[...]
    ts, blk = x.shape
    k_w = cw_ref.shape[0]
    row = lax.broadcasted_iota(I32, x.shape, 0)
    conv = cw_ref[k_w - 1:k_w, :] * x
    for j in range(1, k_w):
        conv = conv + cw_ref[k_w - 1 - j:k_w - j, :] * _shifted_rows(x, xcarry_ref, j, row)
    xcarry_ref[...] = x[ts - SUBLANES:, :]
    xc = conv + cb_ref[...]

    g = jnp.dot(xc.astype(BF16), gw_ref[...], preferred_element_type=F32) + gbias_ref[...]
    r_gate = _sigmoid(g[:, :blk])
    i_gate = _sigmoid(g[:, blk:])
    ap = ap_ref[...]
    softplus = jnp.maximum(ap, 0.0) + jnp.log(1.0 + jnp.exp(-jnp.abs(ap)))
    a = jnp.exp((-LRU_C) * r_gate * softplus)
    u = jnp.sqrt(1.0 - a * a) * (i_gate * xc)

    a_cum, h_loc = a, u
    in_group = jnp.bitwise_and(row, SUBLANES - 1)
    step = 1
    while step < SUBLANES:
        a_prev = jnp.where(in_group < step, 1.0, pltpu.roll(a_cum, step, 0))
        h_prev = jnp.where(in_group < step, 0.0, pltpu.roll(h_loc, step, 0))
        h_loc = h_loc + a_cum * h_prev
        a_cum = a_cum * a_prev
        step *= 2
    carry = hcarry_ref[...]
    groups = []
    for g in range(ts // SUBLANES):
        lo = g * SUBLANES
        h_group = h_loc[lo:lo + SUBLANES, :] + a_cum[lo:lo + SUBLANES, :] * carry
        carry = h_group[SUBLANES - 1:SUBLANES, :]
        groups.append(h_group)
    hs = jnp.concatenate(groups, axis=0)
    hcarry_ref[...] = carry

    gb = gb_ref[...]
    gelu = 0.5 * gb * (1.0 + jnp.tanh(GELU_C * (gb + 0.044715 * (gb * gb * gb))))
    y_ref[...] = (gelu * hs).astype(y_ref.dtype)


def _lru_core(gx, conv_w, conv_b, gate_w, gate_b, a_param, bsz, seq):
    t = gx.shape[0]
    width = gx.shape[1] // 2
    blk = width // LRU_HEADS
    ts = ROW_TILE
    per_seq = seq // ts
    return pl.pallas_call(
        _lru_kernel,
        out_shape=jax.ShapeDtypeStruct((t, width), BF16),
        grid=(bsz, LRU_HEADS, per_seq),
        in_specs=[
            pl.BlockSpec((ts, blk), lambda b, h, s: (b * per_seq + s, LRU_HEADS + h)),
            pl.BlockSpec((ts, blk), lambda b, h, s: (b * per_seq + s, h)),
            pl.BlockSpec((conv_w.shape[0], blk), lambda b, h, s: (0, h)),
            pl.BlockSpec((1, blk), lambda b, h, s: (0, h)),
            pl.BlockSpec((None, blk, 2 * blk), lambda b, h, s: (h, 0, 0)),
            pl.BlockSpec((None, 1, 2 * blk), lambda b, h, s: (h, 0, 0)),
            pl.BlockSpec((1, blk), lambda b, h, s: (0, h)),
        ],
        out_specs=pl.BlockSpec((ts, blk), lambda b, h, s: (b * per_seq + s, h)),
        scratch_shapes=[pltpu.VMEM((SUBLANES, blk), F32), pltpu.VMEM((1, blk), F32)],
        compiler_params=_params("parallel", "parallel", "arbitrary"),
        name="rglru_scan",
    )(gx, gx, conv_w, conv_b, gate_w, gate_b, a_param)


def _forget_cumsum_kernel(f_ref, bf_ref, o_ref):
    x = f_ref[...] + bf_ref[...]
    c = jnp.minimum(x, 0.0) - jnp.log(1.0 + jnp.exp(-jnp.abs(x)))
    seq = c.shape[0]
    row = lax.broadcasted_iota(I32, c.shape, 0)
    step = 1
    while step < seq:
        c = c + jnp.where(row < step, 0.0, pltpu.roll(c, step, 0))
        step *= 2
    o_ref[...] = c.T[:FOX_HEADS, :]


def _forget_cumsum(f, b_f, bsz, seq):
    return pl.pallas_call(
        _forget_cumsum_kernel,
        out_shape=jax.ShapeDtypeStruct((bsz, FOX_HEADS, seq), F32),
        grid=(bsz,),
        in_specs=[
            pl.BlockSpec((seq, LANES), lambda b: (b, 0)),
            pl.BlockSpec((1, LANES), lambda b: (0, 0)),
        ],
        out_specs=pl.BlockSpec((None, FOX_HEADS, seq), lambda b: (b, 0, 0)),
        compiler_params=_params("parallel"),
        name="forget_gate_cumsum",
    )(f, b_f)


def _fox_attn_kernel(q_ref, k_ref, v_ref, ck_ref, o_ref, m_ref, acc_ref, sa_ref, sb_ref, *, scale):
    qi = pl.program_id(2)
    tq = q_ref.shape[0]
    tk = tq
    half = q_ref.shape[1] // 2
    first_head = lax.broadcasted_iota(I32, (1, q_ref.shape[1]), 1) < half

    qs = (q_ref[...].astype(F32) * (scale * LOG2E)).astype(BF16)
    zero = jnp.zeros_like(qs)
    q_heads = (jnp.where(first_head, qs, zero), jnp.where(first_head, zero, qs))
    m_ref[...] = jnp.full_like(m_ref, NEG_BIG)
    acc_ref[...] = jnp.zeros_like(acc_ref)

    def scores(j, s_ref):
        start = pl.multiple_of(j * tk, tk)
        k = k_ref[pl.ds(start, tk), :]
        ck = ck_ref[:, pl.ds(start, tk)] * LOG2E
        for hh in range(2):
            s = lax.dot_general(q_heads[hh], k, (((1,), (1,)), ((), ())), preferred_element_type=F32)
            s_ref[hh] = s - ck[hh:hh + 1, :]

    def accumulate(j, s_ref, diagonal):
        start = pl.multiple_of(j * tk, tk)
        v = v_ref[pl.ds(start, tk), :]
        ones = jnp.ones_like(v)
        v_heads = (jnp.where(first_head, v, ones), jnp.where(first_head, ones, v))
        for hh in range(2):
            s = s_ref[hh]
            if diagonal:
                row = lax.broadcasted_iota(I32, s.shape, 0)
                col = lax.broadcasted_iota(I32, s.shape, 1)
                s = jnp.where(col <= row, s, NEG_BIG)
            m_prev = m_ref[hh]
            m_new = jnp.maximum(m_prev, jnp.max(s, axis=-1, keepdims=True))
            alpha = jnp.exp2(m_prev - m_new)
            p = jnp.exp2(s - jnp.concatenate([m_new] * (tk // LANES), axis=1))
            m_ref[hh] = m_new
            acc_ref[hh] = alpha * acc_ref[hh] + jnp.dot(p.astype(BF16), v_heads[hh], preferred_element_type=F32)

    scores(0, sa_ref)

    def pair(p, carry):
        j = 2 * p
        scores(j + 1, sb_ref)
        accumulate(j, sa_ref, False)
        scores(j + 2, sa_ref)
        accumulate(j + 1, sb_ref, False)
        return carry

    lax.fori_loop(0, qi // 2, pair, 0)

    @pl.when(qi % 2 == 0)
    def _():
        accumulate(qi, sa_ref, True)

    @pl.when(qi % 2 == 1)
    def _():
        scores(qi, sb_ref)
        accumulate(qi - 1, sa_ref, False)
        accumulate(qi, sb_ref, True)

    a0 = acc_ref[0]
    a1 = acc_ref[1]
    o0 = a0 / pltpu.roll(a0, half, 1)
    o1 = a1 / pltpu.roll(a1, half, 1)
    o_ref[...] = jnp.where(first_head, o0, o1).astype(o_ref.dtype)


def _fox_attention(qkv, cum, bsz, seq):
    t = qkv.shape[0]
    d = qkv.shape[1] // 3
    pairs = d // LANES
    tq = ATTN_TILE
    per_seq = seq // tq
    scale = (d // FOX_HEADS) ** -0.5
    ck = cum.reshape(bsz, pairs, 2, seq)
    return pl.pallas_call(
        functools.partial(_fox_attn_kernel, scale=scale),
        out_shape=jax.ShapeDtypeStruct((t, d), BF16),
        grid=(bsz, pairs, per_seq),
        in_specs=[
            pl.BlockSpec((tq, LANES), lambda b, hp, qi: (b * per_seq + qi, hp)),
            pl.BlockSpec((seq, LANES), lambda b, hp, qi: (b, pairs + hp)),
            pl.BlockSpec((seq, LANES), lambda b, hp, qi: (b, 2 * pairs + hp)),
            pl.BlockSpec((None, None, 2, seq), lambda b, hp, qi: (b, hp, 0, 0)),
        ],
        out_specs=pl.BlockSpec((tq, LANES), lambda b, hp, qi: (b * per_seq + qi, hp)),
        scratch_shapes=[
            pltpu.VMEM((2, tq, LANES), F32),
            pltpu.VMEM((2, tq, LANES), F32),
            pltpu.VMEM((2, tq, tq), F32),
            pltpu.VMEM((2, tq, tq), F32),
        ],
        compiler_params=_params("parallel", "parallel", "arbitrary"),
        name="forgetting_attention",
    )(qkv, qkv, qkv, ck)


def _router_kernel(x_ref, g_ref, sh_ref, sc_ref, rw_ref, rb_ref,
                   h_ref, e_ref, gt_ref, rk_ref, cnt_ref, carry_ref):
    @pl.when(pl.program_id(0) == 0)
    def _():
        carry_ref[...] = jnp.zeros_like(carry_ref)

    h = _norm_mod(x_ref[...], g_ref[...], sh_ref[...], sc_ref[...])
    h_ref[...] = h
    logits = jnp.dot(h, rw_ref[...], precision=HIGHEST, preferred_element_type=F32) + rb_ref[...]
    tm = logits.shape[0]
    lane = lax.broadcasted_iota(I32, logits.shape, 1).astype(F32)

    work = logits
    picked = jnp.zeros(logits.shape, F32)
    vals, idxs = [], []
    for _ in range(TOP_K):
        mx = jnp.max(work, axis=-1, keepdims=True)
        idx = jnp.min(jnp.where(work == mx, lane, float(LANES)), axis=-1, keepdims=True)
        sel = lane == idx
        vals.append(mx)
        idxs.append(idx)
        picked = jnp.where(sel, 1.0, picked)
        work = jnp.where(sel, -3.0e38, work)

    ex = [jnp.exp(v - vals[0]) for v in vals]
    den = ex[0] + ex[1] + ex[2] + ex[3]

    r_i = lax.broadcasted_iota(I32, (tm, tm), 0)
    c_i = lax.broadcasted_iota(I32, (tm, tm), 1)
    earlier = jnp.where(c_i < r_i, 1.0, 0.0).astype(BF16)
    before = jnp.dot(earlier, picked.astype(BF16), preferred_element_type=F32) + carry_ref[...]
    carry_ref[...] = carry_ref[...] + jnp.sum(picked, axis=0, keepdims=True)
    cnt_ref[...] = carry_ref[...]

    e_out = jnp.zeros(logits.shape, F32)
    g_out = jnp.zeros(logits.shape, F32)
    r_out = jnp.zeros(logits.shape, F32)
    for k in range(TOP_K):
        rank = jnp.sum(jnp.where(lane == idxs[k], before, 0.0), axis=-1, keepdims=True)
        at_k = lane == float(k)
        e_out = jnp.where(at_k, idxs[k], e_out)
        g_out = jnp.where(at_k, ex[k] / den, g_out)
        r_out = jnp.where(at_k, rank, r_out)
    e_ref[...] = e_out.astype(I32)
    gt_ref[...] = g_out
    rk_ref[...] = r_out.astype(I32)


def _router(x, g, shift, scale, rw, rb, seq):
    t, d = x.shape
    tm = ROW_TILE
    per_seq = seq // tm
    slab = lambda dt: jax.ShapeDtypeStruct((t, LANES), dt)
    slab_spec = pl.BlockSpec((tm, LANES), lambda i: (i, 0))
    return pl.pallas_call(
        _router_kernel,
        out_shape=[jax.ShapeDtypeStruct((t, d), F32), slab(I32), slab(F32), slab(I32),
                   jax.ShapeDtypeStruct((1, LANES), F32)],
        grid=(t // tm,),
        in_specs=[
            pl.BlockSpec((tm, d), lambda i: (i, 0)),
            pl.BlockSpec((1, d), lambda i: (0, 0)),
            pl.BlockSpec((None, 1, d), lambda i: (i // per_seq, 0, 0)),
            pl.BlockSpec((None, 1, d), lambda i: (i // per_seq, 0, 0)),
            pl.BlockSpec((d, LANES), lambda i: (0, 0)),
            pl.BlockSpec((1, LANES), lambda i: (0, 0)),
        ],
        out_specs=[pl.BlockSpec((tm, d), lambda i: (i, 0)), slab_spec, slab_spec, slab_spec,
                   pl.BlockSpec((1, LANES), lambda i: (0, 0))],
        scratch_shapes=[pltpu.VMEM((1, LANES), F32)],
        compiler_params=_params("arbitrary"),
        name="moe_router",
    )(x, g, shift, scale, rw, rb)


def _dispatch_kernel(dest_ref, pad_end_ref, padded_ref, h_ref, xs_hbm, zero_ref, zsem, sem, *, td):
    i = pl.program_id(0)
    tile = zero_ref.shape[0]
    n_tiles = xs_hbm.shape[0] // tile
    n_exp = pad_end_ref.shape[0]
    n_used = pad_end_ref[n_exp - 1] // tile

    def zero_tile(tile_idx):
        start = pl.multiple_of(tile_idx * tile, tile)
        return pltpu.make_async_copy(zero_ref, xs_hbm.at[pl.ds(start, tile)], zsem)

    def for_zeroed_tiles(fn):
        def per_expert(e, carry):
            @pl.when(padded_ref[e] > 0)
            def _():
                fn(zero_tile(pad_end_ref[e] // tile - 1))
            return carry
        lax.fori_loop(0, n_exp, per_expert, 0)

        def per_tail(tile_idx, carry):
            fn(zero_tile(tile_idx))
            return carry
        lax.fori_loop(n_used, n_tiles, per_tail, 0)

    @pl.when(i == 0)
    def _():
        zero_ref[...] = jnp.zeros_like(zero_ref)
        for_zeroed_tiles(lambda cp: cp.start())
        for_zeroed_tiles(lambda cp: cp.wait())

    def issue(jo, carry):
        for u in range(SUBLANES):
            j = jo * SUBLANES + u
            for k in range(TOP_K):
                dst = dest_ref[(i * td + j) * TOP_K + k]
                pltpu.make_async_copy(h_ref.at[jo, pl.ds(u, 1)], xs_hbm.at[pl.ds(dst, 1)], sem).start()
        return carry

    lax.fori_loop(0, td // SUBLANES, issue, 0)
    n_copied = td * TOP_K
    pltpu.make_async_copy(xs_hbm.at[pl.ds(0, n_copied)], xs_hbm.at[pl.ds(0, n_copied)], sem).wait()


def _dispatch(dest, pad_end, padded, h, n_rows):
    t, d = h.shape
    td = DISPATCH_TILE
    grid_spec = pltpu.PrefetchScalarGridSpec(
        num_scalar_prefetch=3,
        grid=(t // td,),
        in_specs=[pl.BlockSpec((td // SUBLANES, SUBLANES, d), lambda i, de, pe, pa: (i, 0, 0))],
        out_specs=pl.BlockSpec(memory_space=pl.ANY),
        scratch_shapes=[pltpu.VMEM((EXPERT_TILE, d), F32), pltpu.SemaphoreType.DMA, pltpu.SemaphoreType.DMA],
    )
    return pl.pallas_call(
        functools.partial(_dispatch_kernel, td=td),
        out_shape=jax.ShapeDtypeStruct((n_rows, d), F32),
        grid_spec=grid_spec,
        compiler_params=_params("arbitrary"),
        name="moe_dispatch",
    )(dest, pad_end, padded, h.reshape(t // SUBLANES, SUBLANES, d))


def _expert_kernel(te_ref, nu_ref, xs_ref, wgu_ref, bgu_ref, wd_ref, bd_ref, y_ref, wgu_bf, wd_bf):
    i = pl.program_id(0)

    @pl.when(i >= nu_ref[0])
    def _():
        y_ref[...] = jnp.zeros_like(y_ref)

    @pl.when(jnp.logical_or(i == 0, te_ref[i] != te_ref[jnp.maximum(i - 1, 0)]))
    def _():
        wgu_bf[...] = wgu_ref[...].astype(BF16)
        wd_bf[...] = wd_ref[...].astype(BF16)

    @pl.when(i < nu_ref[0])
    def _():
        f = wd_ref.shape[0]
        gu = jnp.dot(xs_ref[...].astype(BF16), wgu_bf[...], preferred_element_type=F32) + bgu_ref[...]
        gate = jnp.minimum(gu[:, :f], SWIGLU_LIMIT)
        up = jnp.clip(gu[:, f:], -SWIGLU_LIMIT, SWIGLU_LIMIT)
        act = gate * _sigmoid(SWIGLU_ALPHA * gate) * (up + 1.0)
        y_ref[...] = jnp.dot(act.astype(BF16), wd_bf[...], preferred_element_type=F32) + bd_ref[...]


def _expert_mlp(tile_e, n_used, xs, layer, wgu, bgu, wd, bd):
    n_rows, d = xs.shape
    tile = EXPERT_TILE
    f = wd.shape[2]
    row_map = lambda i, te, nu: (jnp.minimum(i, nu[0] - 1), 0)
    exp_map = lambda i, te, nu: (layer, te[i], 0, 0)
    grid_spec = pltpu.PrefetchScalarGridSpec(
        num_scalar_prefetch=2,
        grid=(n_rows // tile,),
        in_specs=[
            pl.BlockSpec((tile, d), row_map),
            pl.BlockSpec((None, None, d, 2 * f), exp_map),
            pl.BlockSpec((None, None, 1, 2 * f), exp_map),
            pl.BlockSpec((None, None, f, d), exp_map),
            pl.BlockSpec((None, None, 1, d), exp_map),
        ],
        out_specs=pl.BlockSpec((tile, d), lambda i, te, nu: (i, 0)),
        scratch_shapes=[pltpu.VMEM((d, 2 * f), BF16), pltpu.VMEM((f, d), BF16)],
    )
    return pl.pallas_call(
        _expert_kernel,
        out_shape=jax.ShapeDtypeStruct((n_rows, d), F32),
        grid_spec=grid_spec,
        compiler_params=_params("arbitrary"),
        name="moe_expert_mlp",
    )(tile_e, n_used, xs, wgu, bgu[:, :, None, :], wd, bd[:, :, None, :])


def _combine_kernel(dest_ref, x_ref, gt_ref, gate_ref, yb_hbm, o_ref, buf_ref, sem, *, tc):
    i = pl.program_id(0)
    n = pl.num_programs(0)

    def issue(step, slot):
        def body(jo, carry):
            for u in range(SUBLANES):
                tok = step * tc + jo * SUBLANES + u
                for k in range(TOP_K):
                    pltpu.make_async_copy(yb_hbm.at[pl.ds(dest_ref[tok * TOP_K + k], 1)],
                                          buf_ref.at[slot, k, jo, pl.ds(u, 1)], sem.at[slot]).start()
            return carry
        lax.fori_loop(0, tc // SUBLANES, body, 0)

    @pl.when(i == 0)
    def _():
        issue(0, 0)

    @pl.when(i + 1 < n)
    def _():
        issue(i + 1, (i + 1) % 2)

    slot = i % 2

    for k in range(TOP_K):
        plane = buf_ref.at[slot, k]
        pltpu.make_async_copy(plane, plane, sem.at[slot]).wait()

    gates = gt_ref[...]
    d = x_ref.shape[1]
    y = gates[:, 0:1] * buf_ref[slot, 0].reshape(tc, d)
    for k in range(1, TOP_K):
        y = y + gates[:, k:k + 1] * buf_ref[slot, k].reshape(tc, d)
    o_ref[...] = x_ref[...] + gate_ref[...] * y


def _combine(dest, x, gates, gate2, yb, seq):
    t, d = x.shape
    tc = COMBINE_TILE
    per_seq = seq // tc
    grid_spec = pltpu.PrefetchScalarGridSpec(
        num_scalar_prefetch=1,
        grid=(t // tc,),
        in_specs=[
            pl.BlockSpec((tc, d), lambda i, dr: (i, 0)),
            pl.BlockSpec((tc, LANES), lambda i, dr: (i, 0)),
            pl.BlockSpec((None, 1, d), lambda i, dr: (i // per_seq, 0, 0)),
            pl.BlockSpec(memory_space=pl.ANY),
        ],
        out_specs=pl.BlockSpec((tc, d), lambda i, dr: (i, 0)),
        scratch_shapes=[pltpu.VMEM((2, TOP_K, tc // SUBLANES, SUBLANES, d), F32), pltpu.SemaphoreType.DMA((2,))],
    )
    return pl.pallas_call(
        functools.partial(_combine_kernel, tc=tc),
        out_shape=jax.ShapeDtypeStruct((t, d), F32),
        grid_spec=grid_spec,
        compiler_params=_params("arbitrary"),
        name="moe_combine_residual",
    )(dest, x, gates, gate2, yb)


def _moe_layer(x, g, shift, scale, gate2, rw, rb, layer, wgu, bgu, wd, bd, seq):
    t, d = x.shape
    n_exp = rw.shape[1]
    rw_pad = jnp.zeros((d, LANES), F32).at[:, :n_exp].set(rw)
    rb_pad = jnp.full((1, LANES), NEG_BIG, F32).at[0, :n_exp].set(rb)
    h, e_slab, gt_slab, rk_slab, cnt = _router(x, g, shift, scale, rw_pad, rb_pad, seq)

    tile = EXPERT_TILE
    n_tiles = t * TOP_K // tile + n_exp
    counts = cnt[0, :n_exp].astype(I32)
    padded = (counts + tile - 1) // tile * tile
    pad_end = jnp.cumsum(padded)
    pad_start = pad_end - padded
    dest = (pad_start[e_slab[:, :TOP_K]] + rk_slab[:, :TOP_K]).reshape(-1)
    n_used = pad_end[-1] // tile
    tile_ids = jnp.arange(n_tiles, dtype=I32)
    first_row = jnp.minimum(tile_ids, n_used - 1) * tile
    tile_e = jnp.sum((pad_end[None, :] <= first_row[:, None]).astype(I32), axis=1)

    xs = _dispatch(dest, pad_end, padded, h, n_tiles * tile)
    yb = _expert_mlp(tile_e, n_used.reshape(1), xs, layer, wgu, bgu, wd, bd)
    return _combine(dest, x, gt_slab, gate2, yb, seq)


def _final_norm_kernel(x_ref, g_ref, o_ref):
    x = x_ref[...]
    ms = jnp.mean(x * x, axis=-1, keepdims=True)
    o_ref[...] = x * lax.rsqrt(ms + RMS_EPS) * g_ref[...]


def _final_norm(x, g):
    t, d = x.shape
    tm = ROW_TILE
    return pl.pallas_call(
        _final_norm_kernel,
        out_shape=jax.ShapeDtypeStruct((t, d), F32),
        grid=(t // tm,),
        in_specs=[pl.BlockSpec((tm, d), lambda i: (i, 0)), pl.BlockSpec((1, d), lambda i: (0, 0))],
        out_specs=pl.BlockSpec((tm, d), lambda i: (i, 0)),
        compiler_params=_params("parallel"),
        name="final_rmsnorm",
    )(x, g)


def kernel(x, c, norm_mix_g, norm_ffn_g, w_mod, b_mod, conv_w_in, conv_w, conv_w_out, lru_w_in, lru_conv_w, lru_conv_b, lru_gate_w, lru_gate_b, lru_a_param, lru_w_out, fox_w_in, fox_b_f, fox_w_out, router_w, router_b, moe_w_gate_up, moe_b_gate_up, moe_w_down, moe_b_down, final_g):
    bsz, seq, d = x.shape
    depth = w_mod.shape[0]
    t = bsz * seq
    mod = _modulation(c, w_mod, b_mod)
    xt = x.reshape(t, d)
    for i in range(depth):
        shift1, scale1, gate1 = mod[i, :, 0], mod[i, :, 1], mod[i, :, 2]
        shift2, scale2, gate2 = mod[i, :, 3], mod[i, :, 4], mod[i, :, 5]
        g_mix = norm_mix_g[i][None, :]
        kind, j = i % 3, i // 3
        if kind == 0:
            bcv = _norm_matmul(xt, g_mix, shift1, scale1, conv_w_in[j].astype(BF16), 3 * d, F32, seq)[0]
            xt = _conv_mixer(bcv, conv_w[j], conv_w_out[j].astype(BF16), xt, gate1, seq)
        elif kind == 1:
            gx = _norm_matmul(xt, g_mix, shift1, scale1, lru_w_in[j].astype(BF16), lru_w_in.shape[2], F32, seq)[0]
            y = _lru_core(gx, lru_conv_w[j], lru_conv_b[j][None, :], lru_gate_w[j].astype(BF16),
                          lru_gate_b[j][:, None, :], lru_a_param[j][None, :], bsz, seq)
            xt = _proj_residual(y, lru_w_out[j].astype(BF16), xt, gate1, seq)
        else:
            w_in = jnp.zeros((d, 3 * d + LANES), F32).at[:, :3 * d + FOX_HEADS].set(fox_w_in[j]).astype(BF16)
            qkv, f = _norm_matmul(xt, g_mix, shift1, scale1, w_in, 3 * d, BF16, seq)
            b_f = jnp.zeros((1, LANES), F32).at[0, :FOX_HEADS].set(fox_b_f[j])
            cum = _forget_cumsum(f, b_f, bsz, seq)
            o = _fox_attention(qkv, cum, bsz, seq)
            xt = _proj_residual(o, fox_w_out[j].astype(BF16), xt, gate1, seq)
        xt = _moe_layer(xt, norm_ffn_g[i][None, :], shift2, scale2, gate2, router_w[i], router_b[i],
                        i, moe_w_gate_up, moe_b_gate_up, moe_w_down, moe_b_down, seq)
    return _final_norm(xt, final_g[None, :]).reshape(bsz, seq, d)
```

```python
import functools

import jax
import jax.numpy as jnp
from jax import lax
from jax.experimental import pallas as pl
from jax.experimental.pallas import tpu as pltpu

F32 = jnp.float32
BF16 = jnp.bfloat16
I32 = jnp.int32
HIGHEST = lax.Precision.HIGHEST

RMS_EPS = 1e-6
LRU_HEADS = 4
LRU_C = 8.0
FOX_HEADS = 16
N_EXPERTS = 32
TOP_K = 4
SWIGLU_LIMIT = 7.0
SWIGLU_ALPHA = 1.702
NEG_BIG = -1e30
GELU_C = 0.7978845608028654
LOG2E = 1.4426950408889634

LANES = 128
SUBLANES = 8
VMEM_LIMIT = 56 * 1024 * 1024

ROW_TILE = 512
EXPERT_TILE = 256
DISPATCH_TILE = 256
COMBINE_TILE = 128
ATTN_TILE = 512


def _params(*sem):
    return pltpu.CompilerParams(dimension_semantics=sem, vmem_limit_bytes=VMEM_LIMIT)


def _sigmoid(z):
    return 1.0 / (1.0 + jnp.exp(-z))


def _norm_mod(x, g, shift, scale):
    ms = jnp.mean(x * x, axis=-1, keepdims=True)
    y = x * lax.rsqrt(ms + RMS_EPS) * g
    return y * (1.0 + scale) + shift


def _shifted_rows(u, carry_ref, j, row):
    sh = pltpu.roll(u, j, 0)
    for r in range(j):
        src = SUBLANES - j + r
        sh = jnp.where(row == r, carry_ref[src:src + 1, :], sh)
    return sh


def _mod_kernel(c_ref, w_ref, b_ref, o_ref):
    c = c_ref[...]
    ca = c * _sigmoid(c)
    o_ref[0] = jnp.dot(ca, w_ref[0], precision=HIGHEST, preferred_element_type=F32) + b_ref[0]


def _modulation(c, w_mod, b_mod):
    depth, d, n = w_mod.shape
    bsz = c.shape[0]
    rows = -(-bsz // SUBLANES) * SUBLANES
    c_pad = jnp.zeros((rows, d), F32).at[:bsz].set(c)
    tn = n // 4
    out = pl.pallas_call(
        _mod_kernel,
        out_shape=jax.ShapeDtypeStruct((depth, rows, n), F32),
        grid=(depth, n // tn),
        in_specs=[
            pl.BlockSpec((rows, d), lambda l, j: (0, 0)),
            pl.BlockSpec((1, d, tn), lambda l, j: (l, 0, j)),
            pl.BlockSpec((1, 1, tn), lambda l, j: (l, 0, j)),
        ],
        out_specs=pl.BlockSpec((1, rows, tn), lambda l, j: (l, 0, j)),
        compiler_params=_params("parallel", "parallel"),
        name="adaln_modulation",
    )(c_pad, w_mod, b_mod.reshape(depth, 1, n))
    return out[:, :bsz].reshape(depth, bsz, 6, 1, d)


def _norm_matmul_kernel(x_ref, g_ref, sh_ref, sc_ref, w_ref, *out_refs, n_main):
    h = _norm_mod(x_ref[...], g_ref[...], sh_ref[...], sc_ref[...]).astype(BF16)
    y = jnp.dot(h, w_ref[...], preferred_element_type=F32)
    out_refs[0][...] = y[:, :n_main].astype(out_refs[0].dtype)
    if len(out_refs) > 1:
        out_refs[1][...] = y[:, n_main:]


def _norm_matmul(x, g, shift, scale, w, n_main, out_dtype, seq):
    t, d = x.shape
    n = w.shape[1]
    tm = ROW_TILE
    per_seq = seq // tm
    out_shape = [jax.ShapeDtypeStruct((t, n_main), out_dtype)]
    out_specs = [pl.BlockSpec((tm, n_main), lambda i: (i, 0))]
    if n > n_main:
        out_shape.append(jax.ShapeDtypeStruct((t, n - n_main), F32))
        out_specs.append(pl.BlockSpec((tm, n - n_main), lambda i: (i, 0)))
    return pl.pallas_call(
        functools.partial(_norm_matmul_kernel, n_main=n_main),
        out_shape=out_shape,
        grid=(t // tm,),
        in_specs=[
            pl.BlockSpec((tm, d), lambda i: (i, 0)),
            pl.BlockSpec((1, d), lambda i: (0, 0)),
            pl.BlockSpec((None, 1, d), lambda i: (i // per_seq, 0, 0)),
            pl.BlockSpec((None, 1, d), lambda i: (i // per_seq, 0, 0)),
            pl.BlockSpec((d, n), lambda i: (0, 0)),
        ],
        out_specs=out_specs,
        compiler_params=_params("parallel"),
        name="norm_mod_in_proj",
    )(x, g, shift, scale, w)


def _proj_residual_kernel(z_ref, w_ref, x_ref, gate_ref, o_ref):
    y = jnp.dot(z_ref[...].astype(BF16), w_ref[...], preferred_element_type=F32)
    o_ref[...] = x_ref[...] + gate_ref[...] * y


def _proj_residual(z, w, x, gate, seq):
    t, d = x.shape
    tm = ROW_TILE
    per_seq = seq // tm
    return pl.pallas_call(
        _proj_residual_kernel,
        out_shape=jax.ShapeDtypeStruct((t, d), F32),
        grid=(t // tm,),
        in_specs=[
            pl.BlockSpec((tm, z.shape[1]), lambda i: (i, 0)),
            pl.BlockSpec(w.shape, lambda i: (0, 0)),
            pl.BlockSpec((tm, d), lambda i: (i, 0)),
            pl.BlockSpec((None, 1, d), lambda i: (i // per_seq, 0, 0)),
        ],
        out_specs=pl.BlockSpec((tm, d), lambda i: (i, 0)),
        compiler_params=_params("parallel"),
        name="out_proj_residual",
    )(z, w, x, gate)


def _conv_mixer_kernel(b_ref, c_ref, v_ref, cw_ref, w_ref, x_ref, gate_ref, o_ref, carry_ref, *, per_seq):
    @pl.when(pl.program_id(0) % per_seq == 0)
    def _():
        carry_ref[...] = jnp.zeros_like(carry_ref)

    cv = c_ref[...] * v_ref[...]
    tm = cv.shape[0]
    k_w = cw_ref.shape[0]
    row = lax.broadcasted_iota(I32, cv.shape, 0)
    conv = cw_ref[k_w - 1:k_w, :] * cv
    for j in range(1, k_w):
        conv = conv + cw_ref[k_w - 1 - j:k_w - j, :] * _shifted_rows(cv, carry_ref, j, row)
    carry_ref[...] = cv[tm - SUBLANES:, :]
    z = (b_ref[...] * conv).astype(BF16)
    y = jnp.dot(z, w_ref[...], preferred_element_type=F32)
    o_ref[...] = x_ref[...] + gate_ref[...] * y


def _conv_mixer(bcv, conv_w, w_out, x, gate, seq):
    t, d = x.shape
    tm = ROW_TILE
    per_seq = seq // tm
    return pl.pallas_call(
        functools.partial(_conv_mixer_kernel, per_seq=per_seq),
        out_shape=jax.ShapeDtypeStruct((t, d), F32),
        grid=(t // tm,),
        in_specs=[
            pl.BlockSpec((tm, d), lambda i: (i, 0)),
            pl.BlockSpec((tm, d), lambda i: (i, 1)),
            pl.BlockSpec((tm, d), lambda i: (i, 2)),
            pl.BlockSpec(conv_w.shape, lambda i: (0, 0)),
            pl.BlockSpec(w_out.shape, lambda i: (0, 0)),
            pl.BlockSpec((tm, d), lambda i: (i, 0)),
            pl.BlockSpec((None, 1, d), lambda i: (i // per_seq, 0, 0)),
        ],
        out_specs=pl.BlockSpec((tm, d), lambda i: (i, 0)),
        scratch_shapes=[pltpu.VMEM((SUBLANES, d), F32)],
        compiler_params=_params("arbitrary"),
        name="short_conv_mixer",
    )(bcv, bcv, bcv, conv_w, w_out, x, gate)


def _lru_kernel(xb_ref, gb_ref, cw_ref, cb_ref, gw_ref, gbias_ref, ap_ref, y_ref, xcarry_ref, hcarry_ref):
    @pl.when(pl.program_id(2) == 0)
    def _():
        xcarry_ref[...] = jnp.zeros_like(xcarry_ref)
        hcarry_ref[...] = jnp.zeros_like(hcarry_ref)

    x = xb_ref[...]
    ts, blk = x.shape
    k_w = cw_ref.shape[0]
    row = lax.broadcasted_iota(I32, x.shape, 0)
    conv = cw_ref[k_w - 1:k_w, :] * x
    for j in range(1, k_w):
        conv = conv + cw_ref[k_w - 1 - j:k_w - j, :] * _shifted_rows(x, xcarry_ref, j, row)
    xcarry_ref[...] = x[ts - SUBLANES:, :]
    xc = conv + cb_ref[...]

    g = jnp.dot(xc.astype(BF16), gw_ref[...], preferred_element_type=F32) + gbias_ref[...]
    r_gate = _sigmoid(g[:, :blk])
    i_gate = _sigmoid(g[:, blk:])
    ap = ap_ref[...]
    softplus = jnp.maximum(ap, 0.0) + jnp.log(1.0 + jnp.exp(-jnp.abs(ap)))
    a = jnp.exp((-LRU_C) * r_gate * softplus)
    u = jnp.sqrt(1.0 - a * a) * (i_gate * xc)

    a_cum, h_loc = a, u
    in_group = jnp.bitwise_and(row, SUBLANES - 1)
    step = 1
    while step < SUBLANES:
        a_prev = jnp.where(in_group < step, 1.0, pltpu.roll(a_cum, step, 0))
        h_prev = jnp.where(in_group < step, 0.0, pltpu.roll(h_loc, step, 0))
        h_loc = h_loc + a_cum * h_prev
        a_cum = a_cum * a_prev
        step *= 2
    carry = hcarry_ref[...]
    groups = []
    for g in range(ts // SUBLANES):
        lo = g * SUBLANES
        h_group = h_loc[lo:lo + SUBLANES, :] + a_cum[lo:lo + SUBLANES, :] * carry
        carry = h_group[SUBLANES - 1:SUBLANES, :]
        groups.append(h_group)
    hs = jnp.concatenate(groups, axis=0)
    hcarry_ref[...] = carry

    gb = gb_ref[...]
    gelu = 0.5 * gb * (1.0 + jnp.tanh(GELU_C * (gb + 0.044715 * (gb * gb * gb))))
    y_ref[...] = (gelu * hs).astype(y_ref.dtype)


def _lru_core(gx, conv_w, conv_b, gate_w, gate_b, a_param, bsz, seq):
    t = gx.shape[0]
    width = gx.shape[1] // 2
    blk = width // LRU_HEADS
    ts = ROW_TILE
    per_seq = seq // ts
    return pl.pallas_call(
        _lru_kernel,
        out_shape=jax.ShapeDtypeStruct((t, width), BF16),
        grid=(bsz, LRU_HEADS, per_seq),
        in_specs=[
            pl.BlockSpec((ts, blk), lambda b, h, s: (b * per_seq + s, LRU_HEADS + h)),
            pl.BlockSpec((ts, blk), lambda b, h, s: (b * per_seq + s, h)),
            pl.BlockSpec((conv_w.shape[0], blk), lambda b, h, s: (0, h)),
            pl.BlockSpec((1, blk), lambda b, h, s: (0, h)),
            pl.BlockSpec((None, blk, 2 * blk), lambda b, h, s: (h, 0, 0)),
            pl.BlockSpec((None, 1, 2 * blk), lambda b, h, s: (h, 0, 0)),
            pl.BlockSpec((1, blk), lambda b, h, s: (0, h)),
        ],
        out_specs=pl.BlockSpec((ts, blk), lambda b, h, s: (b * per_seq + s, h)),
        scratch_shapes=[pltpu.VMEM((SUBLANES, blk), F32), pltpu.VMEM((1, blk), F32)],
        compiler_params=_params("parallel", "parallel", "arbitrary"),
        name="rglru_scan",
    )(gx, gx, conv_w, conv_b, gate_w, gate_b, a_param)


def _forget_cumsum_kernel(f_ref, bf_ref, o_ref):
    x = f_ref[...] + bf_ref[...]
    c = jnp.minimum(x, 0.0) - jnp.log(1.0 + jnp.exp(-jnp.abs(x)))
    seq = c.shape[0]
    row = lax.broadcasted_iota(I32, c.shape, 0)
    step = 1
    while step < seq:
        c = c + jnp.where(row < step, 0.0, pltpu.roll(c, step, 0))
        step *= 2
    o_ref[...] = c.T[:FOX_HEADS, :]


def _forget_cumsum(f, b_f, bsz, seq):
    return pl.pallas_call(
        _forget_cumsum_kernel,
        out_shape=jax.ShapeDtypeStruct((bsz, FOX_HEADS, seq), F32),
        grid=(bsz,),
        in_specs=[
            pl.BlockSpec((seq, LANES), lambda b: (b, 0)),
            pl.BlockSpec((1, LANES), lambda b: (0, 0)),
        ],
        out_specs=pl.BlockSpec((None, FOX_HEADS, seq), lambda b: (b, 0, 0)),
        compiler_params=_params("parallel"),
        name="forget_gate_cumsum",
    )(f, b_f)


def _fox_attn_kernel(q_ref, k_ref, v_ref, ck_ref, o_ref, m_ref, acc_ref, sa_ref, sb_ref, *, scale):
    qi = pl.program_id(2)
    tq = q_ref.shape[0]
    tk = tq
    half = q_ref.shape[1] // 2
    first_head = lax.broadcasted_iota(I32, (1, q_ref.shape[1]), 1) < half

    qs = (q_ref[...].astype(F32) * (scale * LOG2E)).astype(BF16)
    zero = jnp.zeros_like(qs)
    q_heads = (jnp.where(first_head, qs, zero), jnp.where(first_head, zero, qs))
    m_ref[...] = jnp.full_like(m_ref, NEG_BIG)
    acc_ref[...] = jnp.zeros_like(acc_ref)

    def scores(j, s_ref):
        start = pl.multiple_of(j * tk, tk)
        k = k_ref[pl.ds(start, tk), :]
        ck = ck_ref[:, pl.ds(start, tk)] * LOG2E
        for hh in range(2):
            s = lax.dot_general(q_heads[hh], k, (((1,), (1,)), ((), ())), preferred_element_type=F32)
            s_ref[hh] = s - ck[hh:hh + 1, :]

    def accumulate(j, s_ref, diagonal):
        start = pl.multiple_of(j * tk, tk)
        v = v_ref[pl.ds(start, tk), :]
        ones = jnp.ones_like(v)
        v_heads = (jnp.where(first_head, v, ones), jnp.where(first_head, ones, v))
        for hh in range(2):
            s = s_ref[hh]
            if diagonal:
                row = lax.broadcasted_iota(I32, s.shape, 0)
                col = lax.broadcasted_iota(I32, s.shape, 1)
                s = jnp.where(col <= row, s, NEG_BIG)
            m_prev = m_ref[hh]
            m_new = jnp.maximum(m_prev, jnp.max(s, axis=-1, keepdims=True))
            alpha = jnp.exp2(m_prev - m_new)
            p = jnp.exp2(s - jnp.concatenate([m_new] * (tk // LANES), axis=1))
            m_ref[hh] = m_new
            acc_ref[hh] = alpha * acc_ref[hh] + jnp.dot(p.astype(BF16), v_heads[hh], preferred_element_type=F32)

    scores(0, sa_ref)

    def pair(p, carry):
        j = 2 * p
        scores(j + 1, sb_ref)
        accumulate(j, sa_ref, False)
        scores(j + 2, sa_ref)
        accumulate(j + 1, sb_ref, False)
        return carry

    lax.fori_loop(0, qi // 2, pair, 0)

    @pl.when(qi % 2 == 0)
    def _():
        accumulate(qi, sa_ref, True)

    @pl.when(qi % 2 == 1)
    def _():
        scores(qi, sb_ref)
        accumulate(qi - 1, sa_ref, False)
        accumulate(qi, sb_ref, True)

    a0 = acc_ref[0]
    a1 = acc_ref[1]
    o0 = a0 / pltpu.roll(a0, half, 1)
    o1 = a1 / pltpu.roll(a1, half, 1)
    o_ref[...] = jnp.where(first_head, o0, o1).astype(o_ref.dtype)


def _fox_attention(qkv, cum, bsz, seq):
    t = qkv.shape[0]
    d = qkv.shape[1] // 3
    pairs = d // LANES
    tq = ATTN_TILE
    per_seq = seq // tq
    scale = (d // FOX_HEADS) ** -0.5
    ck = cum.reshape(bsz, pairs, 2, seq)
    return pl.pallas_call(
        functools.partial(_fox_attn_kernel, scale=scale),
        out_shape=jax.ShapeDtypeStruct((t, d), BF16),
        grid=(bsz, pairs, per_seq),
        in_specs=[
            pl.BlockSpec((tq, LANES), lambda b, hp, qi: (b * per_seq + qi, hp)),
            pl.BlockSpec((seq, LANES), lambda b, hp, qi: (b, pairs + hp)),
            pl.BlockSpec((seq, LANES), lambda b, hp, qi: (b, 2 * pairs + hp)),
            pl.BlockSpec((None, None, 2, seq), lambda b, hp, qi: (b, hp, 0, 0)),
        ],
        out_specs=pl.BlockSpec((tq, LANES), lambda b, hp, qi: (b * per_seq + qi, hp)),
        scratch_shapes=[
            pltpu.VMEM((2, tq, LANES), F32),
            pltpu.VMEM((2, tq, LANES), F32),
            pltpu.VMEM((2, tq, tq), F32),
            pltpu.VMEM((2, tq, tq), F32),
        ],
        compiler_params=_params("parallel", "parallel", "arbitrary"),
        name="forgetting_attention",
    )(qkv, qkv, qkv, ck)


def _router_kernel(x_ref, g_ref, sh_ref, sc_ref, rw_ref, rb_ref,
                   h_ref, e_ref, gt_ref, rk_ref, cnt_ref, carry_ref):
    @pl.when(pl.program_id(0) == 0)
    def _():
        carry_ref[...] = jnp.zeros_like(carry_ref)

    h = _norm_mod(x_ref[...], g_ref[...], sh_ref[...], sc_ref[...])
    h_ref[...] = h
    logits = jnp.dot(h, rw_ref[...], precision=HIGHEST, preferred_element_type=F32) + rb_ref[...]
    tm = logits.shape[0]
    lane = lax.broadcasted_iota(I32, logits.shape, 1).astype(F32)

    work = logits
    picked = jnp.zeros(logits.shape, F32)
    vals, idxs = [], []
    for _ in range(TOP_K):
        mx = jnp.max(work, axis=-1, keepdims=True)
        idx = jnp.min(jnp.where(work == mx, lane, float(LANES)), axis=-1, keepdims=True)
        sel = lane == idx
        vals.append(mx)
        idxs.append(idx)
        picked = jnp.where(sel, 1.0, picked)
        work = jnp.where(sel, -3.0e38, work)

    ex = [jnp.exp(v - vals[0]) for v in vals]
    den = ex[0] + ex[1] + ex[2] + ex[3]

    r_i = lax.broadcasted_iota(I32, (tm, tm), 0)
    c_i = lax.broadcasted_iota(I32, (tm, tm), 1)
    earlier = jnp.where(c_i < r_i, 1.0, 0.0).astype(BF16)
    before = jnp.dot(earlier, picked.astype(BF16), preferred_element_type=F32) + carry_ref[...]
    carry_ref[...] = carry_ref[...] + jnp.sum(picked, axis=0, keepdims=True)
    cnt_ref[...] = carry_ref[...]

    e_out = jnp.zeros(logits.shape, F32)
    g_out = jnp.zeros(logits.shape, F32)
    r_out = jnp.zeros(logits.shape, F32)
    for k in range(TOP_K):
        rank = jnp.sum(jnp.where(lane == idxs[k], before, 0.0), axis=-1, keepdims=True)
        at_k = lane == float(k)
        e_out = jnp.where(at_k, idxs[k], e_out)
        g_out = jnp.where(at_k, ex[k] / den, g_out)
        r_out = jnp.where(at_k, rank, r_out)
    e_ref[...] = e_out.astype(I32)
    gt_ref[...] = g_out
    rk_ref[...] = r_out.astype(I32)


def _router(x, g, shift, scale, rw, rb, seq):
    t, d = x.shape
    tm = ROW_TILE
    per_seq = seq // tm
    slab = lambda dt: jax.ShapeDtypeStruct((t, LANES), dt)
    slab_spec = pl.BlockSpec((tm, LANES), lambda i: (i, 0))
    return pl.pallas_call(
        _router_kernel,
        out_shape=[jax.ShapeDtypeStruct((t, d), F32), slab(I32), slab(F32), slab(I32),
                   jax.ShapeDtypeStruct((1, LANES), F32)],
        grid=(t // tm,),
        in_specs=[
            pl.BlockSpec((tm, d), lambda i: (i, 0)),
            pl.BlockSpec((1, d), lambda i: (0, 0)),
            pl.BlockSpec((None, 1, d), lambda i: (i // per_seq, 0, 0)),
            pl.BlockSpec((None, 1, d), lambda i: (i // per_seq, 0, 0)),
            pl.BlockSpec((d, LANES), lambda i: (0, 0)),
            pl.BlockSpec((1, LANES), lambda i: (0, 0)),
        ],
        out_specs=[pl.BlockSpec((tm, d), lambda i: (i, 0)), slab_spec, slab_spec, slab_spec,
                   pl.BlockSpec((1, LANES), lambda i: (0, 0))],
        scratch_shapes=[pltpu.VMEM((1, LANES), F32)],
        compiler_params=_params("arbitrary"),
        name="moe_router",
    )(x, g, shift, scale, rw, rb)


def _dispatch_kernel(dest_ref, pad_end_ref, padded_ref, h_ref, xs_hbm, zero_ref, zsem, sem, *, td):
    i = pl.program_id(0)
    tile = zero_ref.shape[0]
    n_tiles = xs_hbm.shape[0] // tile
    n_exp = pad_end_ref.shape[0]
    n_used = pad_end_ref[n_exp - 1] // tile

    def zero_tile(tile_idx):
        start = pl.multiple_of(tile_idx * tile, tile)
        return pltpu.make_async_copy(zero_ref, xs_hbm.at[pl.ds(start, tile)], zsem)

    def for_zeroed_tiles(fn):
        def per_expert(e, carry):
            @pl.when(padded_ref[e] > 0)
            def _():
                fn(zero_tile(pad_end_ref[e] // tile - 1))
            return carry
        lax.fori_loop(0, n_exp, per_expert, 0)

        def per_tail(tile_idx, carry):
            fn(zero_tile(tile_idx))
            return carry
        lax.fori_loop(n_used, n_tiles, per_tail, 0)

    @pl.when(i == 0)
    def _():
        zero_ref[...] = jnp.zeros_like(zero_ref)
        for_zeroed_tiles(lambda cp: cp.start())
        for_zeroed_tiles(lambda cp: cp.wait())

    for j in range(td):
        for k in range(TOP_K):
            dst = dest_ref[(i * td + j) * TOP_K + k]
            pltpu.make_async_copy(h_ref.at[j // SUBLANES, pl.ds(j % SUBLANES, 1)], xs_hbm.at[pl.ds(dst, 1)],
                                  sem).start()
    n_copied = td * TOP_K
    pltpu.make_async_copy(xs_hbm.at[pl.ds(0, n_copied)], xs_hbm.at[pl.ds(0, n_copied)], sem).wait()


def _dispatch(dest, pad_end, padded, h, n_rows):
    t, d = h.shape
    td = DISPATCH_TILE
    grid_spec = pltpu.PrefetchScalarGridSpec(
        num_scalar_prefetch=3,
        grid=(t // td,),
        in_specs=[pl.BlockSpec((td // SUBLANES, SUBLANES, d), lambda i, de, pe, pa: (i, 0, 0))],
        out_specs=pl.BlockSpec(memory_space=pl.ANY),
        scratch_shapes=[pltpu.VMEM((EXPERT_TILE, d), F32), pltpu.SemaphoreType.DMA, pltpu.SemaphoreType.DMA],
    )
    return pl.pallas_call(
        functools.partial(_dispatch_kernel, td=td),
        out_shape=jax.ShapeDtypeStruct((n_rows, d), F32),
        grid_spec=grid_spec,
        compiler_params=_params("arbitrary"),
        name="moe_dispatch",
    )(dest, pad_end, padded, h.reshape(t // SUBLANES, SUBLANES, d))


def _expert_kernel(te_ref, nu_ref, xs_ref, wgu_ref, bgu_ref, wd_ref, bd_ref, y_ref, wgu_bf, wd_bf):
    i = pl.program_id(0)

    @pl.when(i >= nu_ref[0])
    def _():
        y_ref[...] = jnp.zeros_like(y_ref)

    @pl.when(jnp.logical_or(i == 0, te_ref[i] != te_ref[jnp.maximum(i - 1, 0)]))
    def _():
        wgu_bf[...] = wgu_ref[...].astype(BF16)
        wd_bf[...] = wd_ref[...].astype(BF16)

    @pl.when(i < nu_ref[0])
    def _():
        f = wd_ref.shape[0]
        gu = jnp.dot(xs_ref[...].astype(BF16), wgu_bf[...], preferred_element_type=F32) + bgu_ref[...]
        gate = jnp.minimum(gu[:, :f], SWIGLU_LIMIT)
        up = jnp.clip(gu[:, f:], -SWIGLU_LIMIT, SWIGLU_LIMIT)
        act = gate * _sigmoid(SWIGLU_ALPHA * gate) * (up + 1.0)
        y_ref[...] = jnp.dot(act.astype(BF16), wd_bf[...], preferred_element_type=F32) + bd_ref[...]


def _expert_mlp(tile_e, n_used, xs, layer, wgu, bgu, wd, bd):
    n_rows, d = xs.shape
    tile = EXPERT_TILE
    f = wd.shape[2]
    row_map = lambda i, te, nu: (jnp.minimum(i, nu[0] - 1), 0)
    exp_map = lambda i, te, nu: (layer, te[i], 0, 0)
    grid_spec = pltpu.PrefetchScalarGridSpec(
        num_scalar_prefetch=2,
        grid=(n_rows // tile,),
        in_specs=[
            pl.BlockSpec((tile, d), row_map),
            pl.BlockSpec((None, None, d, 2 * f), exp_map),
            pl.BlockSpec((None, None, 1, 2 * f), exp_map),
            pl.BlockSpec((None, None, f, d), exp_map),
            pl.BlockSpec((None, None, 1, d), exp_map),
        ],
        out_specs=pl.BlockSpec((tile, d), lambda i, te, nu: (i, 0)),
        scratch_shapes=[pltpu.VMEM((d, 2 * f), BF16), pltpu.VMEM((f, d), BF16)],
    )
    return pl.pallas_call(
        _expert_kernel,
        out_shape=jax.ShapeDtypeStruct((n_rows, d), F32),
        grid_spec=grid_spec,
        compiler_params=_params("arbitrary"),
        name="moe_expert_mlp",
    )(tile_e, n_used, xs, wgu, bgu[:, :, None, :], wd, bd[:, :, None, :])


def _combine_kernel(dest_ref, x_ref, gt_ref, gate_ref, yb_hbm, o_ref, buf_a, buf_b, sem_a, sem_b, *, tc):
    i = pl.program_id(0)
    n = pl.num_programs(0)
    d = x_ref.shape[1]

    def start_rows(step, buf, sem):
        for j in range(tc):
            for k in range(TOP_K):
                pltpu.make_async_copy(yb_hbm.at[pl.ds(dest_ref[(step * tc + j) * TOP_K + k], 1)],
                                      buf.at[k, j // SUBLANES, pl.ds(j % SUBLANES, 1)], sem).start()

    def wait_rows(buf, sem):
        for k in range(TOP_K):
            pltpu.make_async_copy(buf.at[k], buf.at[k], sem).wait()

    @pl.when(i == 0)
    def _():
        start_rows(0, buf_a, sem_a)

    def step_fn(buf, sem, buf_next, sem_next):
        wait_rows(buf, sem)
        start_rows(jnp.minimum(i + 1, n - 1), buf_next, sem_next)
        gates = gt_ref[...]
        y = gates[:, 0:1] * buf[0].reshape(tc, d)
        for k in range(1, TOP_K):
            y = y + gates[:, k:k + 1] * buf[k].reshape(tc, d)
        o_ref[...] = x_ref[...] + gate_ref[...] * y

        @pl.when(i == n - 1)
        def _():
            wait_rows(buf_next, sem_next)

    @pl.when(i % 2 == 0)
    def _():
        step_fn(buf_a, sem_a, buf_b, sem_b)

    @pl.when(i % 2 == 1)
    def _():
        step_fn(buf_b, sem_b, buf_a, sem_a)


def _combine(dest, x, gates, gate2, yb, seq):
    t, d = x.shape
    tc = COMBINE_TILE
    per_seq = seq // tc
    grid_spec = pltpu.PrefetchScalarGridSpec(
        num_scalar_prefetch=1,
        grid=(t // tc,),
        in_specs=[
            pl.BlockSpec((tc, d), lambda i, dr: (i, 0)),
            pl.BlockSpec((tc, LANES), lambda i, dr: (i, 0)),
            pl.BlockSpec((None, 1, d), lambda i, dr: (i // per_seq, 0, 0)),
            pl.BlockSpec(memory_space=pl.ANY),
        ],
        out_specs=pl.BlockSpec((tc, d), lambda i, dr: (i, 0)),
        scratch_shapes=[pltpu.VMEM((TOP_K, tc // SUBLANES, SUBLANES, d), F32)] * 2 + [pltpu.SemaphoreType.DMA] * 2,
    )
    return pl.pallas_call(
        functools.partial(_combine_kernel, tc=tc),
        out_shape=jax.ShapeDtypeStruct((t, d), F32),
        grid_spec=grid_spec,
        compiler_params=_params("arbitrary"),
        name="moe_combine_residual",
    )(dest, x, gates, gate2, yb)


def _moe_layer(x, g, shift, scale, gate2, rw, rb, layer, wgu, bgu, wd, bd, seq):
    t, d = x.shape
    n_exp = rw.shape[1]
    rw_pad = jnp.zeros((d, LANES), F32).at[:, :n_exp].set(rw)
    rb_pad = jnp.full((1, LANES), NEG_BIG, F32).at[0, :n_exp].set(rb)
    h, e_slab, gt_slab, rk_slab, cnt = _router(x, g, shift, scale, rw_pad, rb_pad, seq)

    tile = EXPERT_TILE
    n_tiles = t * TOP_K // tile + n_exp
    counts = cnt[0, :n_exp].astype(I32)
    padded = (counts + tile - 1) // tile * tile
    pad_end = jnp.cumsum(padded)
    pad_start = pad_end - padded
    dest = (pad_start[e_slab[:, :TOP_K]] + rk_slab[:, :TOP_K]).reshape(-1)
    n_used = pad_end[-1] // tile
    tile_ids = jnp.arange(n_tiles, dtype=I32)
    first_row = jnp.minimum(tile_ids, n_used - 1) * tile
    tile_e = jnp.sum((pad_end[None, :] <= first_row[:, None]).astype(I32), axis=1)

    xs = _dispatch(dest, pad_end, padded, h, n_tiles * tile)
    yb = _expert_mlp(tile_e, n_used.reshape(1), xs, layer, wgu, bgu, wd, bd)
    return _combine(dest, x, gt_slab, gate2, yb, seq)


def _final_norm_kernel(x_ref, g_ref, o_ref):
    x = x_ref[...]
    ms = jnp.mean(x * x, axis=-1, keepdims=True)
    o_ref[...] = x * lax.rsqrt(ms + RMS_EPS) * g_ref[...]


def _final_norm(x, g):
    t, d = x.shape
    tm = ROW_TILE
    return pl.pallas_call(
        _final_norm_kernel,
        out_shape=jax.ShapeDtypeStruct((t, d), F32),
        grid=(t // tm,),
        in_specs=[pl.BlockSpec((tm, d), lambda i: (i, 0)), pl.BlockSpec((1, d), lambda i: (0, 0))],
        out_specs=pl.BlockSpec((tm, d), lambda i: (i, 0)),
        compiler_params=_params("parallel"),
        name="final_rmsnorm",
    )(x, g)


def kernel(x, c, norm_mix_g, norm_ffn_g, w_mod, b_mod, conv_w_in, conv_w, conv_w_out, lru_w_in, lru_conv_w, lru_conv_b, lru_gate_w, lru_gate_b, lru_a_param, lru_w_out, fox_w_in, fox_b_f, fox_w_out, router_w, router_b, moe_w_gate_up, moe_b_gate_up, moe_w_down, moe_b_down, final_g):
    bsz, seq, d = x.shape
    depth = w_mod.shape[0]
    t = bsz * seq
    mod = _modulation(c, w_mod, b_mod)
    xt = x.reshape(t, d)
    for i in range(depth):
        shift1, scale1, gate1 = mod[i, :, 0], mod[i, :, 1], mod[i, :, 2]
        shift2, scale2, gate2 = mod[i, :, 3], mod[i, :, 4], mod[i, :, 5]
        g_mix = norm_mix_g[i][None, :]
        kind, j = i % 3, i // 3
        if kind == 0:
            bcv = _norm_matmul(xt, g_mix, shift1, scale1, conv_w_in[j].astype(BF16), 3 * d, F32, seq)[0]
            xt = _conv_mixer(bcv, conv_w[j], conv_w_out[j].astype(BF16), xt, gate1, seq)
        elif kind == 1:
            gx = _norm_matmul(xt, g_mix, shift1, scale1, lru_w_in[j].astype(BF16), lru_w_in.shape[2], F32, seq)[0]
            y = _lru_core(gx, lru_conv_w[j], lru_conv_b[j][None, :], lru_gate_w[j].astype(BF16),
                          lru_gate_b[j][:, None, :], lru_a_param[j][None, :], bsz, seq)
            xt = _proj_residual(y, lru_w_out[j].astype(BF16), xt, gate1, seq)
        else:
            w_in = jnp.zeros((d, 3 * d + LANES), F32).at[:, :3 * d + FOX_HEADS].set(fox_w_in[j]).astype(BF16)
            qkv, f = _norm_matmul(xt, g_mix, shift1, scale1, w_in, 3 * d, BF16, seq)
            b_f = jnp.zeros((1, LANES), F32).at[0, :FOX_HEADS].set(fox_b_f[j])
            cum = _forget_cumsum(f, b_f, bsz, seq)
            o = _fox_attention(qkv, cum, bsz, seq)
            xt = _proj_residual(o, fox_w_out[j].astype(BF16), xt, gate1, seq)
        xt = _moe_layer(xt, norm_ffn_g[i][None, :], shift2, scale2, gate2, router_w[i], router_b[i],
                        i, moe_w_gate_up, moe_b_gate_up, moe_w_down, moe_b_down, seq)
    return _final_norm(xt, final_g[None, :]).reshape(bsz, seq, d)
```

```python
import functools

import jax
import jax.numpy as jnp
from jax import lax
from jax.experimental import pallas as pl
from jax.experimental.pallas import tpu as pltpu

F32 = jnp.float32
BF16 = jnp.bfloat16
I32 = jnp.int32
HIGHEST = lax.Precision.HIGHEST

RMS_EPS = 1e-6
LRU_HEADS = 4
LRU_C = 8.0
FOX_HEADS = 16
N_EXPERTS = 32
TOP_K = 4
SWIGLU_LIMIT = 7.0
SWIGLU_ALPHA = 1.702
NEG_BIG = -1e30
GELU_C = 0.7978845608028654
LOG2E = 1.4426950408889634

LANES = 128
SUBLANES = 8
VMEM_LIMIT = 56 * 1024 * 1024

ROW_TILE = 512
EXPERT_TILE = 256
DISPATCH_TILE = 512
COMBINE_TILE = 128
ATTN_TILE = 512


def _params(*sem):
    return pltpu.CompilerParams(dimension_semantics=sem, vmem_limit_bytes=VMEM_LIMIT)


def _sigmoid(z):
    return 1.0 / (1.0 + jnp.exp(-z))


def _norm_mod(x, g, shift, scale):
    ms = jnp.mean(x * x, axis=-1, keepdims=True)
    y = x * lax.rsqrt(ms + RMS_EPS) * g
    return y * (1.0 + scale) + shift


def _shifted_rows(u, carry_ref, j, row):
    sh = pltpu.roll(u, j, 0)
    for r in range(j):
        src = SUBLANES - j + r
        sh = jnp.where(row == r, carry_ref[src:src + 1, :], sh)
    return sh


def _mod_kernel(c_ref, w_ref, b_ref, o_ref):
    c = c_ref[...]
    ca = c * _sigmoid(c)
    o_ref[0] = jnp.dot(ca, w_ref[0], precision=HIGHEST, preferred_element_type=F32) + b_ref[0]


def _modulation(c, w_mod, b_mod):
    depth, d, n = w_mod.shape
    bsz = c.shape[0]
    rows = -(-bsz // SUBLANES) * SUBLANES
    c_pad = jnp.zeros((rows, d), F32).at[:bsz].set(c)
    tn = n // 4
    out = pl.pallas_call(
        _mod_kernel,
        out_shape=jax.ShapeDtypeStruct((depth, rows, n), F32),
        grid=(depth, n // tn),
        in_specs=[
            pl.BlockSpec((rows, d), lambda l, j: (0, 0)),
            pl.BlockSpec((1, d, tn), lambda l, j: (l, 0, j)),
            pl.BlockSpec((1, 1, tn), lambda l, j: (l, 0, j)),
        ],
        out_specs=pl.BlockSpec((1, rows, tn), lambda l, j: (l, 0, j)),
        compiler_params=_params("parallel", "parallel"),
        name="adaln_modulation",
    )(c_pad, w_mod, b_mod.reshape(depth, 1, n))
    return out[:, :bsz].reshape(depth, bsz, 6, 1, d)


def _norm_matmul_kernel(x_ref, g_ref, sh_ref, sc_ref, w_ref, *out_refs, n_main):
    h = _norm_mod(x_ref[...], g_ref[...], sh_ref[...], sc_ref[...]).astype(BF16)
    y = jnp.dot(h, w_ref[...], preferred_element_type=F32)
    out_refs[0][...] = y[:, :n_main].astype(out_refs[0].dtype)
    if len(out_refs) > 1:
        out_refs[1][...] = y[:, n_main:]


def _norm_matmul(x, g, shift, scale, w, n_main, out_dtype, seq):
    t, d = x.shape
    n = w.shape[1]
    tm = ROW_TILE
    per_seq = seq // tm
    out_shape = [jax.ShapeDtypeStruct((t, n_main), out_dtype)]
    out_specs = [pl.BlockSpec((tm, n_main), lambda i: (i, 0))]
    if n > n_main:
        out_shape.append(jax.ShapeDtypeStruct((t, n - n_main), F32))
        out_specs.append(pl.BlockSpec((tm, n - n_main), lambda i: (i, 0)))
    return pl.pallas_call(
        functools.partial(_norm_matmul_kernel, n_main=n_main),
        out_shape=out_shape,
        grid=(t // tm,),
        in_specs=[
            pl.BlockSpec((tm, d), lambda i: (i, 0)),
            pl.BlockSpec((1, d), lambda i: (0, 0)),
            pl.BlockSpec((None, 1, d), lambda i: (i // per_seq, 0, 0)),
            pl.BlockSpec((None, 1, d), lambda i: (i // per_seq, 0, 0)),
            pl.BlockSpec((d, n), lambda i: (0, 0)),
        ],
        out_specs=out_specs,
        compiler_params=_params("parallel"),
        name="norm_mod_in_proj",
    )(x, g, shift, scale, w)


def _proj_residual_kernel(z_ref, w_ref, x_ref, gate_ref, o_ref):
    y = jnp.dot(z_ref[...].astype(BF16), w_ref[...], preferred_element_type=F32)
    o_ref[...] = x_ref[...] + gate_ref[...] * y


def _proj_residual(z, w, x, gate, seq):
    t, d = x.shape
    tm = ROW_TILE
    per_seq = seq // tm
    return pl.pallas_call(
        _proj_residual_kernel,
        out_shape=jax.ShapeDtypeStruct((t, d), F32),
        grid=(t // tm,),
        in_specs=[
            pl.BlockSpec((tm, z.shape[1]), lambda i: (i, 0)),
            pl.BlockSpec(w.shape, lambda i: (0, 0)),
            pl.BlockSpec((tm, d), lambda i: (i, 0)),
            pl.BlockSpec((None, 1, d), lambda i: (i // per_seq, 0, 0)),
        ],
        out_specs=pl.BlockSpec((tm, d), lambda i: (i, 0)),
        compiler_params=_params("parallel"),
        name="out_proj_residual",
    )(z, w, x, gate)


def _conv_mixer_kernel(b_ref, c_ref, v_ref, cw_ref, w_ref, x_ref, gate_ref, o_ref, carry_ref, *, per_seq):
    @pl.when(pl.program_id(0) % per_seq == 0)
    def _():
        carry_ref[...] = jnp.zeros_like(carry_ref)

    cv = c_ref[...] * v_ref[...]
    tm = cv.shape[0]
    k_w = cw_ref.shape[0]
    row = lax.broadcasted_iota(I32, cv.shape, 0)
    conv = cw_ref[k_w - 1:k_w, :] * cv
    for j in range(1, k_w):
        conv = conv + cw_ref[k_w - 1 - j:k_w - j, :] * _shifted_rows(cv, carry_ref, j, row)
    carry_ref[...] = cv[tm - SUBLANES:, :]
    z = (b_ref[...] * conv).astype(BF16)
    y = jnp.dot(z, w_ref[...], preferred_element_type=F32)
    o_ref[...] = x_ref[...] + gate_ref[...] * y


def _conv_mixer(bcv, conv_w, w_out, x, gate, seq):
    t, d = x.shape
    tm = ROW_TILE
    per_seq = seq // tm
    return pl.pallas_call(
        functools.partial(_conv_mixer_kernel, per_seq=per_seq),
        out_shape=jax.ShapeDtypeStruct((t, d), F32),
        grid=(t // tm,),
        in_specs=[
            pl.BlockSpec((tm, d), lambda i: (i, 0)),
            pl.BlockSpec((tm, d), lambda i: (i, 1)),
            pl.BlockSpec((tm, d), lambda i: (i, 2)),
            pl.BlockSpec(conv_w.shape, lambda i: (0, 0)),
            pl.BlockSpec(w_out.shape, lambda i: (0, 0)),
            pl.BlockSpec((tm, d), lambda i: (i, 0)),
            pl.BlockSpec((None, 1, d), lambda i: (i // per_seq, 0, 0)),
        ],
        out_specs=pl.BlockSpec((tm, d), lambda i: (i, 0)),
        scratch_shapes=[pltpu.VMEM((SUBLANES, d), F32)],
        compiler_params=_params("arbitrary"),
        name="short_conv_mixer",
    )(bcv, bcv, bcv, conv_w, w_out, x, gate)


def _lru_kernel(xb_ref, gb_ref, cw_ref, cb_ref, gw_ref, gbias_ref, ap_ref, y_ref, xcarry_ref, hcarry_ref):
    @pl.when(pl.program_id(2) == 0)
    def _():
        xcarry_ref[...] = jnp.zeros_like(xcarry_ref)
        hcarry_ref[...] = jnp.zeros_like(hcarry_ref)

    x = xb_ref[...]
    ts, blk = x.shape
    k_w = cw_ref.shape[0]
    row = lax.broadcasted_iota(I32, x.shape, 0)
    conv = cw_ref[k_w - 1:k_w, :] * x
    for j in range(1, k_w):
        conv = conv + cw_ref[k_w - 1 - j:k_w - j, :] * _shifted_rows(x, xcarry_ref, j, row)
    xcarry_ref[...] = x[ts - SUBLANES:, :]
    xc = conv + cb_ref[...]

    g = jnp.dot(xc.astype(BF16), gw_ref[...], preferred_element_type=F32) + gbias_ref[...]
    r_gate = _sigmoid(g[:, :blk])
    i_gate = _sigmoid(g[:, blk:])
    ap = ap_ref[...]
    softplus = jnp.maximum(ap, 0.0) + jnp.log(1.0 + jnp.exp(-jnp.abs(ap)))
    a = jnp.exp((-LRU_C) * r_gate * softplus)
    u = jnp.sqrt(1.0 - a * a) * (i_gate * xc)

    a_cum, h_loc = a, u
    in_group = jnp.bitwise_and(row, SUBLANES - 1)
    step = 1
    while step < SUBLANES:
        a_prev = jnp.where(in_group < step, 1.0, pltpu.roll(a_cum, step, 0))
        h_prev = jnp.where(in_group < step, 0.0, pltpu.roll(h_loc, step, 0))
        h_loc = h_loc + a_cum * h_prev
        a_cum = a_cum * a_prev
        step *= 2
    carry = hcarry_ref[...]
    groups = []
    for g in range(ts // SUBLANES):
        lo = g * SUBLANES
        h_group = h_loc[lo:lo + SUBLANES, :] + a_cum[lo:lo + SUBLANES, :] * carry
        carry = h_group[SUBLANES - 1:SUBLANES, :]
        groups.append(h_group)
    hs = jnp.concatenate(groups, axis=0)
    hcarry_ref[...] = carry

    gb = gb_ref[...]
    gelu = 0.5 * gb * (1.0 + jnp.tanh(GELU_C * (gb + 0.044715 * (gb * gb * gb))))
    y_ref[...] = (gelu * hs).astype(y_ref.dtype)


def _lru_core(gx, conv_w, conv_b, gate_w, gate_b, a_param, bsz, seq):
    t = gx.shape[0]
    width = gx.shape[1] // 2
    blk = width // LRU_HEADS
    ts = ROW_TILE
    per_seq = seq // ts
    return pl.pallas_call(
        _lru_kernel,
        out_shape=jax.ShapeDtypeStruct((t, width), BF16),
        grid=(bsz, LRU_HEADS, per_seq),
        in_specs=[
            pl.BlockSpec((ts, blk), lambda b, h, s: (b * per_seq + s, LRU_HEADS + h)),
            pl.BlockSpec((ts, blk), lambda b, h, s: (b * per_seq + s, h)),
            pl.BlockSpec((conv_w.shape[0], blk), lambda b, h, s: (0, h)),
            pl.BlockSpec((1, blk), lambda b, h, s: (0, h)),
            pl.BlockSpec((None, blk, 2 * blk), lambda b, h, s: (h, 0, 0)),
            pl.BlockSpec((None, 1, 2 * blk), lambda b, h, s: (h, 0, 0)),
            pl.BlockSpec((1, blk), lambda b, h, s: (0, h)),
        ],
        out_specs=pl.BlockSpec((ts, blk), lambda b, h, s: (b * per_seq + s, h)),
        scratch_shapes=[pltpu.VMEM((SUBLANES, blk), F32), pltpu.VMEM((1, blk), F32)],
        compiler_params=_params("parallel", "parallel", "arbitrary"),
        name="rglru_scan",
    )(gx, gx, conv_w, conv_b, gate_w, gate_b, a_param)


def _forget_cumsum_kernel(f_ref, bf_ref, o_ref):
    x = f_ref[...] + bf_ref[...]
    c = jnp.minimum(x, 0.0) - jnp.log(1.0 + jnp.exp(-jnp.abs(x)))
    seq = c.shape[0]
    row = lax.broadcasted_iota(I32, c.shape, 0)
    step = 1
    while step < seq:
        c = c + jnp.where(row < step, 0.0, pltpu.roll(c, step, 0))
        step *= 2
    o_ref[...] = c.T[:FOX_HEADS, :]


def _forget_cumsum(f, b_f, bsz, seq):
    return pl.pallas_call(
        _forget_cumsum_kernel,
        out_shape=jax.ShapeDtypeStruct((bsz, FOX_HEADS, seq), F32),
        grid=(bsz,),
        in_specs=[
            pl.BlockSpec((seq, LANES), lambda b: (b, 0)),
            pl.BlockSpec((1, LANES), lambda b: (0, 0)),
        ],
        out_specs=pl.BlockSpec((None, FOX_HEADS, seq), lambda b: (b, 0, 0)),
        compiler_params=_params("parallel"),
        name="forget_gate_cumsum",
    )(f, b_f)


def _fox_attn_kernel(q_ref, k_ref, v_ref, ck_ref, o_ref, m_ref, acc_ref, sa_ref, sb_ref, *, scale):
    qi = pl.program_id(2)
    tq = q_ref.shape[0]
    tk = tq
    half = q_ref.shape[1] // 2
    first_head = lax.broadcasted_iota(I32, (1, q_ref.shape[1]), 1) < half

    qs = (q_ref[...].astype(F32) * (scale * LOG2E)).astype(BF16)
    zero = jnp.zeros_like(qs)
    q_heads = (jnp.where(first_head, qs, zero), jnp.where(first_head, zero, qs))
    m_ref[...] = jnp.full_like(m_ref, NEG_BIG)
    acc_ref[...] = jnp.zeros_like(acc_ref)

    def scores(j, s_ref):
        start = pl.multiple_of(j * tk, tk)
        k = k_ref[pl.ds(start, tk), :]
        ck = ck_ref[:, pl.ds(start, tk)] * LOG2E
        for hh in range(2):
            s = lax.dot_general(q_heads[hh], k, (((1,), (1,)), ((), ())), preferred_element_type=F32)
            s_ref[hh] = s - ck[hh:hh + 1, :]

    def accumulate(j, s_ref, diagonal):
        start = pl.multiple_of(j * tk, tk)
        v = v_ref[pl.ds(start, tk), :]
        ones = jnp.ones_like(v)
        v_heads = (jnp.where(first_head, v, ones), jnp.where(first_head, ones, v))
        for hh in range(2):
            s = s_ref[hh]
            if diagonal:
                row = lax.broadcasted_iota(I32, s.shape, 0)
                col = lax.broadcasted_iota(I32, s.shape, 1)
                s = jnp.where(col <= row, s, NEG_BIG)
            m_prev = m_ref[hh]
            m_new = jnp.maximum(m_prev, jnp.max(s, axis=-1, keepdims=True))
            alpha = jnp.exp2(m_prev - m_new)
            p = jnp.exp2(s - jnp.concatenate([m_new] * (tk // LANES), axis=1))
            m_ref[hh] = m_new
            acc_ref[hh] = alpha * acc_ref[hh] + jnp.dot(p.astype(BF16), v_heads[hh], preferred_element_type=F32)

    scores(0, sa_ref)

    def pair(p, carry):
        j = 2 * p
        scores(j + 1, sb_ref)
        accumulate(j, sa_ref, False)
        scores(j + 2, sa_ref)
        accumulate(j + 1, sb_ref, False)
        return carry

    lax.fori_loop(0, qi // 2, pair, 0)

    @pl.when(qi % 2 == 0)
    def _():
        accumulate(qi, sa_ref, True)

    @pl.when(qi % 2 == 1)
    def _():
        scores(qi, sb_ref)
        accumulate(qi - 1, sa_ref, False)
        accumulate(qi, sb_ref, True)

    a0 = acc_ref[0]
    a1 = acc_ref[1]
    o0 = a0 / pltpu.roll(a0, half, 1)
    o1 = a1 / pltpu.roll(a1, half, 1)
    o_ref[...] = jnp.where(first_head, o0, o1).astype(o_ref.dtype)


def _fox_attention(qkv, cum, bsz, seq):
    t = qkv.shape[0]
    d = qkv.shape[1] // 3
    pairs = d // LANES
    tq = ATTN_TILE
    per_seq = seq // tq
    scale = (d // FOX_HEADS) ** -0.5
    ck = cum.reshape(bsz, pairs, 2, seq)
    return pl.pallas_call(
        functools.partial(_fox_attn_kernel, scale=scale),
        out_shape=jax.ShapeDtypeStruct((t, d), BF16),
        grid=(bsz, pairs, per_seq),
        in_specs=[
            pl.BlockSpec((tq, LANES), lambda b, hp, qi: (b * per_seq + qi, hp)),
            pl.BlockSpec((seq, LANES), lambda b, hp, qi: (b, pairs + hp)),
            pl.BlockSpec((seq, LANES), lambda b, hp, qi: (b, 2 * pairs + hp)),
            pl.BlockSpec((None, None, 2, seq), lambda b, hp, qi: (b, hp, 0, 0)),
        ],
        out_specs=pl.BlockSpec((tq, LANES), lambda b, hp, qi: (b * per_seq + qi, hp)),
        scratch_shapes=[
            pltpu.VMEM((2, tq, LANES), F32),
            pltpu.VMEM((2, tq, LANES), F32),
            pltpu.VMEM((2, tq, tq), F32),
            pltpu.VMEM((2, tq, tq), F32),
        ],
        compiler_params=_params("parallel", "parallel", "arbitrary"),
        name="forgetting_attention",
    )(qkv, qkv, qkv, ck)


def _router_kernel(x_ref, g_ref, sh_ref, sc_ref, rw_ref, rb_ref,
                   h_ref, e_ref, gt_ref, rk_ref, cnt_ref, carry_ref):
    @pl.when(pl.program_id(0) == 0)
    def _():
        carry_ref[...] = jnp.zeros_like(carry_ref)

    h = _norm_mod(x_ref[...], g_ref[...], sh_ref[...], sc_ref[...])
    h_ref[...] = h
    logits = jnp.dot(h, rw_ref[...], precision=HIGHEST, preferred_element_type=F32) + rb_ref[...]
    tm = logits.shape[0]
    lane = lax.broadcasted_iota(I32, logits.shape, 1).astype(F32)

    work = logits
    picked = jnp.zeros(logits.shape, F32)
    vals, idxs = [], []
    for _ in range(TOP_K):
        mx = jnp.max(work, axis=-1, keepdims=True)
        idx = jnp.min(jnp.where(work == mx, lane, float(LANES)), axis=-1, keepdims=True)
        sel = lane == idx
        vals.append(mx)
        idxs.append(idx)
        picked = jnp.where(sel, 1.0, picked)
        work = jnp.where(sel, -3.0e38, work)

    ex = [jnp.exp(v - vals[0]) for v in vals]
    den = ex[0] + ex[1] + ex[2] + ex[3]

    r_i = lax.broadcasted_iota(I32, (tm, tm), 0)
    c_i = lax.broadcasted_iota(I32, (tm, tm), 1)
    earlier = jnp.where(c_i < r_i, 1.0, 0.0).astype(BF16)
    before = jnp.dot(earlier, picked.astype(BF16), preferred_element_type=F32) + carry_ref[...]
    carry_ref[...] = carry_ref[...] + jnp.sum(picked, axis=0, keepdims=True)
    cnt_ref[...] = carry_ref[...]

    e_out = jnp.zeros(logits.shape, F32)
    g_out = jnp.zeros(logits.shape, F32)
    r_out = jnp.zeros(logits.shape, F32)
    for k in range(TOP_K):
        rank = jnp.sum(jnp.where(lane == idxs[k], before, 0.0), axis=-1, keepdims=True)
        at_k = lane == float(k)
        e_out = jnp.where(at_k, idxs[k], e_out)
        g_out = jnp.where(at_k, ex[k] / den, g_out)
        r_out = jnp.where(at_k, rank, r_out)
    e_ref[...] = e_out.astype(I32)
    gt_ref[...] = g_out
    rk_ref[...] = r_out.astype(I32)


def _router(x, g, shift, scale, rw, rb, seq):
    t, d = x.shape
    tm = ROW_TILE
    per_seq = seq // tm
    slab = lambda dt: jax.ShapeDtypeStruct((t, LANES), dt)
    slab_spec = pl.BlockSpec((tm, LANES), lambda i: (i, 0))
    return pl.pallas_call(
        _router_kernel,
        out_shape=[jax.ShapeDtypeStruct((t, d), F32), slab(I32), slab(F32), slab(I32),
                   jax.ShapeDtypeStruct((1, LANES), F32)],
        grid=(t // tm,),
        in_specs=[
            pl.BlockSpec((tm, d), lambda i: (i, 0)),
            pl.BlockSpec((1, d), lambda i: (0, 0)),
            pl.BlockSpec((None, 1, d), lambda i: (i // per_seq, 0, 0)),
            pl.BlockSpec((None, 1, d), lambda i: (i // per_seq, 0, 0)),
            pl.BlockSpec((d, LANES), lambda i: (0, 0)),
            pl.BlockSpec((1, LANES), lambda i: (0, 0)),
        ],
        out_specs=[pl.BlockSpec((tm, d), lambda i: (i, 0)), slab_spec, slab_spec, slab_spec,
                   pl.BlockSpec((1, LANES), lambda i: (0, 0))],
        scratch_shapes=[pltpu.VMEM((1, LANES), F32)],
        compiler_params=_params("arbitrary"),
        name="moe_router",
    )(x, g, shift, scale, rw, rb)


def _dispatch_kernel(dest_ref, pad_end_ref, padded_ref, h_ref, xs_hbm, zero_ref, zsem, sem, *, td):
    i = pl.program_id(0)
    tile = zero_ref.shape[0]
    n_tiles = xs_hbm.shape[0] // tile
    n_exp = pad_end_ref.shape[0]
    n_used = pad_end_ref[n_exp - 1] // tile

    def zero_tile(tile_idx):
        start = pl.multiple_of(tile_idx * tile, tile)
        return pltpu.make_async_copy(zero_ref, xs_hbm.at[pl.ds(start, tile)], zsem)

    def for_zeroed_tiles(fn):
        def per_expert(e, carry):
            @pl.when(padded_ref[e] > 0)
            def _():
                fn(zero_tile(pad_end_ref[e] // tile - 1))
            return carry
        lax.fori_loop(0, n_exp, per_expert, 0)

        def per_tail(tile_idx, carry):
            fn(zero_tile(tile_idx))
            return carry
        lax.fori_loop(n_used, n_tiles, per_tail, 0)

    @pl.when(i == 0)
    def _():
        zero_ref[...] = jnp.zeros_like(zero_ref)
        for_zeroed_tiles(lambda cp: cp.start())
        for_zeroed_tiles(lambda cp: cp.wait())

    def issue(jo, carry):
        for u in range(SUBLANES):
            j = jo * SUBLANES + u
            for k in range(TOP_K):
                dst = dest_ref[(i * td + j) * TOP_K + k]
                pltpu.make_async_copy(h_ref.at[jo, pl.ds(u, 1)], xs_hbm.at[pl.ds(dst, 1)],
                                      sem).start(priority=k % 2)
        return carry

    lax.fori_loop(0, td // SUBLANES, issue, 0)
    n_copied = td * TOP_K
    pltpu.make_async_copy(xs_hbm.at[pl.ds(0, n_copied)], xs_hbm.at[pl.ds(0, n_copied)], sem).wait()


def _dispatch(dest, pad_end, padded, h, n_rows):
    t, d = h.shape
    td = DISPATCH_TILE
    grid_spec = pltpu.PrefetchScalarGridSpec(
        num_scalar_prefetch=3,
        grid=(t // td,),
        in_specs=[pl.BlockSpec((td // SUBLANES, SUBLANES, d), lambda i, de, pe, pa: (i, 0, 0))],
        out_specs=pl.BlockSpec(memory_space=pl.ANY),
        scratch_shapes=[pltpu.VMEM((EXPERT_TILE, d), F32), pltpu.SemaphoreType.DMA, pltpu.SemaphoreType.DMA],
    )
    return pl.pallas_call(
        functools.partial(_dispatch_kernel, td=td),
        out_shape=jax.ShapeDtypeStruct((n_rows, d), F32),
        grid_spec=grid_spec,
        compiler_params=_params("arbitrary"),
        name="moe_dispatch",
    )(dest, pad_end, padded, h.reshape(t // SUBLANES, SUBLANES, d))


def _expert_kernel(te_ref, nu_ref, xs_ref, wgu_ref, bgu_ref, wd_ref, bd_ref, y_ref, wgu_bf, wd_bf):
    i = pl.program_id(0)

    @pl.when(i >= nu_ref[0])
    def _():
        y_ref[...] = jnp.zeros_like(y_ref)

    @pl.when(jnp.logical_or(i == 0, te_ref[i] != te_ref[jnp.maximum(i - 1, 0)]))
    def _():
        wgu_bf[...] = wgu_ref[...].astype(BF16)
        wd_bf[...] = wd_ref[...].astype(BF16)

    @pl.when(i < nu_ref[0])
    def _():
        f = wd_ref.shape[0]
        gu = jnp.dot(xs_ref[...].astype(BF16), wgu_bf[...], preferred_element_type=F32) + bgu_ref[...]
        gate = jnp.minimum(gu[:, :f], SWIGLU_LIMIT)
        up = jnp.clip(gu[:, f:], -SWIGLU_LIMIT, SWIGLU_LIMIT)
        act = gate * _sigmoid(SWIGLU_ALPHA * gate) * (up + 1.0)
        y_ref[...] = jnp.dot(act.astype(BF16), wd_bf[...], preferred_element_type=F32) + bd_ref[...]


def _expert_mlp(tile_e, n_used, xs, layer, wgu, bgu, wd, bd):
    n_rows, d = xs.shape
    tile = EXPERT_TILE
    f = wd.shape[2]
    row_map = lambda i, te, nu: (jnp.minimum(i, nu[0] - 1), 0)
    exp_map = lambda i, te, nu: (layer, te[i], 0, 0)
    grid_spec = pltpu.PrefetchScalarGridSpec(
        num_scalar_prefetch=2,
        grid=(n_rows // tile,),
        in_specs=[
            pl.BlockSpec((tile, d), row_map),
            pl.BlockSpec((None, None, d, 2 * f), exp_map),
            pl.BlockSpec((None, None, 1, 2 * f), exp_map),
            pl.BlockSpec((None, None, f, d), exp_map),
            pl.BlockSpec((None, None, 1, d), exp_map),
        ],
        out_specs=pl.BlockSpec((tile, d), lambda i, te, nu: (i, 0)),
        scratch_shapes=[pltpu.VMEM((d, 2 * f), BF16), pltpu.VMEM((f, d), BF16)],
    )
    return pl.pallas_call(
        _expert_kernel,
        out_shape=jax.ShapeDtypeStruct((n_rows, d), F32),
        grid_spec=grid_spec,
        compiler_params=_params("arbitrary"),
        name="moe_expert_mlp",
    )(tile_e, n_used, xs, wgu, bgu[:, :, None, :], wd, bd[:, :, None, :])


def _combine_kernel(dest_ref, x_ref, gt_ref, gate_ref, yb_hbm, o_ref, buf_ref, sem, *, tc):
    i = pl.program_id(0)
    n = pl.num_programs(0)

    def issue(step, slot):
        def body(jo, carry):
            for u in range(SUBLANES):
                tok = step * tc + jo * SUBLANES + u
                for k in range(TOP_K):
                    pltpu.make_async_copy(yb_hbm.at[pl.ds(dest_ref[tok * TOP_K + k], 1)],
                                          buf_ref.at[slot, k, jo, pl.ds(u, 1)],
                                          sem.at[slot]).start(priority=k % 2)
            return carry
        lax.fori_loop(0, tc // SUBLANES, body, 0)

    @pl.when(i == 0)
    def _():
        issue(0, 0)

    @pl.when(i + 1 < n)
    def _():
        issue(i + 1, (i + 1) % 2)

    slot = i % 2

    for k in range(TOP_K):
        plane = buf_ref.at[slot, k]
        pltpu.make_async_copy(plane, plane, sem.at[slot]).wait()

    gates = gt_ref[...]
    d = x_ref.shape[1]
    y = gates[:, 0:1] * buf_ref[slot, 0].reshape(tc, d)
    for k in range(1, TOP_K):
        y = y + gates[:, k:k + 1] * buf_ref[slot, k].reshape(tc, d)
    o_ref[...] = x_ref[...] + gate_ref[...] * y


def _combine(dest, x, gates, gate2, yb, seq):
    t, d = x.shape
    tc = COMBINE_TILE
    per_seq = seq // tc
    grid_spec = pltpu.PrefetchScalarGridSpec(
        num_scalar_prefetch=1,
        grid=(t // tc,),
        in_specs=[
            pl.BlockSpec((tc, d), lambda i, dr: (i, 0)),
            pl.BlockSpec((tc, LANES), lambda i, dr: (i, 0)),
            pl.BlockSpec((None, 1, d), lambda i, dr: (i // per_seq, 0, 0)),
            pl.BlockSpec(memory_space=pl.ANY),
        ],
        out_specs=pl.BlockSpec((tc, d), lambda i, dr: (i, 0)),
        scratch_shapes=[pltpu.VMEM((2, TOP_K, tc // SUBLANES, SUBLANES, d), F32), pltpu.SemaphoreType.DMA((2,))],
    )
    return pl.pallas_call(
        functools.partial(_combine_kernel, tc=tc),
        out_shape=jax.ShapeDtypeStruct((t, d), F32),
        grid_spec=grid_spec,
        compiler_params=_params("arbitrary"),
        name="moe_combine_residual",
    )(dest, x, gates, gate2, yb)


def _moe_layer(x, g, shift, scale, gate2, rw, rb, layer, wgu, bgu, wd, bd, seq):
    t, d = x.shape
    n_exp = rw.shape[1]
    rw_pad = jnp.zeros((d, LANES), F32).at[:, :n_exp].set(rw)
    rb_pad = jnp.full((1, LANES), NEG_BIG, F32).at[0, :n_exp].set(rb)
    h, e_slab, gt_slab, rk_slab, cnt = _router(x, g, shift, scale, rw_pad, rb_pad, seq)

    tile = EXPERT_TILE
    n_tiles = t * TOP_K // tile + n_exp
    counts = cnt[0, :n_exp].astype(I32)
    padded = (counts + tile - 1) // tile * tile
    pad_end = jnp.cumsum(padded)
    pad_start = pad_end - padded
    dest = (pad_start[e_slab[:, :TOP_K]] + rk_slab[:, :TOP_K]).reshape(-1)
    n_used = pad_end[-1] // tile
    tile_ids = jnp.arange(n_tiles, dtype=I32)
    first_row = jnp.minimum(tile_ids, n_used - 1) * tile
    tile_e = jnp.sum((pad_end[None, :] <= first_row[:, None]).astype(I32), axis=1)

    xs = _dispatch(dest, pad_end, padded, h, n_tiles * tile)
    yb = _expert_mlp(tile_e, n_used.reshape(1), xs, layer, wgu, bgu, wd, bd)
    return _combine(dest, x, gt_slab, gate2, yb, seq)


def _final_norm_kernel(x_ref, g_ref, o_ref):
    x = x_ref[...]
    ms = jnp.mean(x * x, axis=-1, keepdims=True)
    o_ref[...] = x * lax.rsqrt(ms + RMS_EPS) * g_ref[...]


def _final_norm(x, g):
    t, d = x.shape
    tm = ROW_TILE
    return pl.pallas_call(
        _final_norm_kernel,
        out_shape=jax.ShapeDtypeStruct((t, d), F32),
        grid=(t // tm,),
        in_specs=[pl.BlockSpec((tm, d), lambda i: (i, 0)), pl.BlockSpec((1, d), lambda i: (0, 0))],
        out_specs=pl.BlockSpec((tm, d), lambda i: (i, 0)),
        compiler_params=_params("parallel"),
        name="final_rmsnorm",
    )(x, g)


def kernel(x, c, norm_mix_g, norm_ffn_g, w_mod, b_mod, conv_w_in, conv_w, conv_w_out, lru_w_in, lru_conv_w, lru_conv_b, lru_gate_w, lru_gate_b, lru_a_param, lru_w_out, fox_w_in, fox_b_f, fox_w_out, router_w, router_b, moe_w_gate_up, moe_b_gate_up, moe_w_down, moe_b_down, final_g):
    bsz, seq, d = x.shape
    depth = w_mod.shape[0]
    t = bsz * seq
    mod = _modulation(c, w_mod, b_mod)
    xt = x.reshape(t, d)
    for i in range(depth):
        shift1, scale1, gate1 = mod[i, :, 0], mod[i, :, 1], mod[i, :, 2]
        shift2, scale2, gate2 = mod[i, :, 3], mod[i, :, 4], mod[i, :, 5]
        g_mix = norm_mix_g[i][None, :]
        kind, j = i % 3, i // 3
        if kind == 0:
            bcv = _norm_matmul(xt, g_mix, shift1, scale1, conv_w_in[j].astype(BF16), 3 * d, F32, seq)[0]
            xt = _conv_mixer(bcv, conv_w[j], conv_w_out[j].astype(BF16), xt, gate1, seq)
        elif kind == 1:
            gx = _norm_matmul(xt, g_mix, shift1, scale1, lru_w_in[j].astype(BF16), lru_w_in.shape[2], F32, seq)[0]
            y = _lru_core(gx, lru_conv_w[j], lru_conv_b[j][None, :], lru_gate_w[j].astype(BF16),
                          lru_gate_b[j][:, None, :], lru_a_param[j][None, :], bsz, seq)
            xt = _proj_residual(y, lru_w_out[j].astype(BF16), xt, gate1, seq)
        else:
            w_in = jnp.zeros((d, 3 * d + LANES), F32).at[:, :3 * d + FOX_HEADS].set(fox_w_in[j]).astype(BF16)
            qkv, f = _norm_matmul(xt, g_mix, shift1, scale1, w_in, 3 * d, BF16, seq)
            b_f = jnp.zeros((1, LANES), F32).at[0, :FOX_HEADS].set(fox_b_f[j])
            cum = _forget_cumsum(f, b_f, bsz, seq)
            o = _fox_attention(qkv, cum, bsz, seq)
            xt = _proj_residual(o, fox_w_out[j].astype(BF16), xt, gate1, seq)
        xt = _moe_layer(xt, norm_ffn_g[i][None, :], shift2, scale2, gate2, router_w[i], router_b[i],
                        i, moe_w_gate_up, moe_b_gate_up, moe_w_down, moe_b_down, seq)
    return _final_norm(xt, final_g[None, :]).reshape(bsz, seq, d)
```

```python
import functools

import jax
import jax.numpy as jnp
from jax import lax
from jax.experimental import pallas as pl
from jax.experimental.pallas import tpu as pltpu

F32 = jnp.float32
BF16 = jnp.bfloat16
I32 = jnp.int32
HIGHEST = lax.Precision.HIGHEST

RMS_EPS = 1e-6
LRU_HEADS = 4
LRU_C = 8.0
FOX_HEADS = 16
N_EXPERTS = 32
TOP_K = 4
SWIGLU_LIMIT = 7.0
SWIGLU_ALPHA = 1.702
NEG_BIG = -1e30
GELU_C = 0.7978845608028654
LOG2E = 1.4426950408889634

LANES = 128
SUBLANES = 8
VMEM_LIMIT = 56 * 1024 * 1024

ROW_TILE = 512
EXPERT_TILE = 256
DISPATCH_TILE = 512
COMBINE_TILE = 128
ATTN_TILE = 512


def _params(*sem):
    return pltpu.CompilerParams(dimension_semantics=sem, vmem_limit_bytes=VMEM_LIMIT)


def _sigmoid(z):
    return 1.0 / (1.0 + jnp.exp(-z))


def _norm_mod(x, g, shift, scale):
    ms = jnp.mean(x * x, axis=-1, keepdims=True)
    y = x * lax.rsqrt(ms + RMS_EPS) * g
    return y * (1.0 + scale) + shift


def _shifted_rows(u, carry_ref, j, row):
    sh = pltpu.roll(u, j, 0)
    for r in range(j):
        src = SUBLANES - j + r
        sh = jnp.where(row == r, carry_ref[src:src + 1, :], sh)
    return sh


def _mod_kernel(c_ref, w_ref, b_ref, o_ref):
    c = c_ref[...]
    ca = c * _sigmoid(c)
    o_ref[0] = jnp.dot(ca, w_ref[0], precision=HIGHEST, preferred_element_type=F32) + b_ref[0]


def _modulation(c, w_mod, b_mod):
    depth, d, n = w_mod.shape
    bsz = c.shape[0]
    rows = -(-bsz // SUBLANES) * SUBLANES
    c_pad = jnp.zeros((rows, d), F32).at[:bsz].set(c)
    tn = n // 4
    out = pl.pallas_call(
        _mod_kernel,
        out_shape=jax.ShapeDtypeStruct((depth, rows, n), F32),
        grid=(depth, n // tn),
        in_specs=[
            pl.BlockSpec((rows, d), lambda l, j: (0, 0)),
            pl.BlockSpec((1, d, tn), lambda l, j: (l, 0, j)),
            pl.BlockSpec((1, 1, tn), lambda l, j: (l, 0, j)),
        ],
        out_specs=pl.BlockSpec((1, rows, tn), lambda l, j: (l, 0, j)),
        compiler_params=_params("parallel", "parallel"),
        name="adaln_modulation",
    )(c_pad, w_mod, b_mod.reshape(depth, 1, n))
    return out[:, :bsz].reshape(depth, bsz, 6, 1, d)


def _norm_matmul_kernel(x_ref, g_ref, sh_ref, sc_ref, w_ref, *out_refs, n_main):
    h = _norm_mod(x_ref[...], g_ref[...], sh_ref[...], sc_ref[...]).astype(BF16)
    y = jnp.dot(h, w_ref[...], preferred_element_type=F32)
    out_refs[0][...] = y[:, :n_main].astype(out_refs[0].dtype)
    if len(out_refs) > 1:
        out_refs[1][...] = y[:, n_main:]


def _norm_matmul(x, g, shift, scale, w, n_main, out_dtype, seq):
    t, d = x.shape
    n = w.shape[1]
    tm = ROW_TILE
    per_seq = seq // tm
    out_shape = [jax.ShapeDtypeStruct((t, n_main), out_dtype)]
    out_specs = [pl.BlockSpec((tm, n_main), lambda i: (i, 0))]
    if n > n_main:
        out_shape.append(jax.ShapeDtypeStruct((t, n - n_main), F32))
        out_specs.append(pl.BlockSpec((tm, n - n_main), lambda i: (i, 0)))
    return pl.pallas_call(
        functools.partial(_norm_matmul_kernel, n_main=n_main),
        out_shape=out_shape,
        grid=(t // tm,),
        in_specs=[
            pl.BlockSpec((tm, d), lambda i: (i, 0)),
            pl.BlockSpec((1, d), lambda i: (0, 0)),
            pl.BlockSpec((None, 1, d), lambda i: (i // per_seq, 0, 0)),
            pl.BlockSpec((None, 1, d), lambda i: (i // per_seq, 0, 0)),
            pl.BlockSpec((d, n), lambda i: (0, 0)),
        ],
        out_specs=out_specs,
        compiler_params=_params("parallel"),
        name="norm_mod_in_proj",
    )(x, g, shift, scale, w)


def _proj_residual_kernel(z_ref, w_ref, x_ref, gate_ref, o_ref):
    y = jnp.dot(z_ref[...].astype(BF16), w_ref[...], preferred_element_type=F32)
    o_ref[...] = x_ref[...] + gate_ref[...] * y


def _proj_residual(z, w, x, gate, seq):
    t, d = x.shape
    tm = ROW_TILE
    per_seq = seq // tm
    return pl.pallas_call(
        _proj_residual_kernel,
        out_shape=jax.ShapeDtypeStruct((t, d), F32),
        grid=(t // tm,),
        in_specs=[
            pl.BlockSpec((tm, z.shape[1]), lambda i: (i, 0)),
            pl.BlockSpec(w.shape, lambda i: (0, 0)),
            pl.BlockSpec((tm, d), lambda i: (i, 0)),
            pl.BlockSpec((None, 1, d), lambda i: (i // per_seq, 0, 0)),
        ],
        out_specs=pl.BlockSpec((tm, d), lambda i: (i, 0)),
        compiler_params=_params("parallel"),
        name="out_proj_residual",
    )(z, w, x, gate)


def _conv_mixer_kernel(b_ref, c_ref, v_ref, cw_ref, w_ref, x_ref, gate_ref, o_ref, carry_ref, *, per_seq):
    @pl.when(pl.program_id(0) % per_seq == 0)
    def _():
        carry_ref[...] = jnp.zeros_like(carry_ref)

    cv = c_ref[...].astype(F32) * v_ref[...].astype(F32)
    tm = cv.shape[0]
    k_w = cw_ref.shape[0]
    row = lax.broadcasted_iota(I32, cv.shape, 0)
    conv = cw_ref[k_w - 1:k_w, :] * cv
    for j in range(1, k_w):
        conv = conv + cw_ref[k_w - 1 - j:k_w - j, :] * _shifted_rows(cv, carry_ref, j, row)
    carry_ref[...] = cv[tm - SUBLANES:, :]
    z = (b_ref[...].astype(F32) * conv).astype(BF16)
    y = jnp.dot(z, w_ref[...], preferred_element_type=F32)
    o_ref[...] = x_ref[...] + gate_ref[...] * y


def _conv_mixer(bcv, conv_w, w_out, x, gate, seq):
    t, d = x.shape
    tm = ROW_TILE
    per_seq = seq // tm
    return pl.pallas_call(
        functools.partial(_conv_mixer_kernel, per_seq=per_seq),
        out_shape=jax.ShapeDtypeStruct((t, d), F32),
        grid=(t // tm,),
        in_specs=[
            pl.BlockSpec((tm, d), lambda i: (i, 0)),
            pl.BlockSpec((tm, d), lambda i: (i, 1)),
            pl.BlockSpec((tm, d), lambda i: (i, 2)),
            pl.BlockSpec(conv_w.shape, lambda i: (0, 0)),
            pl.BlockSpec(w_out.shape, lambda i: (0, 0)),
            pl.BlockSpec((tm, d), lambda i: (i, 0)),
            pl.BlockSpec((None, 1, d), lambda i: (i // per_seq, 0, 0)),
        ],
        out_specs=pl.BlockSpec((tm, d), lambda i: (i, 0)),
        scratch_shapes=[pltpu.VMEM((SUBLANES, d), F32)],
        compiler_params=_params("arbitrary"),
        name="short_conv_mixer",
    )(bcv, bcv, bcv, conv_w, w_out, x, gate)


def _lru_kernel(xb_ref, gb_ref, cw_ref, cb_ref, gw_ref, gbias_ref, ap_ref, y_ref, xcarry_ref, hcarry_ref):
    @pl.when(pl.program_id(2) == 0)
    def _():
        xcarry_ref[...] = jnp.zeros_like(xcarry_ref)
        hcarry_ref[...] = jnp.zeros_like(hcarry_ref)

    x = xb_ref[...].astype(F32)
    ts, blk = x.shape
    k_w = cw_ref.shape[0]
    row = lax.broadcasted_iota(I32, x.shape, 0)
    conv = cw_ref[k_w - 1:k_w, :] * x
    for j in range(1, k_w):
        conv = conv + cw_ref[k_w - 1 - j:k_w - j, :] * _shifted_rows(x, xcarry_ref, j, row)
    xcarry_ref[...] = x[ts - SUBLANES:, :]
    xc = conv + cb_ref[...]

    g = jnp.dot(xc.astype(BF16), gw_ref[...], preferred_element_type=F32) + gbias_ref[...]
    r_gate = _sigmoid(g[:, :blk])
    i_gate = _sigmoid(g[:, blk:])
    ap = ap_ref[...]
    softplus = jnp.maximum(ap, 0.0) + jnp.log(1.0 + jnp.exp(-jnp.abs(ap)))
    a = jnp.exp((-LRU_C) * r_gate * softplus)
    u = jnp.sqrt(1.0 - a * a) * (i_gate * xc)

    a_cum, h_loc = a, u
    in_group = jnp.bitwise_and(row, SUBLANES - 1)
    step = 1
    while step < SUBLANES:
        a_prev = jnp.where(in_group < step, 1.0, pltpu.roll(a_cum, step, 0))
        h_prev = jnp.where(in_group < step, 0.0, pltpu.roll(h_loc, step, 0))
        h_loc = h_loc + a_cum * h_prev
        a_cum = a_cum * a_prev
        step *= 2
    carry = hcarry_ref[...]
    groups = []
    for g in range(ts // SUBLANES):
        lo = g * SUBLANES
        h_group = h_loc[lo:lo + SUBLANES, :] + a_cum[lo:lo + SUBLANES, :] * carry
        carry = h_group[SUBLANES - 1:SUBLANES, :]
        groups.append(h_group)
    hs = jnp.concatenate(groups, axis=0)
    hcarry_ref[...] = carry

    gb = gb_ref[...].astype(F32)
    gelu = 0.5 * gb * (1.0 + jnp.tanh(GELU_C * (gb + 0.044715 * (gb * gb * gb))))
    y_ref[...] = (gelu * hs).astype(y_ref.dtype)


def _lru_core(gx, conv_w, conv_b, gate_w, gate_b, a_param, bsz, seq):
    t = gx.shape[0]
    width = gx.shape[1] // 2
    blk = width // LRU_HEADS
    ts = ROW_TILE
    per_seq = seq // ts
    return pl.pallas_call(
        _lru_kernel,
        out_shape=jax.ShapeDtypeStruct((t, width), BF16),
        grid=(bsz, LRU_HEADS, per_seq),
        in_specs=[
            pl.BlockSpec((ts, blk), lambda b, h, s: (b * per_seq + s, LRU_HEADS + h)),
            pl.BlockSpec((ts, blk), lambda b, h, s: (b * per_seq + s, h)),
            pl.BlockSpec((conv_w.shape[0], blk), lambda b, h, s: (0, h)),
            pl.BlockSpec((1, blk), lambda b, h, s: (0, h)),
            pl.BlockSpec((None, blk, 2 * blk), lambda b, h, s: (h, 0, 0)),
            pl.BlockSpec((None, 1, 2 * blk), lambda b, h, s: (h, 0, 0)),
            pl.BlockSpec((1, blk), lambda b, h, s: (0, h)),
        ],
        out_specs=pl.BlockSpec((ts, blk), lambda b, h, s: (b * per_seq + s, h)),
        scratch_shapes=[pltpu.VMEM((SUBLANES, blk), F32), pltpu.VMEM((1, blk), F32)],
        compiler_params=_params("parallel", "parallel", "arbitrary"),
        name="rglru_scan",
    )(gx, gx, conv_w, conv_b, gate_w, gate_b, a_param)


def _forget_cumsum_kernel(f_ref, bf_ref, o_ref):
    x = f_ref[...] + bf_ref[...]
    c = jnp.minimum(x, 0.0) - jnp.log(1.0 + jnp.exp(-jnp.abs(x)))
    seq = c.shape[0]
    row = lax.broadcasted_iota(I32, c.shape, 0)
    step = 1
    while step < seq:
        c = c + jnp.where(row < step, 0.0, pltpu.roll(c, step, 0))
        step *= 2
    o_ref[...] = c.T[:FOX_HEADS, :]


def _forget_cumsum(f, b_f, bsz, seq):
    return pl.pallas_call(
        _forget_cumsum_kernel,
        out_shape=jax.ShapeDtypeStruct((bsz, FOX_HEADS, seq), F32),
        grid=(bsz,),
        in_specs=[
            pl.BlockSpec((seq, LANES), lambda b: (b, 0)),
            pl.BlockSpec((1, LANES), lambda b: (0, 0)),
        ],
        out_specs=pl.BlockSpec((None, FOX_HEADS, seq), lambda b: (b, 0, 0)),
        compiler_params=_params("parallel"),
        name="forget_gate_cumsum",
    )(f, b_f)


def _fox_attn_kernel(q_ref, k_ref, v_ref, ck_ref, o_ref, m_ref, acc_ref, sa_ref, sb_ref, *, scale):
    qi = pl.program_id(2)
    tq = q_ref.shape[0]
    tk = tq
    half = q_ref.shape[1] // 2
    first_head = lax.broadcasted_iota(I32, (1, q_ref.shape[1]), 1) < half

    qs = (q_ref[...].astype(F32) * (scale * LOG2E)).astype(BF16)
    zero = jnp.zeros_like(qs)
    q_heads = (jnp.where(first_head, qs, zero), jnp.where(first_head, zero, qs))
    m_ref[...] = jnp.full_like(m_ref, NEG_BIG)
    acc_ref[...] = jnp.zeros_like(acc_ref)

    def scores(j, s_ref):
        start = pl.multiple_of(j * tk, tk)
        k = k_ref[pl.ds(start, tk), :]
        ck = ck_ref[:, pl.ds(start, tk)] * LOG2E
        for hh in range(2):
            s = lax.dot_general(q_heads[hh], k, (((1,), (1,)), ((), ())), preferred_element_type=F32)
            s_ref[hh] = s - ck[hh:hh + 1, :]

    def accumulate(j, s_ref, diagonal):
        start = pl.multiple_of(j * tk, tk)
        v = v_ref[pl.ds(start, tk), :]
        ones = jnp.ones_like(v)
        v_heads = (jnp.where(first_head, v, ones), jnp.where(first_head, ones, v))
        for hh in range(2):
            s = s_ref[hh]
            if diagonal:
                row = lax.broadcasted_iota(I32, s.shape, 0)
                col = lax.broadcasted_iota(I32, s.shape, 1)
                s = jnp.where(col <= row, s, NEG_BIG)
            m_prev = m_ref[hh]
            m_new = jnp.maximum(m_prev, jnp.max(s, axis=-1, keepdims=True))
            alpha = jnp.exp2(m_prev - m_new)
            p = jnp.exp2(s - jnp.concatenate([m_new] * (tk // LANES), axis=1))
            m_ref[hh] = m_new
            acc_ref[hh] = alpha * acc_ref[hh] + jnp.dot(p.astype(BF16), v_heads[hh], preferred_element_type=F32)

    scores(0, sa_ref)

    def pair(p, carry):
        j = 2 * p
        scores(j + 1, sb_ref)
        accumulate(j, sa_ref, False)
        scores(j + 2, sa_ref)
        accumulate(j + 1, sb_ref, False)
        return carry

    lax.fori_loop(0, qi // 2, pair, 0)

    @pl.when(qi % 2 == 0)
    def _():
        accumulate(qi, sa_ref, True)

    @pl.when(qi % 2 == 1)
    def _():
        scores(qi, sb_ref)
        accumulate(qi - 1, sa_ref, False)
        accumulate(qi, sb_ref, True)

    a0 = acc_ref[0]
    a1 = acc_ref[1]
    o0 = a0 / pltpu.roll(a0, half, 1)
    o1 = a1 / pltpu.roll(a1, half, 1)
    o_ref[...] = jnp.where(first_head, o0, o1).astype(o_ref.dtype)


def _fox_attention(qkv, cum, bsz, seq):
    t = qkv.shape[0]
    d = qkv.shape[1] // 3
    pairs = d // LANES
    tq = ATTN_TILE
    per_seq = seq // tq
    scale = (d // FOX_HEADS) ** -0.5
    ck = cum.reshape(bsz, pairs, 2, seq)
    return pl.pallas_call(
        functools.partial(_fox_attn_kernel, scale=scale),
        out_shape=jax.ShapeDtypeStruct((t, d), BF16),
        grid=(bsz, pairs, per_seq),
        in_specs=[
            pl.BlockSpec((tq, LANES), lambda b, hp, qi: (b * per_seq + qi, hp)),
            pl.BlockSpec((seq, LANES), lambda b, hp, qi: (b, pairs + hp)),
            pl.BlockSpec((seq, LANES), lambda b, hp, qi: (b, 2 * pairs + hp)),
            pl.BlockSpec((None, None, 2, seq), lambda b, hp, qi: (b, hp, 0, 0)),
        ],
        out_specs=pl.BlockSpec((tq, LANES), lambda b, hp, qi: (b * per_seq + qi, hp)),
        scratch_shapes=[
            pltpu.VMEM((2, tq, LANES), F32),
            pltpu.VMEM((2, tq, LANES), F32),
            pltpu.VMEM((2, tq, tq), F32),
            pltpu.VMEM((2, tq, tq), F32),
        ],
        compiler_params=_params("parallel", "parallel", "arbitrary"),
        name="forgetting_attention",
    )(qkv, qkv, qkv, ck)


def _router_kernel(x_ref, g_ref, sh_ref, sc_ref, rw_ref, rb_ref,
                   h_ref, e_ref, gt_ref, rk_ref, cnt_ref, carry_ref):
    @pl.when(pl.program_id(0) == 0)
    def _():
        carry_ref[...] = jnp.zeros_like(carry_ref)

    h = _norm_mod(x_ref[...], g_ref[...], sh_ref[...], sc_ref[...])
    h_ref[...] = h
    logits = jnp.dot(h, rw_ref[...], precision=HIGHEST, preferred_element_type=F32) + rb_ref[...]
    tm = logits.shape[0]
    lane = lax.broadcasted_iota(I32, logits.shape, 1).astype(F32)

    work = logits
    picked = jnp.zeros(logits.shape, F32)
    vals, idxs = [], []
    for _ in range(TOP_K):
        mx = jnp.max(work, axis=-1, keepdims=True)
        idx = jnp.min(jnp.where(work == mx, lane, float(LANES)), axis=-1, keepdims=True)
        sel = lane == idx
        vals.append(mx)
        idxs.append(idx)
        picked = jnp.where(sel, 1.0, picked)
        work = jnp.where(sel, -3.0e38, work)

    ex = [jnp.exp(v - vals[0]) for v in vals]
    den = ex[0] + ex[1] + ex[2] + ex[3]

    r_i = lax.broadcasted_iota(I32, (tm, tm), 0)
    c_i = lax.broadcasted_iota(I32, (tm, tm), 1)
    earlier = jnp.where(c_i < r_i, 1.0, 0.0).astype(BF16)
    before = jnp.dot(earlier, picked.astype(BF16), preferred_element_type=F32) + carry_ref[...]
    carry_ref[...] = carry_ref[...] + jnp.sum(picked, axis=0, keepdims=True)
    cnt_ref[...] = carry_ref[...]

    e_out = jnp.zeros(logits.shape, F32)
    g_out = jnp.zeros(logits.shape, F32)
    r_out = jnp.zeros(logits.shape, F32)
    for k in range(TOP_K):
        rank = jnp.sum(jnp.where(lane == idxs[k], before, 0.0), axis=-1, keepdims=True)
        at_k = lane == float(k)
        e_out = jnp.where(at_k, idxs[k], e_out)
        g_out = jnp.where(at_k, ex[k] / den, g_out)
        r_out = jnp.where(at_k, rank, r_out)
    e_ref[...] = e_out.astype(I32)
    gt_ref[...] = g_out
    rk_ref[...] = r_out.astype(I32)


def _router(x, g, shift, scale, rw, rb, seq):
    t, d = x.shape
    tm = ROW_TILE
    per_seq = seq // tm
    slab = lambda dt: jax.ShapeDtypeStruct((t, LANES), dt)
    slab_spec = pl.BlockSpec((tm, LANES), lambda i: (i, 0))
    return pl.pallas_call(
        _router_kernel,
        out_shape=[jax.ShapeDtypeStruct((t, d), F32), slab(I32), slab(F32), slab(I32),
                   jax.ShapeDtypeStruct((1, LANES), F32)],
        grid=(t // tm,),
        in_specs=[
            pl.BlockSpec((tm, d), lambda i: (i, 0)),
            pl.BlockSpec((1, d), lambda i: (0, 0)),
            pl.BlockSpec((None, 1, d), lambda i: (i // per_seq, 0, 0)),
            pl.BlockSpec((None, 1, d), lambda i: (i // per_seq, 0, 0)),
            pl.BlockSpec((d, LANES), lambda i: (0, 0)),
            pl.BlockSpec((1, LANES), lambda i: (0, 0)),
        ],
        out_specs=[pl.BlockSpec((tm, d), lambda i: (i, 0)), slab_spec, slab_spec, slab_spec,
                   pl.BlockSpec((1, LANES), lambda i: (0, 0))],
        scratch_shapes=[pltpu.VMEM((1, LANES), F32)],
        compiler_params=_params("arbitrary"),
        name="moe_router",
    )(x, g, shift, scale, rw, rb)


def _dispatch_kernel(dest_ref, pad_end_ref, padded_ref, h_ref, xs_hbm, zero_ref, zsem, sem, *, td):
    i = pl.program_id(0)
    tile = zero_ref.shape[0]
    n_tiles = xs_hbm.shape[0] // tile
    n_exp = pad_end_ref.shape[0]
    n_used = pad_end_ref[n_exp - 1] // tile

    def zero_tile(tile_idx):
        start = pl.multiple_of(tile_idx * tile, tile)
        return pltpu.make_async_copy(zero_ref, xs_hbm.at[pl.ds(start, tile)], zsem)

    def for_zeroed_tiles(fn):
        def per_expert(e, carry):
            @pl.when(padded_ref[e] > 0)
            def _():
                fn(zero_tile(pad_end_ref[e] // tile - 1))
            return carry
        lax.fori_loop(0, n_exp, per_expert, 0)

        def per_tail(tile_idx, carry):
            fn(zero_tile(tile_idx))
            return carry
        lax.fori_loop(n_used, n_tiles, per_tail, 0)

    @pl.when(i == 0)
    def _():
        zero_ref[...] = jnp.zeros_like(zero_ref)
        for_zeroed_tiles(lambda cp: cp.start())
        for_zeroed_tiles(lambda cp: cp.wait())

    def issue(jo, carry):
        for u in range(SUBLANES):
            j = jo * SUBLANES + u
            for k in range(TOP_K):
                dst = dest_ref[(i * td + j) * TOP_K + k]
                pltpu.make_async_copy(h_ref.at[jo, pl.ds(u, 1)], xs_hbm.at[pl.ds(dst, 1)],
                                      sem).start(priority=k % 2)
        return carry

    lax.fori_loop(0, td // SUBLANES, issue, 0)
    n_copied = td * TOP_K
    pltpu.make_async_copy(xs_hbm.at[pl.ds(0, n_copied)], xs_hbm.at[pl.ds(0, n_copied)], sem).wait()


def _dispatch(dest, pad_end, padded, h, n_rows):
    t, d = h.shape
    td = DISPATCH_TILE
    grid_spec = pltpu.PrefetchScalarGridSpec(
        num_scalar_prefetch=3,
        grid=(t // td,),
        in_specs=[pl.BlockSpec((td // SUBLANES, SUBLANES, d), lambda i, de, pe, pa: (i, 0, 0))],
        out_specs=pl.BlockSpec(memory_space=pl.ANY),
        scratch_shapes=[pltpu.VMEM((EXPERT_TILE, d), F32), pltpu.SemaphoreType.DMA, pltpu.SemaphoreType.DMA],
    )
    return pl.pallas_call(
        functools.partial(_dispatch_kernel, td=td),
        out_shape=jax.ShapeDtypeStruct((n_rows, d), F32),
        grid_spec=grid_spec,
        compiler_params=_params("arbitrary"),
        name="moe_dispatch",
    )(dest, pad_end, padded, h.reshape(t // SUBLANES, SUBLANES, d))


def _expert_kernel(te_ref, nu_ref, xs_ref, wgu_ref, bgu_ref, wd_ref, bd_ref, y_ref, wgu_bf, wd_bf):
    i = pl.program_id(0)

    @pl.when(i >= nu_ref[0])
    def _():
        y_ref[...] = jnp.zeros_like(y_ref)

    @pl.when(jnp.logical_or(i == 0, te_ref[i] != te_ref[jnp.maximum(i - 1, 0)]))
    def _():
        wgu_bf[...] = wgu_ref[...].astype(BF16)
        wd_bf[...] = wd_ref[...].astype(BF16)

    @pl.when(i < nu_ref[0])
    def _():
        f = wd_ref.shape[0]
        gu = jnp.dot(xs_ref[...].astype(BF16), wgu_bf[...], preferred_element_type=F32) + bgu_ref[...]
        gate = jnp.minimum(gu[:, :f], SWIGLU_LIMIT)
        up = jnp.clip(gu[:, f:], -SWIGLU_LIMIT, SWIGLU_LIMIT)
        act = gate * _sigmoid(SWIGLU_ALPHA * gate) * (up + 1.0)
        y_ref[...] = jnp.dot(act.astype(BF16), wd_bf[...], preferred_element_type=F32) + bd_ref[...]


def _expert_mlp(tile_e, n_used, xs, layer, wgu, bgu, wd, bd):
    n_rows, d = xs.shape
    tile = EXPERT_TILE
    f = wd.shape[2]
    row_map = lambda i, te, nu: (jnp.minimum(i, nu[0] - 1), 0)
    exp_map = lambda i, te, nu: (layer, te[i], 0, 0)
    grid_spec = pltpu.PrefetchScalarGridSpec(
        num_scalar_prefetch=2,
        grid=(n_rows // tile,),
        in_specs=[
            pl.BlockSpec((tile, d), row_map),
            pl.BlockSpec((None, None, d, 2 * f), exp_map),
            pl.BlockSpec((None, None, 1, 2 * f), exp_map),
            pl.BlockSpec((None, None, f, d), exp_map),
            pl.BlockSpec((None, None, 1, d), exp_map),
        ],
        out_specs=pl.BlockSpec((tile, d), lambda i, te, nu: (i, 0)),
        scratch_shapes=[pltpu.VMEM((d, 2 * f), BF16), pltpu.VMEM((f, d), BF16)],
    )
    return pl.pallas_call(
        _expert_kernel,
        out_shape=jax.ShapeDtypeStruct((n_rows, d), F32),
        grid_spec=grid_spec,
        compiler_params=_params("arbitrary"),
        name="moe_expert_mlp",
    )(tile_e, n_used, xs, wgu, bgu[:, :, None, :], wd, bd[:, :, None, :])


def _combine_kernel(dest_ref, x_ref, gt_ref, gate_ref, yb_hbm, o_ref, buf_ref, sem, *, tc):
    i = pl.program_id(0)
    n = pl.num_programs(0)

    def issue(step, slot):
        def body(jo, carry):
            for u in range(SUBLANES):
                tok = step * tc + jo * SUBLANES + u
                for k in range(TOP_K):
                    pltpu.make_async_copy(yb_hbm.at[pl.ds(dest_ref[tok * TOP_K + k], 1)],
                                          buf_ref.at[slot, k, jo, pl.ds(u, 1)],
                                          sem.at[slot]).start(priority=k % 2)
            return carry
        lax.fori_loop(0, tc // SUBLANES, body, 0)

    @pl.when(i == 0)
    def _():
        issue(0, 0)

    @pl.when(i + 1 < n)
    def _():
        issue(i + 1, (i + 1) % 2)

    slot = i % 2

    for k in range(TOP_K):
        plane = buf_ref.at[slot, k]
        pltpu.make_async_copy(plane, plane, sem.at[slot]).wait()

    gates = gt_ref[...]
    d = x_ref.shape[1]
    y = gates[:, 0:1] * buf_ref[slot, 0].reshape(tc, d)
    for k in range(1, TOP_K):
        y = y + gates[:, k:k + 1] * buf_ref[slot, k].reshape(tc, d)
    o_ref[...] = x_ref[...] + gate_ref[...] * y


def _combine(dest, x, gates, gate2, yb, seq):
    t, d = x.shape
    tc = COMBINE_TILE
    per_seq = seq // tc
    grid_spec = pltpu.PrefetchScalarGridSpec(
        num_scalar_prefetch=1,
        grid=(t // tc,),
        in_specs=[
            pl.BlockSpec((tc, d), lambda i, dr: (i, 0)),
            pl.BlockSpec((tc, LANES), lambda i, dr: (i, 0)),
            pl.BlockSpec((None, 1, d), lambda i, dr: (i // per_seq, 0, 0)),
            pl.BlockSpec(memory_space=pl.ANY),
        ],
        out_specs=pl.BlockSpec((tc, d), lambda i, dr: (i, 0)),
        scratch_shapes=[pltpu.VMEM((2, TOP_K, tc // SUBLANES, SUBLANES, d), F32), pltpu.SemaphoreType.DMA((2,))],
    )
    return pl.pallas_call(
        functools.partial(_combine_kernel, tc=tc),
        out_shape=jax.ShapeDtypeStruct((t, d), F32),
        grid_spec=grid_spec,
        compiler_params=_params("arbitrary"),
        name="moe_combine_residual",
    )(dest, x, gates, gate2, yb)


def _moe_layer(x, g, shift, scale, gate2, rw, rb, layer, wgu, bgu, wd, bd, seq):
    t, d = x.shape
    n_exp = rw.shape[1]
    rw_pad = jnp.zeros((d, LANES), F32).at[:, :n_exp].set(rw)
    rb_pad = jnp.full((1, LANES), NEG_BIG, F32).at[0, :n_exp].set(rb)
    h, e_slab, gt_slab, rk_slab, cnt = _router(x, g, shift, scale, rw_pad, rb_pad, seq)

    tile = EXPERT_TILE
    n_tiles = t * TOP_K // tile + n_exp
    counts = cnt[0, :n_exp].astype(I32)
    padded = (counts + tile - 1) // tile * tile
    pad_end = jnp.cumsum(padded)
    pad_start = pad_end - padded
    dest = (pad_start[e_slab[:, :TOP_K]] + rk_slab[:, :TOP_K]).reshape(-1)
    n_used = pad_end[-1] // tile
    tile_ids = jnp.arange(n_tiles, dtype=I32)
    first_row = jnp.minimum(tile_ids, n_used - 1) * tile
    tile_e = jnp.sum((pad_end[None, :] <= first_row[:, None]).astype(I32), axis=1)

    xs = _dispatch(dest, pad_end, padded, h, n_tiles * tile)
    yb = _expert_mlp(tile_e, n_used.reshape(1), xs, layer, wgu, bgu, wd, bd)
    return _combine(dest, x, gt_slab, gate2, yb, seq)


def _final_norm_kernel(x_ref, g_ref, o_ref):
    x = x_ref[...]
    ms = jnp.mean(x * x, axis=-1, keepdims=True)
    o_ref[...] = x * lax.rsqrt(ms + RMS_EPS) * g_ref[...]


def _final_norm(x, g):
    t, d = x.shape
    tm = ROW_TILE
    return pl.pallas_call(
        _final_norm_kernel,
        out_shape=jax.ShapeDtypeStruct((t, d), F32),
        grid=(t // tm,),
        in_specs=[pl.BlockSpec((tm, d), lambda i: (i, 0)), pl.BlockSpec((1, d), lambda i: (0, 0))],
        out_specs=pl.BlockSpec((tm, d), lambda i: (i, 0)),
        compiler_params=_params("parallel"),
        name="final_rmsnorm",
    )(x, g)


def kernel(x, c, norm_mix_g, norm_ffn_g, w_mod, b_mod, conv_w_in, conv_w, conv_w_out, lru_w_in, lru_conv_w, lru_conv_b, lru_gate_w, lru_gate_b, lru_a_param, lru_w_out, fox_w_in, fox_b_f, fox_w_out, router_w, router_b, moe_w_gate_up, moe_b_gate_up, moe_w_down, moe_b_down, final_g):
    bsz, seq, d = x.shape
    depth = w_mod.shape[0]
    t = bsz * seq
    mod = _modulation(c, w_mod, b_mod)
    xt = x.reshape(t, d)
    for i in range(depth):
        shift1, scale1, gate1 = mod[i, :, 0], mod[i, :, 1], mod[i, :, 2]
        shift2, scale2, gate2 = mod[i, :, 3], mod[i, :, 4], mod[i, :, 5]
        g_mix = norm_mix_g[i][None, :]
        kind, j = i % 3, i // 3
        if kind == 0:
            bcv = _norm_matmul(xt, g_mix, shift1, scale1, conv_w_in[j].astype(BF16), 3 * d, BF16, seq)[0]
            xt = _conv_mixer(bcv, conv_w[j], conv_w_out[j].astype(BF16), xt, gate1, seq)
        elif kind == 1:
            gx = _norm_matmul(xt, g_mix, shift1, scale1, lru_w_in[j].astype(BF16), lru_w_in.shape[2], BF16, seq)[0]
            y = _lru_core(gx, lru_conv_w[j], lru_conv_b[j][None, :], lru_gate_w[j].astype(BF16),
                          lru_gate_b[j][:, None, :], lru_a_param[j][None, :], bsz, seq)
            xt = _proj_residual(y, lru_w_out[j].astype(BF16), xt, gate1, seq)
        else:
            w_in = jnp.zeros((d, 3 * d + LANES), F32).at[:, :3 * d + FOX_HEADS].set(fox_w_in[j]).astype(BF16)
            qkv, f = _norm_matmul(xt, g_mix, shift1, scale1, w_in, 3 * d, BF16, seq)
            b_f = jnp.zeros((1, LANES), F32).at[0, :FOX_HEADS].set(fox_b_f[j])
            cum = _forget_cumsum(f, b_f, bsz, seq)
            o = _fox_attention(qkv, cum, bsz, seq)
            xt = _proj_residual(o, fox_w_out[j].astype(BF16), xt, gate1, seq)
        xt = _moe_layer(xt, norm_ffn_g[i][None, :], shift2, scale2, gate2, router_w[i], router_b[i],
                        i, moe_w_gate_up, moe_b_gate_up, moe_w_down, moe_b_down, seq)
    return _final_norm(xt, final_g[None, :]).reshape(bsz, seq, d)
```

```python
import functools

import jax
import jax.numpy as jnp
from jax import lax
from jax.experimental import pallas as pl
from jax.experimental.pallas import tpu as pltpu

F32 = jnp.float32
BF16 = jnp.bfloat16
I32 = jnp.int32
HIGHEST = lax.Precision.HIGHEST

RMS_EPS = 1e-6
LRU_HEADS = 4
LRU_C = 8.0
FOX_HEADS = 16
N_EXPERTS = 32
TOP_K = 4
SWIGLU_LIMIT = 7.0
SWIGLU_ALPHA = 1.702
NEG_BIG = -1e30
GELU_C = 0.7978845608028654
LOG2E = 1.4426950408889634

LANES = 128
SUBLANES = 8
VMEM_LIMIT = 56 * 1024 * 1024

ROW_TILE = 512
EXPERT_TILE = 256
DISPATCH_TILE = 1024
COMBINE_TILE = 256
ATTN_TILE = 512


def _params(*sem):
    return pltpu.CompilerParams(dimension_semantics=sem, vmem_limit_bytes=VMEM_LIMIT)


def _sigmoid(z):
    return 1.0 / (1.0 + jnp.exp(-z))


def _norm_mod(x, g, shift, scale):
    ms = jnp.mean(x * x, axis=-1, keepdims=True)
    y = x * lax.rsqrt(ms + RMS_EPS) * g
    return y * (1.0 + scale) + shift


def _shifted_rows(u, carry_ref, j, row):
    sh = pltpu.roll(u, j, 0)
    for r in range(j):
        src = SUBLANES - j + r
        sh = jnp.where(row == r, carry_ref[src:src + 1, :], sh)
    return sh


def _mod_kernel(c_ref, w_ref, b_ref, o_ref):
    c = c_ref[...]
    ca = c * _sigmoid(c)
    o_ref[0] = jnp.dot(ca, w_ref[0], precision=HIGHEST, preferred_element_type=F32) + b_ref[0]


def _modulation(c, w_mod, b_mod):
    depth, d, n = w_mod.shape
    bsz = c.shape[0]
    rows = -(-bsz // SUBLANES) * SUBLANES
    c_pad = jnp.zeros((rows, d), F32).at[:bsz].set(c)
    tn = n // 4
    out = pl.pallas_call(
        _mod_kernel,
        out_shape=jax.ShapeDtypeStruct((depth, rows, n), F32),
        grid=(depth, n // tn),
        in_specs=[
            pl.BlockSpec((rows, d), lambda l, j: (0, 0)),
            pl.BlockSpec((1, d, tn), lambda l, j: (l, 0, j)),
            pl.BlockSpec((1, 1, tn), lambda l, j: (l, 0, j)),
        ],
        out_specs=pl.BlockSpec((1, rows, tn), lambda l, j: (l, 0, j)),
        compiler_params=_params("parallel", "parallel"),
        name="adaln_modulation",
    )(c_pad, w_mod, b_mod.reshape(depth, 1, n))
    return out[:, :bsz].reshape(depth, bsz, 6, 1, d)


def _norm_matmul_kernel(x_ref, g_ref, sh_ref, sc_ref, w_ref, *out_refs, n_main):
    h = _norm_mod(x_ref[...], g_ref[...], sh_ref[...], sc_ref[...]).astype(BF16)
    y = jnp.dot(h, w_ref[...], preferred_element_type=F32)
    out_refs[0][...] = y[:, :n_main].astype(out_refs[0].dtype)
    if len(out_refs) > 1:
        out_refs[1][...] = y[:, n_main:]


def _norm_matmul(x, g, shift, scale, w, n_main, out_dtype, seq):
    t, d = x.shape
    n = w.shape[1]
    tm = ROW_TILE
    per_seq = seq // tm
    out_shape = [jax.ShapeDtypeStruct((t, n_main), out_dtype)]
    out_specs = [pl.BlockSpec((tm, n_main), lambda i: (i, 0))]
    if n > n_main:
        out_shape.append(jax.ShapeDtypeStruct((t, n - n_main), F32))
        out_specs.append(pl.BlockSpec((tm, n - n_main), lambda i: (i, 0)))
    return pl.pallas_call(
        functools.partial(_norm_matmul_kernel, n_main=n_main),
        out_shape=out_shape,
        grid=(t // tm,),
        in_specs=[
            pl.BlockSpec((tm, d), lambda i: (i, 0)),
            pl.BlockSpec((1, d), lambda i: (0, 0)),
            pl.BlockSpec((None, 1, d), lambda i: (i // per_seq, 0, 0)),
            pl.BlockSpec((None, 1, d), lambda i: (i // per_seq, 0, 0)),
            pl.BlockSpec((d, n), lambda i: (0, 0)),
        ],
        out_specs=out_specs,
        compiler_params=_params("parallel"),
        name="norm_mod_in_proj",
    )(x, g, shift, scale, w)


def _proj_residual_kernel(z_ref, w_ref, x_ref, gate_ref, o_ref):
    y = jnp.dot(z_ref[...].astype(BF16), w_ref[...], preferred_element_type=F32)
    o_ref[...] = x_ref[...] + gate_ref[...] * y


def _proj_residual(z, w, x, gate, seq):
    t, d = x.shape
    tm = ROW_TILE
    per_seq = seq // tm
    return pl.pallas_call(
        _proj_residual_kernel,
        out_shape=jax.ShapeDtypeStruct((t, d), F32),
        grid=(t // tm,),
        in_specs=[
            pl.BlockSpec((tm, z.shape[1]), lambda i: (i, 0)),
            pl.BlockSpec(w.shape, lambda i: (0, 0)),
            pl.BlockSpec((tm, d), lambda i: (i, 0)),
            pl.BlockSpec((None, 1, d), lambda i: (i // per_seq, 0, 0)),
        ],
        out_specs=pl.BlockSpec((tm, d), lambda i: (i, 0)),
        compiler_params=_params("parallel"),
        name="out_proj_residual",
    )(z, w, x, gate)


def _conv_mixer_kernel(b_ref, c_ref, v_ref, cw_ref, w_ref, x_ref, gate_ref, o_ref, carry_ref, *, per_seq):
    @pl.when(pl.program_id(0) % per_seq == 0)
    def _():
        carry_ref[...] = jnp.zeros_like(carry_ref)

    cv = c_ref[...].astype(F32) * v_ref[...].astype(F32)
    tm = cv.shape[0]
    k_w = cw_ref.shape[0]
    row = lax.broadcasted_iota(I32, cv.shape, 0)
    conv = cw_ref[k_w - 1:k_w, :] * cv
    for j in range(1, k_w):
        conv = conv + cw_ref[k_w - 1 - j:k_w - j, :] * _shifted_rows(cv, carry_ref, j, row)
    carry_ref[...] = cv[tm - SUBLANES:, :]
    z = (b_ref[...].astype(F32) * conv).astype(BF16)
    y = jnp.dot(z, w_ref[...], preferred_element_type=F32)
    o_ref[...] = x_ref[...] + gate_ref[...] * y


def _conv_mixer(bcv, conv_w, w_out, x, gate, seq):
    t, d = x.shape
    tm = ROW_TILE
    per_seq = seq // tm
    return pl.pallas_call(
        functools.partial(_conv_mixer_kernel, per_seq=per_seq),
        out_shape=jax.ShapeDtypeStruct((t, d), F32),
        grid=(t // tm,),
        in_specs=[
            pl.BlockSpec((tm, d), lambda i: (i, 0)),
            pl.BlockSpec((tm, d), lambda i: (i, 1)),
            pl.BlockSpec((tm, d), lambda i: (i, 2)),
            pl.BlockSpec(conv_w.shape, lambda i: (0, 0)),
            pl.BlockSpec(w_out.shape, lambda i: (0, 0)),
            pl.BlockSpec((tm, d), lambda i: (i, 0)),
            pl.BlockSpec((None, 1, d), lambda i: (i // per_seq, 0, 0)),
        ],
        out_specs=pl.BlockSpec((tm, d), lambda i: (i, 0)),
        scratch_shapes=[pltpu.VMEM((SUBLANES, d), F32)],
        compiler_params=_params("arbitrary"),
        name="short_conv_mixer",
    )(bcv, bcv, bcv, conv_w, w_out, x, gate)


def _lru_kernel(xb_ref, gb_ref, cw_ref, cb_ref, gw_ref, gbias_ref, ap_ref, y_ref, xcarry_ref, hcarry_ref):
    @pl.when(pl.program_id(2) == 0)
    def _():
        xcarry_ref[...] = jnp.zeros_like(xcarry_ref)
        hcarry_ref[...] = jnp.zeros_like(hcarry_ref)

    x = xb_ref[...].astype(F32)
    ts, blk = x.shape
    k_w = cw_ref.shape[0]
    row = lax.broadcasted_iota(I32, x.shape, 0)
    conv = cw_ref[k_w - 1:k_w, :] * x
    for j in range(1, k_w):
        conv = conv + cw_ref[k_w - 1 - j:k_w - j, :] * _shifted_rows(x, xcarry_ref, j, row)
    xcarry_ref[...] = x[ts - SUBLANES:, :]
    xc = conv + cb_ref[...]

    g = jnp.dot(xc.astype(BF16), gw_ref[...], preferred_element_type=F32) + gbias_ref[...]
    r_gate = _sigmoid(g[:, :blk])
    i_gate = _sigmoid(g[:, blk:])
    ap = ap_ref[...]
    softplus = jnp.maximum(ap, 0.0) + jnp.log(1.0 + jnp.exp(-jnp.abs(ap)))
    a = jnp.exp((-LRU_C) * r_gate * softplus)
    u = jnp.sqrt(1.0 - a * a) * (i_gate * xc)

    a_cum, h_loc = a, u
    in_group = jnp.bitwise_and(row, SUBLANES - 1)
    step = 1
    while step < SUBLANES:
        a_prev = jnp.where(in_group < step, 1.0, pltpu.roll(a_cum, step, 0))
        h_prev = jnp.where(in_group < step, 0.0, pltpu.roll(h_loc, step, 0))
        h_loc = h_loc + a_cum * h_prev
        a_cum = a_cum * a_prev
        step *= 2
    carry = hcarry_ref[...]
    groups = []
    for g in range(ts // SUBLANES):
        lo = g * SUBLANES
        h_group = h_loc[lo:lo + SUBLANES, :] + a_cum[lo:lo + SUBLANES, :] * carry
        carry = h_group[SUBLANES - 1:SUBLANES, :]
        groups.append(h_group)
    hs = jnp.concatenate(groups, axis=0)
    hcarry_ref[...] = carry

    gb = gb_ref[...].astype(F32)
    gelu = 0.5 * gb * (1.0 + jnp.tanh(GELU_C * (gb + 0.044715 * (gb * gb * gb))))
    y_ref[...] = (gelu * hs).astype(y_ref.dtype)


def _lru_core(gx, conv_w, conv_b, gate_w, gate_b, a_param, bsz, seq):
    t = gx.shape[0]
    width = gx.shape[1] // 2
    blk = width // LRU_HEADS
    ts = ROW_TILE
    per_seq = seq // ts
    return pl.pallas_call(
        _lru_kernel,
        out_shape=jax.ShapeDtypeStruct((t, width), BF16),
        grid=(bsz, LRU_HEADS, per_seq),
        in_specs=[
            pl.BlockSpec((ts, blk), lambda b, h, s: (b * per_seq + s, LRU_HEADS + h)),
            pl.BlockSpec((ts, blk), lambda b, h, s: (b * per_seq + s, h)),
            pl.BlockSpec((conv_w.shape[0], blk), lambda b, h, s: (0, h)),
            pl.BlockSpec((1, blk), lambda b, h, s: (0, h)),
            pl.BlockSpec((None, blk, 2 * blk), lambda b, h, s: (h, 0, 0)),
            pl.BlockSpec((None, 1, 2 * blk), lambda b, h, s: (h, 0, 0)),
            pl.BlockSpec((1, blk), lambda b, h, s: (0, h)),
        ],
        out_specs=pl.BlockSpec((ts, blk), lambda b, h, s: (b * per_seq + s, h)),
        scratch_shapes=[pltpu.VMEM((SUBLANES, blk), F32), pltpu.VMEM((1, blk), F32)],
        compiler_params=_params("parallel", "parallel", "arbitrary"),
        name="rglru_scan",
    )(gx, gx, conv_w, conv_b, gate_w, gate_b, a_param)


def _forget_cumsum_kernel(f_ref, bf_ref, o_ref):
    x = f_ref[...] + bf_ref[...]
    c = jnp.minimum(x, 0.0) - jnp.log(1.0 + jnp.exp(-jnp.abs(x)))
    seq = c.shape[0]
    row = lax.broadcasted_iota(I32, c.shape, 0)
    step = 1
    while step < seq:
        c = c + jnp.where(row < step, 0.0, pltpu.roll(c, step, 0))
        step *= 2
    o_ref[...] = c.T[:FOX_HEADS, :]


def _forget_cumsum(f, b_f, bsz, seq):
    return pl.pallas_call(
        _forget_cumsum_kernel,
        out_shape=jax.ShapeDtypeStruct((bsz, FOX_HEADS, seq), F32),
        grid=(bsz,),
        in_specs=[
            pl.BlockSpec((seq, LANES), lambda b: (b, 0)),
            pl.BlockSpec((1, LANES), lambda b: (0, 0)),
        ],
        out_specs=pl.BlockSpec((None, FOX_HEADS, seq), lambda b: (b, 0, 0)),
        compiler_params=_params("parallel"),
        name="forget_gate_cumsum",
    )(f, b_f)


def _fox_attn_kernel(q_ref, k_ref, v_ref, ck_ref, o_ref, m_ref, acc_ref, sa_ref, sb_ref, *, scale):
    qi = pl.program_id(2)
    tq = q_ref.shape[0]
    tk = tq
    half = q_ref.shape[1] // 2
    first_head = lax.broadcasted_iota(I32, (1, q_ref.shape[1]), 1) < half

    qs = (q_ref[...].astype(F32) * (scale * LOG2E)).astype(BF16)
    zero = jnp.zeros_like(qs)
    q_heads = (jnp.where(first_head, qs, zero), jnp.where(first_head, zero, qs))
    m_ref[...] = jnp.full_like(m_ref, NEG_BIG)
    acc_ref[...] = jnp.zeros_like(acc_ref)

    def scores(j, s_ref):
        start = pl.multiple_of(j * tk, tk)
        k = k_ref[pl.ds(start, tk), :]
        ck = ck_ref[:, pl.ds(start, tk)] * LOG2E
        for hh in range(2):
            s = lax.dot_general(q_heads[hh], k, (((1,), (1,)), ((), ())), preferred_element_type=F32)
            s_ref[hh] = s - ck[hh:hh + 1, :]

    def accumulate(j, s_ref, diagonal):
        start = pl.multiple_of(j * tk, tk)
        v = v_ref[pl.ds(start, tk), :]
        ones = jnp.ones_like(v)
        v_heads = (jnp.where(first_head, v, ones), jnp.where(first_head, ones, v))
        for hh in range(2):
            s = s_ref[hh]
            if diagonal:
                row = lax.broadcasted_iota(I32, s.shape, 0)
                col = lax.broadcasted_iota(I32, s.shape, 1)
                s = jnp.where(col <= row, s, NEG_BIG)
            m_prev = m_ref[hh]
            m_new = jnp.maximum(m_prev, jnp.max(s, axis=-1, keepdims=True))
            alpha = jnp.exp2(m_prev - m_new)
            p = jnp.exp2(s - jnp.concatenate([m_new] * (tk // LANES), axis=1))
            m_ref[hh] = m_new
            acc_ref[hh] = alpha * acc_ref[hh] + jnp.dot(p.astype(BF16), v_heads[hh], preferred_element_type=F32)

    scores(0, sa_ref)

    def pair(p, carry):
        j = 2 * p
        scores(j + 1, sb_ref)
        accumulate(j, sa_ref, False)
        scores(j + 2, sa_ref)
        accumulate(j + 1, sb_ref, False)
        return carry

    lax.fori_loop(0, qi // 2, pair, 0)

    @pl.when(qi % 2 == 0)
    def _():
        accumulate(qi, sa_ref, True)

    @pl.when(qi % 2 == 1)
    def _():
        scores(qi, sb_ref)
        accumulate(qi - 1, sa_ref, False)
        accumulate(qi, sb_ref, True)

    a0 = acc_ref[0]
    a1 = acc_ref[1]
    o0 = a0 / pltpu.roll(a0, half, 1)
    o1 = a1 / pltpu.roll(a1, half, 1)
    o_ref[...] = jnp.where(first_head, o0, o1).astype(o_ref.dtype)


def _fox_attention(qkv, cum, bsz, seq):
    t = qkv.shape[0]
    d = qkv.shape[1] // 3
    pairs = d // LANES
    tq = ATTN_TILE
    per_seq = seq // tq
    scale = (d // FOX_HEADS) ** -0.5
    ck = cum.reshape(bsz, pairs, 2, seq)
    return pl.pallas_call(
        functools.partial(_fox_attn_kernel, scale=scale),
        out_shape=jax.ShapeDtypeStruct((t, d), BF16),
        grid=(bsz, pairs, per_seq),
        in_specs=[
            pl.BlockSpec((tq, LANES), lambda b, hp, qi: (b * per_seq + qi, hp)),
            pl.BlockSpec((seq, LANES), lambda b, hp, qi: (b, pairs + hp)),
            pl.BlockSpec((seq, LANES), lambda b, hp, qi: (b, 2 * pairs + hp)),
            pl.BlockSpec((None, None, 2, seq), lambda b, hp, qi: (b, hp, 0, 0)),
        ],
        out_specs=pl.BlockSpec((tq, LANES), lambda b, hp, qi: (b * per_seq + qi, hp)),
        scratch_shapes=[
            pltpu.VMEM((2, tq, LANES), F32),
            pltpu.VMEM((2, tq, LANES), F32),
            pltpu.VMEM((2, tq, tq), F32),
            pltpu.VMEM((2, tq, tq), F32),
        ],
        compiler_params=_params("parallel", "parallel", "arbitrary"),
        name="forgetting_attention",
    )(qkv, qkv, qkv, ck)


def _router_kernel(x_ref, g_ref, sh_ref, sc_ref, rw_ref, rb_ref,
                   h_ref, e_ref, gt_ref, rk_ref, cnt_ref, carry_ref):
    @pl.when(pl.program_id(0) == 0)
    def _():
        carry_ref[...] = jnp.zeros_like(carry_ref)

    h = _norm_mod(x_ref[...], g_ref[...], sh_ref[...], sc_ref[...])
    h_ref[...] = h
    logits = jnp.dot(h, rw_ref[...], precision=HIGHEST, preferred_element_type=F32) + rb_ref[...]
    tm = logits.shape[0]
    lane = lax.broadcasted_iota(I32, logits.shape, 1).astype(F32)

    work = logits
    picked = jnp.zeros(logits.shape, F32)
    vals, idxs = [], []
    for _ in range(TOP_K):
        mx = jnp.max(work, axis=-1, keepdims=True)
        idx = jnp.min(jnp.where(work == mx, lane, float(LANES)), axis=-1, keepdims=True)
        sel = lane == idx
        vals.append(mx)
        idxs.append(idx)
        picked = jnp.where(sel, 1.0, picked)
        work = jnp.where(sel, -3.0e38, work)

    ex = [jnp.exp(v - vals[0]) for v in vals]
    den = ex[0] + ex[1] + ex[2] + ex[3]

    r_i = lax.broadcasted_iota(I32, (tm, tm), 0)
    c_i = lax.broadcasted_iota(I32, (tm, tm), 1)
    earlier = jnp.where(c_i < r_i, 1.0, 0.0).astype(BF16)
    before = jnp.dot(earlier, picked.astype(BF16), preferred_element_type=F32) + carry_ref[...]
    carry_ref[...] = carry_ref[...] + jnp.sum(picked, axis=0, keepdims=True)
    cnt_ref[...] = carry_ref[...]

    e_out = jnp.zeros(logits.shape, F32)
    g_out = jnp.zeros(logits.shape, F32)
    r_out = jnp.zeros(logits.shape, F32)
    for k in range(TOP_K):
        rank = jnp.sum(jnp.where(lane == idxs[k], before, 0.0), axis=-1, keepdims=True)
        at_k = lane == float(k)
        e_out = jnp.where(at_k, idxs[k], e_out)
        g_out = jnp.where(at_k, ex[k] / den, g_out)
        r_out = jnp.where(at_k, rank, r_out)
    e_ref[...] = e_out.astype(I32)
    gt_ref[...] = g_out
    rk_ref[...] = r_out.astype(I32)


def _router(x, g, shift, scale, rw, rb, seq):
    t, d = x.shape
    tm = ROW_TILE
    per_seq = seq // tm
    slab = lambda dt: jax.ShapeDtypeStruct((t, LANES), dt)
    slab_spec = pl.BlockSpec((tm, LANES), lambda i: (i, 0))
    return pl.pallas_call(
        _router_kernel,
        out_shape=[jax.ShapeDtypeStruct((t, d), F32), slab(I32), slab(F32), slab(I32),
                   jax.ShapeDtypeStruct((1, LANES), F32)],
        grid=(t // tm,),
        in_specs=[
            pl.BlockSpec((tm, d), lambda i: (i, 0)),
            pl.BlockSpec((1, d), lambda i: (0, 0)),
            pl.BlockSpec((None, 1, d), lambda i: (i // per_seq, 0, 0)),
            pl.BlockSpec((None, 1, d), lambda i: (i // per_seq, 0, 0)),
            pl.BlockSpec((d, LANES), lambda i: (0, 0)),
            pl.BlockSpec((1, LANES), lambda i: (0, 0)),
        ],
        out_specs=[pl.BlockSpec((tm, d), lambda i: (i, 0)), slab_spec, slab_spec, slab_spec,
                   pl.BlockSpec((1, LANES), lambda i: (0, 0))],
        scratch_shapes=[pltpu.VMEM((1, LANES), F32)],
        compiler_params=_params("arbitrary"),
        name="moe_router",
    )(x, g, shift, scale, rw, rb)


def _dispatch_kernel(dest_ref, pad_end_ref, padded_ref, h_ref, xs_hbm, zero_ref, zsem, sem, *, td):
    i = pl.program_id(0)
    tile = zero_ref.shape[0]
    n_tiles = xs_hbm.shape[0] // tile
    n_exp = pad_end_ref.shape[0]
    n_used = pad_end_ref[n_exp - 1] // tile

    def zero_tile(tile_idx):
        start = pl.multiple_of(tile_idx * tile, tile)
        return pltpu.make_async_copy(zero_ref, xs_hbm.at[pl.ds(start, tile)], zsem)

    def for_zeroed_tiles(fn):
        def per_expert(e, carry):
            @pl.when(padded_ref[e] > 0)
            def _():
                fn(zero_tile(pad_end_ref[e] // tile - 1))
            return carry
        lax.fori_loop(0, n_exp, per_expert, 0)

        def per_tail(tile_idx, carry):
            fn(zero_tile(tile_idx))
            return carry
        lax.fori_loop(n_used, n_tiles, per_tail, 0)

    @pl.when(i == 0)
    def _():
        zero_ref[...] = jnp.zeros_like(zero_ref)
        for_zeroed_tiles(lambda cp: cp.start())
        for_zeroed_tiles(lambda cp: cp.wait())

    def issue(jo, carry):
        for u in range(SUBLANES):
            j = jo * SUBLANES + u
            for k in range(TOP_K):
                dst = dest_ref[(i * td + j) * TOP_K + k]
                pltpu.make_async_copy(h_ref.at[jo, pl.ds(u, 1)], xs_hbm.at[pl.ds(dst, 1)],
                                      sem).start(priority=k % 2)
        return carry

    lax.fori_loop(0, td // SUBLANES, issue, 0)
    n_copied = td * TOP_K
    pltpu.make_async_copy(xs_hbm.at[pl.ds(0, n_copied)], xs_hbm.at[pl.ds(0, n_copied)], sem).wait()


def _dispatch(dest, pad_end, padded, h, n_rows):
    t, d = h.shape
    td = DISPATCH_TILE
    grid_spec = pltpu.PrefetchScalarGridSpec(
        num_scalar_prefetch=3,
        grid=(t // td,),
        in_specs=[pl.BlockSpec((td // SUBLANES, SUBLANES, d), lambda i, de, pe, pa: (i, 0, 0))],
        out_specs=pl.BlockSpec(memory_space=pl.ANY),
        scratch_shapes=[pltpu.VMEM((EXPERT_TILE, d), F32), pltpu.SemaphoreType.DMA, pltpu.SemaphoreType.DMA],
    )
    return pl.pallas_call(
        functools.partial(_dispatch_kernel, td=td),
        out_shape=jax.ShapeDtypeStruct((n_rows, d), F32),
        grid_spec=grid_spec,
        compiler_params=_params("arbitrary"),
        name="moe_dispatch",
    )(dest, pad_end, padded, h.reshape(t // SUBLANES, SUBLANES, d))


def _expert_kernel(te_ref, nu_ref, xs_ref, wgu_ref, bgu_ref, wd_ref, bd_ref, y_ref, wgu_bf, wd_bf):
    i = pl.program_id(0)

    @pl.when(i >= nu_ref[0])
    def _():
        y_ref[...] = jnp.zeros_like(y_ref)

    @pl.when(jnp.logical_or(i == 0, te_ref[i] != te_ref[jnp.maximum(i - 1, 0)]))
    def _():
        wgu_bf[...] = wgu_ref[...].astype(BF16)
        wd_bf[...] = wd_ref[...].astype(BF16)

    @pl.when(i < nu_ref[0])
    def _():
        f = wd_ref.shape[0]
        gu = jnp.dot(xs_ref[...].astype(BF16), wgu_bf[...], preferred_element_type=F32) + bgu_ref[...]
        gate = jnp.minimum(gu[:, :f], SWIGLU_LIMIT)
        up = jnp.clip(gu[:, f:], -SWIGLU_LIMIT, SWIGLU_LIMIT)
        act = gate * _sigmoid(SWIGLU_ALPHA * gate) * (up + 1.0)
        y_ref[...] = jnp.dot(act.astype(BF16), wd_bf[...], preferred_element_type=F32) + bd_ref[...]


def _expert_mlp(tile_e, n_used, xs, layer, wgu, bgu, wd, bd):
    n_rows, d = xs.shape
    tile = EXPERT_TILE
    f = wd.shape[2]
    row_map = lambda i, te, nu: (jnp.minimum(i, nu[0] - 1), 0)
    exp_map = lambda i, te, nu: (layer, te[i], 0, 0)
    grid_spec = pltpu.PrefetchScalarGridSpec(
        num_scalar_prefetch=2,
        grid=(n_rows // tile,),
        in_specs=[
            pl.BlockSpec((tile, d), row_map),
            pl.BlockSpec((None, None, d, 2 * f), exp_map),
            pl.BlockSpec((None, None, 1, 2 * f), exp_map),
            pl.BlockSpec((None, None, f, d), exp_map),
            pl.BlockSpec((None, None, 1, d), exp_map),
        ],
        out_specs=pl.BlockSpec((tile, d), lambda i, te, nu: (i, 0)),
        scratch_shapes=[pltpu.VMEM((d, 2 * f), BF16), pltpu.VMEM((f, d), BF16)],
    )
    return pl.pallas_call(
        _expert_kernel,
        out_shape=jax.ShapeDtypeStruct((n_rows, d), F32),
        grid_spec=grid_spec,
        compiler_params=_params("arbitrary"),
        name="moe_expert_mlp",
    )(tile_e, n_used, xs, wgu, bgu[:, :, None, :], wd, bd[:, :, None, :])


def _combine_kernel(dest_ref, x_ref, gt_ref, gate_ref, yb_hbm, o_ref, buf_ref, sem, *, tc):
    i = pl.program_id(0)
    n = pl.num_programs(0)

    def issue(step, slot):
        def body(jo, carry):
            for u in range(SUBLANES):
                tok = step * tc + jo * SUBLANES + u
                for k in range(TOP_K):
                    pltpu.make_async_copy(yb_hbm.at[pl.ds(dest_ref[tok * TOP_K + k], 1)],
                                          buf_ref.at[slot, k, jo, pl.ds(u, 1)],
                                          sem.at[slot]).start(priority=k % 2)
            return carry
        lax.fori_loop(0, tc // SUBLANES, body, 0)

    @pl.when(i == 0)
    def _():
        issue(0, 0)

    @pl.when(i + 1 < n)
    def _():
        issue(i + 1, (i + 1) % 2)

    slot = i % 2

    for k in range(TOP_K):
        plane = buf_ref.at[slot, k]
        pltpu.make_async_copy(plane, plane, sem.at[slot]).wait()

    gates = gt_ref[...]
    d = x_ref.shape[1]
    y = gates[:, 0:1] * buf_ref[slot, 0].reshape(tc, d)
    for k in range(1, TOP_K):
        y = y + gates[:, k:k + 1] * buf_ref[slot, k].reshape(tc, d)
    o_ref[...] = x_ref[...] + gate_ref[...] * y


def _combine(dest, x, gates, gate2, yb, seq):
    t, d = x.shape
    tc = COMBINE_TILE
    per_seq = seq // tc
    grid_spec = pltpu.PrefetchScalarGridSpec(
        num_scalar_prefetch=1,
        grid=(t // tc,),
        in_specs=[
            pl.BlockSpec((tc, d), lambda i, dr: (i, 0)),
            pl.BlockSpec((tc, LANES), lambda i, dr: (i, 0)),
            pl.BlockSpec((None, 1, d), lambda i, dr: (i // per_seq, 0, 0)),
            pl.BlockSpec(memory_space=pl.ANY),
        ],
        out_specs=pl.BlockSpec((tc, d), lambda i, dr: (i, 0)),
        scratch_shapes=[pltpu.VMEM((2, TOP_K, tc // SUBLANES, SUBLANES, d), F32), pltpu.SemaphoreType.DMA((2,))],
    )
    return pl.pallas_call(
        functools.partial(_combine_kernel, tc=tc),
        out_shape=jax.ShapeDtypeStruct((t, d), F32),
        grid_spec=grid_spec,
        compiler_params=_params("arbitrary"),
        name="moe_combine_residual",
    )(dest, x, gates, gate2, yb)


def _moe_layer(x, g, shift, scale, gate2, rw, rb, layer, wgu, bgu, wd, bd, seq):
    t, d = x.shape
    n_exp = rw.shape[1]
    rw_pad = jnp.zeros((d, LANES), F32).at[:, :n_exp].set(rw)
    rb_pad = jnp.full((1, LANES), NEG_BIG, F32).at[0, :n_exp].set(rb)
    h, e_slab, gt_slab, rk_slab, cnt = _router(x, g, shift, scale, rw_pad, rb_pad, seq)

    tile = EXPERT_TILE
    n_tiles = t * TOP_K // tile + n_exp
    counts = cnt[0, :n_exp].astype(I32)
    padded = (counts + tile - 1) // tile * tile
    pad_end = jnp.cumsum(padded)
    pad_start = pad_end - padded
    dest = (pad_start[e_slab[:, :TOP_K]] + rk_slab[:, :TOP_K]).reshape(-1)
    n_used = pad_end[-1] // tile
    tile_ids = jnp.arange(n_tiles, dtype=I32)
    first_row = jnp.minimum(tile_ids, n_used - 1) * tile
    tile_e = jnp.sum((pad_end[None, :] <= first_row[:, None]).astype(I32), axis=1)

    xs = _dispatch(dest, pad_end, padded, h, n_tiles * tile)
    yb = _expert_mlp(tile_e, n_used.reshape(1), xs, layer, wgu, bgu, wd, bd)
    return _combine(dest, x, gt_slab, gate2, yb, seq)


def _final_norm_kernel(x_ref, g_ref, o_ref):
    x = x_ref[...]
    ms = jnp.mean(x * x, axis=-1, keepdims=True)
    o_ref[...] = x * lax.rsqrt(ms + RMS_EPS) * g_ref[...]


def _final_norm(x, g):
    t, d = x.shape
    tm = ROW_TILE
    return pl.pallas_call(
        _final_norm_kernel,
        out_shape=jax.ShapeDtypeStruct((t, d), F32),
        grid=(t // tm,),
        in_specs=[pl.BlockSpec((tm, d), lambda i: (i, 0)), pl.BlockSpec((1, d), lambda i: (0, 0))],
        out_specs=pl.BlockSpec((tm, d), lambda i: (i, 0)),
        compiler_params=_params("parallel"),
        name="final_rmsnorm",
    )(x, g)


def kernel(x, c, norm_mix_g, norm_ffn_g, w_mod, b_mod, conv_w_in, conv_w, conv_w_out, lru_w_in, lru_conv_w, lru_conv_b, lru_gate_w, lru_gate_b, lru_a_param, lru_w_out, fox_w_in, fox_b_f, fox_w_out, router_w, router_b, moe_w_gate_up, moe_b_gate_up, moe_w_down, moe_b_down, final_g):
    bsz, seq, d = x.shape
    depth = w_mod.shape[0]
    t = bsz * seq
    mod = _modulation(c, w_mod, b_mod)
    xt = x.reshape(t, d)
    for i in range(depth):
        shift1, scale1, gate1 = mod[i, :, 0], mod[i, :, 1], mod[i, :, 2]
        shift2, scale2, gate2 = mod[i, :, 3], mod[i, :, 4], mod[i, :, 5]
        g_mix = norm_mix_g[i][None, :]
        kind, j = i % 3, i // 3
        if kind == 0:
            bcv = _norm_matmul(xt, g_mix, shift1, scale1, conv_w_in[j].astype(BF16), 3 * d, BF16, seq)[0]
            xt = _conv_mixer(bcv, conv_w[j], conv_w_out[j].astype(BF16), xt, gate1, seq)
        elif kind == 1:
            gx = _norm_matmul(xt, g_mix, shift1, scale1, lru_w_in[j].astype(BF16), lru_w_in.shape[2], BF16, seq)[0]
            y = _lru_core(gx, lru_conv_w[j], lru_conv_b[j][None, :], lru_gate_w[j].astype(BF16),
                          lru_gate_b[j][:, None, :], lru_a_param[j][None, :], bsz, seq)
            xt = _proj_residual(y, lru_w_out[j].astype(BF16), xt, gate1, seq)
        else:
            w_in = jnp.zeros((d, 3 * d + LANES), F32).at[:, :3 * d + FOX_HEADS].set(fox_w_in[j]).astype(BF16)
            qkv, f = _norm_matmul(xt, g_mix, shift1, scale1, w_in, 3 * d, BF16, seq)
            b_f = jnp.zeros((1, LANES), F32).at[0, :FOX_HEADS].set(fox_b_f[j])
            cum = _forget_cumsum(f, b_f, bsz, seq)
            o = _fox_attention(qkv, cum, bsz, seq)
            xt = _proj_residual(o, fox_w_out[j].astype(BF16), xt, gate1, seq)
        xt = _moe_layer(xt, norm_ffn_g[i][None, :], shift2, scale2, gate2, router_w[i], router_b[i],
                        i, moe_w_gate_up, moe_b_gate_up, moe_w_down, moe_b_down, seq)
    return _final_norm(xt, final_g[None, :]).reshape(bsz, seq, d)
```
